```python
import jax, jax.numpy as jnp
from jax import lax
import numpy as np

D_MODEL = 2048
BATCH = 1
SEQ = 8192
DEPTH = 1

SSD_HEADS = 32
SSD_HEAD_DIM = 64
SSD_WIDTH = SSD_HEADS * SSD_HEAD_DIM
SSD_GROUPS = 4
SSD_STATE = 128
SSD_CONV = 4
SSD_CHUNK = 128
SSD_CONV_WIDTH = SSD_WIDTH + 2 * SSD_GROUPS * SSD_STATE
MLSTM_HEADS = 8
MLSTM_QK_DIM = 128
MLSTM_V_DIM = 256
MLSTM_WIDTH = MLSTM_HEADS * MLSTM_V_DIM
MLSTM_CHUNK = 64
GATE_SOFTCAP = 15.0
MIX_WIDTH = SSD_WIDTH + MLSTM_WIDTH
D_FF = -(-8 * D_MODEL // (3 * 256)) * 256
EPS = 1e-6
IN_SIZES = (SSD_WIDTH, SSD_CONV_WIDTH, SSD_HEADS,
            MLSTM_HEADS * MLSTM_QK_DIM, MLSTM_HEADS * MLSTM_QK_DIM,
            MLSTM_WIDTH, MLSTM_WIDTH, MLSTM_HEADS, MLSTM_HEADS)
IN_WIDTH = sum(IN_SIZES)
SPLIT_POINTS = tuple(int(s) for s in np.cumsum(IN_SIZES)[:-1])

kernel_name = "hymba_ssd_mlstm_swiglu_layer"


def rmsnorm(x, g):
    xf = x.astype(jnp.float32)
    y = xf * lax.rsqrt(jnp.mean(xf * xf, axis=-1, keepdims=True) + EPS)
    return (y * g.astype(jnp.float32)).astype(x.dtype)


def softcap(x, cap):
    return cap * jnp.tanh(x / cap)


def ssd_mixer(z, xbc, dt_raw, conv_w, conv_b, dt_bias, a_log, d_skip, norm_g):
    b, L, _ = xbc.shape
    G, R, P, N, CH = SSD_GROUPS, SSD_HEADS // SSD_GROUPS, SSD_HEAD_DIM, SSD_STATE, SSD_CHUNK
    nc = L // CH
    f32 = jnp.float32
    xbc = xbc.astype(f32)
    pad = jnp.pad(xbc, ((0, 0), (SSD_CONV - 1, 0), (0, 0)))
    conv = conv_b.astype(f32)
    for tap in range(SSD_CONV):
        conv = conv + pad[:, tap:tap + L] * conv_w[tap].astype(f32)
    xbc = jax.nn.silu(conv)
    xs, Bm, Cm = jnp.split(xbc, [SSD_WIDTH, SSD_WIDTH + G * N], axis=-1)
    xs = xs.reshape(b, nc, CH, G, R, P)
    Bm = Bm.reshape(b, nc, CH, G, N)
    Cm = Cm.reshape(b, nc, CH, G, N)
    dt = jax.nn.softplus(dt_raw.astype(f32) + dt_bias.astype(f32))
    A = -jnp.exp(a_log.astype(f32))
    dt_c = dt.reshape(b, nc, CH, G, R)
    a_dt = (dt_c * A.reshape(G, R)).transpose(0, 3, 4, 1, 2)
    a_cum = jnp.cumsum(a_dt, axis=-1)
    X = xs * dt_c[..., None]
    seg = a_cum[..., :, None] - a_cum[..., None, :]
    causal = jnp.tril(jnp.ones((CH, CH), dtype=bool))
    Lmat = jnp.exp(jnp.where(causal, seg, -jnp.inf))
    cb = jnp.einsum("bclgn,bcsgn->bcgls", Cm, Bm)
    y_diag = jnp.einsum("bcgls,bgrcls,bcsgrp->bclgrp", cb, Lmat, X)
    decay_states = jnp.exp(a_cum[..., -1:] - a_cum)
    states = jnp.einsum("bclgn,bgrcl,bclgrp->cbgrpn", Bm, decay_states, X)
    chunk_decay = jnp.exp(a_cum[..., -1]).transpose(3, 0, 1, 2)

    def step(carry, inp):
        st, dec = inp
        return carry * dec[..., None, None] + st, carry

    _, prev_states = lax.scan(step, jnp.zeros((b, G, R, P, N), f32), (states, chunk_decay))
    y_off = jnp.einsum("bclgn,cbgrpn,bgrcl->bclgrp", Cm, prev_states, jnp.exp(a_cum))
    y = y_diag + y_off + d_skip.astype(f32).reshape(G, R)[:, :, None] * xs
    y = y.reshape(b, L, SSD_WIDTH) * jax.nn.silu(z.astype(f32))
    yg = y.reshape(b, L, G, SSD_WIDTH // G)
    yg = yg * lax.rsqrt(jnp.mean(yg * yg, axis=-1, keepdims=True) + EPS)
    y = yg.reshape(b, L, SSD_WIDTH) * norm_g.astype(f32)
    return y.astype(z.dtype)


def mlstm_mixer(q, k, v, o_raw, i_raw, f_raw, i_bias, f_bias, norm_g):
    b, L, _ = q.shape
    H, DK, DV, CH = MLSTM_HEADS, MLSTM_QK_DIM, MLSTM_V_DIM, MLSTM_CHUNK
    nc = L // CH
    f32 = jnp.float32
    q = q.astype(f32).reshape(b, nc, CH, H, DK)
    k = k.astype(f32).reshape(b, nc, CH, H, DK) * (DK ** -0.5)
    v = v.astype(f32).reshape(b, nc, CH, H, DV)
    log_i = softcap(i_raw.astype(f32) + i_bias.astype(f32), GATE_SOFTCAP)
    log_f = jax.nn.log_sigmoid(softcap(f_raw.astype(f32) + f_bias.astype(f32), GATE_SOFTCAP))
    log_i = log_i.reshape(b, nc, CH, H).transpose(0, 3, 1, 2)
    log_f = log_f.reshape(b, nc, CH, H).transpose(0, 3, 1, 2)
    bcum = jnp.cumsum(log_f, axis=-1)
    b_last = bcum[..., -1]
    a = b_last[..., None] - bcum + log_i
    m_loc = jnp.max(a, axis=-1)
    w = jnp.exp(a - m_loc[..., None])
    C_loc = jnp.einsum("bhcs,bcshv,bcshk->cbhvk", w, v, k)
    n_loc = jnp.einsum("bhcs,bcshk->cbhk", w, k)

    def step(carry, inp):
        C, n, m = carry
        Cl, nl, bl, ml = inp
        m_new = jnp.maximum(bl + m, ml)
        s_old = jnp.exp(bl + m - m_new)
        s_new = jnp.exp(ml - m_new)
        C_new = s_old[..., None, None] * C + s_new[..., None, None] * Cl
        n_new = s_old[..., None] * n + s_new[..., None] * nl
        return (C_new, n_new, m_new), (C, n, m)

    init = (jnp.zeros((b, H, DV, DK), f32), jnp.zeros((b, H, DK), f32), jnp.zeros((b, H), f32))
    _, (prev_C, prev_n, prev_m) = lax.scan(
        step, init, (C_loc, n_loc, b_last.transpose(2, 0, 1), m_loc.transpose(2, 0, 1)))
    D = bcum[..., :, None] - bcum[..., None, :] + log_i[..., None, :]
    causal = jnp.tril(jnp.ones((CH, CH), dtype=bool))
    D = jnp.where(causal, D, -jnp.inf)
    m_intra = jnp.max(D, axis=-1)
    inter_log = bcum + prev_m.transpose(1, 2, 0)[..., None]
    m_t = jnp.maximum(inter_log, m_intra)
    S = jnp.einsum("bclhk,bcshk->bhcls", q, k) * jnp.exp(D - m_t[..., None])
    w_inter = jnp.exp(inter_log - m_t)
    w_inter_t = w_inter.transpose(0, 2, 3, 1)[..., None]
    num = (jnp.einsum("bhcls,bcshv->bclhv", S, v)
           + w_inter_t * jnp.einsum("bclhk,cbhvk->bclhv", q, prev_C))
    nq = jnp.sum(S, axis=-1) + w_inter * jnp.einsum("bclhk,cbhk->bhcl", q, prev_n)
    den = jnp.maximum(jnp.abs(nq), jnp.exp(-m_t))
    h = num / den.transpose(0, 2, 3, 1)[..., None]
    h = h * lax.rsqrt(jnp.mean(h * h, axis=-1, keepdims=True) + EPS)
    h = h.reshape(b, L, MLSTM_WIDTH) * norm_g.astype(f32)
    h = h * jax.nn.sigmoid(o_raw.astype(f32))
    return h.astype(o_raw.dtype)


def setup_inputs(seed: int = 0) -> dict:
    key = jax.random.key(seed)
    ks = jax.random.split(key, 20)
    f32 = jnp.float32
    nrm = lambda k, shape, scale: jax.random.normal(k, shape, f32) * scale
    x = jax.random.normal(ks[0], (BATCH, SEQ, D_MODEL), f32)
    norm_mix_g = 1.0 + nrm(ks[1], (DEPTH, D_MODEL), 0.02)
    w_in = nrm(ks[2], (DEPTH, D_MODEL, IN_WIDTH), D_MODEL ** -0.5)
    conv_w = nrm(ks[3], (DEPTH, SSD_CONV, SSD_CONV_WIDTH), SSD_CONV ** -0.5)
    conv_b = nrm(ks[4], (DEPTH, SSD_CONV_WIDTH), 0.02)
    dt0 = jnp.exp(jax.random.uniform(ks[5], (DEPTH, SSD_HEADS), f32,
                                     jnp.log(0.001), jnp.log(0.1)))
    dt_bias = dt0 + jnp.log(-jnp.expm1(-dt0))
    a_log = jnp.log(jax.random.uniform(ks[6], (DEPTH, SSD_HEADS), f32, 1.0, 16.0))
    d_skip = 1.0 + nrm(ks[7], (DEPTH, SSD_HEADS), 0.02)
    ssd_norm_g = 1.0 + nrm(ks[8], (DEPTH, SSD_WIDTH), 0.02)
    i_bias = nrm(ks[9], (DEPTH, MLSTM_HEADS), 0.1)
    f_bias = jnp.linspace(3.0, 6.0, MLSTM_HEADS, dtype=f32)[None] + nrm(ks[10], (DEPTH, MLSTM_HEADS), 0.1)
    mlstm_norm_g = 1.0 + nrm(ks[11], (DEPTH, MLSTM_WIDTH), 0.02)
    w_out = nrm(ks[12], (DEPTH, MIX_WIDTH, D_MODEL), MIX_WIDTH ** -0.5)
    norm_ffn_g = 1.0 + nrm(ks[13], (DEPTH, D_MODEL), 0.02)
    w_gate = nrm(ks[14], (DEPTH, D_MODEL, D_FF), D_MODEL ** -0.5)
    w_up = nrm(ks[15], (DEPTH, D_MODEL, D_FF), D_MODEL ** -0.5)
    w_down = nrm(ks[16], (DEPTH, D_FF, D_MODEL), D_FF ** -0.5)
    final_norm_g = 1.0 + nrm(ks[17], (D_MODEL,), 0.02)
    return {"x": x, "norm_mix_g": norm_mix_g, "w_in": w_in, "conv_w": conv_w,
            "conv_b": conv_b, "dt_bias": dt_bias, "a_log": a_log, "d_skip": d_skip,
            "ssd_norm_g": ssd_norm_g, "i_bias": i_bias, "f_bias": f_bias,
            "mlstm_norm_g": mlstm_norm_g, "w_out": w_out, "norm_ffn_g": norm_ffn_g,
            "w_gate": w_gate, "w_up": w_up, "w_down": w_down, "final_norm_g": final_norm_g}


def reference(x, norm_mix_g, w_in, conv_w, conv_b, dt_bias, a_log, d_skip, ssd_norm_g,
              i_bias, f_bias, mlstm_norm_g, w_out, norm_ffn_g, w_gate, w_up, w_down,
              final_norm_g):
    h = x
    for l in range(DEPTH):
        u = rmsnorm(h, norm_mix_g[l])
        proj = jnp.einsum("bsd,de->bse", u, w_in[l])
        z, xbc, dt_raw, q, k, v, o_raw, i_raw, f_raw = jnp.split(proj, SPLIT_POINTS, axis=-1)
        y_ssd = ssd_mixer(z, xbc, dt_raw, conv_w[l], conv_b[l], dt_bias[l], a_log[l],
                          d_skip[l], ssd_norm_g[l])
        y_ml = mlstm_mixer(q, k, v, o_raw, i_raw, f_raw, i_bias[l], f_bias[l], mlstm_norm_g[l])
        y_mix = jnp.concatenate([y_ssd, y_ml], axis=-1)
        h = h + jnp.einsum("bse,ed->bsd", y_mix, w_out[l])
        u = rmsnorm(h, norm_ffn_g[l])
        gate = jnp.einsum("bsd,df->bsf", u, w_gate[l])
        up = jnp.einsum("bsd,df->bsf", u, w_up[l])
        h = h + jnp.einsum("bsf,fd->bsd", jax.nn.silu(gate) * up, w_down[l])
    return rmsnorm(h, final_norm_g)
```

```python
import functools

import jax
import jax.numpy as jnp
from jax import lax
from jax.experimental import pallas as pl
from jax.experimental.pallas import tpu as pltpu

F32 = jnp.float32
BF16 = jnp.bfloat16
EPS = 1e-6

V7X_LANES = 128
V7X_SUBLANES = 8
V7X_VMEM_BYTES = 64 * 1024 * 1024
VMEM_LIMIT_BYTES = 56 * 1024 * 1024

SSD_HEADS = 32
SSD_HEAD_DIM = 64
SSD_WIDTH = SSD_HEADS * SSD_HEAD_DIM
SSD_GROUPS = 4
SSD_STATE = 128
SSD_CONV = 4
SSD_GROUP_WIDTH = SSD_WIDTH // SSD_GROUPS
SSD_BC_WIDTH = SSD_GROUPS * SSD_STATE
SSD_CONV_WIDTH = SSD_WIDTH + 2 * SSD_BC_WIDTH
MLSTM_HEADS = 8
MLSTM_QK_DIM = 128
MLSTM_V_DIM = 256
MLSTM_WIDTH = MLSTM_HEADS * MLSTM_V_DIM
MLSTM_QK_WIDTH = MLSTM_HEADS * MLSTM_QK_DIM
GATE_SOFTCAP = 15.0

DT_LANE = 0
I_LANE = SSD_HEADS
F_LANE = SSD_HEADS + MLSTM_HEADS
SMALL_WIDTH = V7X_LANES

SSD_CHUNK = 128
MLSTM_CHUNK = 256
PROJ_TM = 1024
PROJ_TN = 1024
OUT_TM = 1024
OUT_TN = 512
FFN_TM = 512
FFN_TF = 512
NORM_ROWS = 128


def _dot(a, b):
    return jnp.dot(a, b, preferred_element_type=F32)


def _dot_nt(a, b):
    return lax.dot_general(a, b, (((1,), (1,)), ((), ())), preferred_element_type=F32)


def _split3(x):
    hi = x.astype(BF16)
    r1 = x - hi.astype(F32)
    mid = r1.astype(BF16)
    lo = (r1 - mid.astype(F32)).astype(BF16)
    return hi, mid, lo


def _dot01_left(m01, x):
    hi, mid, lo = _split3(x)
    return _dot(m01, hi) + _dot(m01, mid) + _dot(m01, lo)


def _dot01_right(x, m01):
    hi, mid, lo = _split3(x)
    return _dot(hi, m01) + _dot(mid, m01) + _dot(lo, m01)


def _sigmoid(x):
    return 1.0 / (1.0 + jnp.exp(-x))


def _softplus(x):
    return jnp.maximum(x, 0.0) + jnp.log1p(jnp.exp(-jnp.abs(x)))


def _log_sigmoid(x):
    return jnp.minimum(x, 0.0) - jnp.log1p(jnp.exp(-jnp.abs(x)))


def _softcap(x):
    return GATE_SOFTCAP * jnp.tanh(x / GATE_SOFTCAP)


def _rmsnorm_rows(src_ref, g_ref, dst_ref, rows):
    g = g_ref[...]

    def body(i, carry):
        r0 = pl.multiple_of(i * NORM_ROWS, NORM_ROWS)
        x = src_ref[pl.ds(r0, NORM_ROWS), :].astype(F32)
        y = x * lax.rsqrt(jnp.mean(x * x, axis=-1, keepdims=True) + EPS)
        dst_ref[pl.ds(r0, NORM_ROWS), :] = (y * g).astype(dst_ref.dtype)
        return carry

    lax.fori_loop(0, rows // NORM_ROWS, body, 0)


def _in_proj_kernel(x_ref, g_ref, w_ref, ws_ref, proj_ref, small_ref, u_ref):
    @pl.when(pl.program_id(1) == 0)
    def _():
        _rmsnorm_rows(x_ref, g_ref, u_ref, x_ref.shape[0])
        small_ref[...] = _dot(u_ref[...], ws_ref[...])

    proj_ref[...] = _dot(u_ref[...], w_ref[...]).astype(proj_ref.dtype)


def _in_proj(x, g, w_big, w_small):
    seq, d = x.shape
    n_big = w_big.shape[1]
    grid = (seq // PROJ_TM, n_big // PROJ_TN)
    return pl.pallas_call(
        _in_proj_kernel,
        grid=grid,
        in_specs=[
            pl.BlockSpec((PROJ_TM, d), lambda m, n: (m, 0)),
            pl.BlockSpec((1, d), lambda m, n: (0, 0)),
            pl.BlockSpec((d, PROJ_TN), lambda m, n: (0, n)),
            pl.BlockSpec((d, SMALL_WIDTH), lambda m, n: (0, 0)),
        ],
        out_specs=[
            pl.BlockSpec((PROJ_TM, PROJ_TN), lambda m, n: (m, n)),
            pl.BlockSpec((PROJ_TM, SMALL_WIDTH), lambda m, n: (m, 0)),
        ],
        out_shape=[
            jax.ShapeDtypeStruct((seq, n_big), BF16),
            jax.ShapeDtypeStruct((seq, SMALL_WIDTH), F32),
        ],
        scratch_shapes=[pltpu.VMEM((PROJ_TM, d), BF16)],
        compiler_params=pltpu.CompilerParams(
            dimension_semantics=("arbitrary", "arbitrary"),
            vmem_limit_bytes=VMEM_LIMIT_BYTES),
        name="in_proj",
    )(x, g, w_big, w_small)


def _ssd_kernel(z_ref, xbc_ref, small_ref, convw_ref, convb_ref, dtb_ref, alog_ref,
                dskip_ref, ng_ref, y_ref, xpad_ref, state_ref):
    T = SSD_CHUNK
    P = SSD_HEAD_DIM
    NS = SSD_STATE
    GW = SSD_GROUP_WIDTH
    PAD = V7X_SUBLANES

    @pl.when(pl.program_id(0) == 0)
    def _():
        xpad_ref[0:PAD, :] = jnp.zeros((PAD, SSD_CONV_WIDTH), F32)
        state_ref[...] = jnp.zeros(state_ref.shape, F32)

    xpad_ref[PAD:PAD + T, :] = xbc_ref[...].astype(F32)
    conv = convb_ref[...]
    for tap in range(SSD_CONV):
        off = PAD - (SSD_CONV - 1) + tap
        conv = conv + xpad_ref[off:off + T, :] * convw_ref[tap:tap + 1, :]
    xpad_ref[0:PAD, :] = xpad_ref[T:T + PAD, :]
    act = conv * _sigmoid(conv)

    lane = lax.broadcasted_iota(jnp.int32, (T, SMALL_WIDTH), 1)
    head_lane = lane < SSD_HEADS
    dt = jnp.where(head_lane, _softplus(small_ref[...] + dtb_ref[...]), 0.0)
    a_dt = dt * (-jnp.exp(alog_ref[...]))
    row_i = lax.broadcasted_iota(jnp.int32, (T, T), 0)
    col_i = lax.broadcasted_iota(jnp.int32, (T, T), 1)
    causal = row_i >= col_i
    tri = causal.astype(BF16)
    tri_t = (row_i <= col_i).astype(BF16)
    a_cum = _dot01_left(tri, a_dt)
    a_cum_t = _dot01_right(a_dt.T, tri_t)
    dt_t = dt.T
    a_last = a_cum[T - 1:T, :]
    w2 = dt * jnp.exp(a_last - a_cum)

    e_row = lax.broadcasted_iota(jnp.int32, (SMALL_WIDTH, SSD_WIDTH), 0)
    e_col = lax.broadcasted_iota(jnp.int32, (SMALL_WIDTH, SSD_WIDTH), 1)
    expand = (e_row == e_col // P).astype(BF16)
    decay_wide = _dot01_right(
        jnp.broadcast_to(jnp.exp(a_last), (V7X_SUBLANES, SMALL_WIDTH)), expand)[0:1, :]

    lane_t = lax.broadcasted_iota(jnp.int32, (T, V7X_LANES), 1)
    first_head = lane_t < P

    for g in range(SSD_GROUPS):
        xs_g = act[:, g * GW:(g + 1) * GW]
        b_f = act[:, SSD_WIDTH + g * NS:SSD_WIDTH + (g + 1) * NS]
        c_f = act[:, SSD_WIDTH + SSD_BC_WIDTH + g * NS:SSD_WIDTH + SSD_BC_WIDTH + (g + 1) * NS]
        b_bf = b_f.astype(BF16)
        c_bf = c_f.astype(BF16)
        cb = _dot_nt(c_bf, b_bf)
        state_g = state_ref[g]
        y_off = _dot(c_bf, state_g.astype(BF16))

        y_pairs = []
        xd_pairs = []
        for j in range(GW // V7X_LANES):
            ha = g * (GW // P) + 2 * j
            hb = ha + 1
            col_a = a_cum[:, ha:ha + 1]
            col_b = a_cum[:, hb:hb + 1]
            seg_a = jnp.where(causal, col_a - a_cum_t[ha:ha + 1, :], -jnp.inf)
            seg_b = jnp.where(causal, col_b - a_cum_t[hb:hb + 1, :], -jnp.inf)
            m_a = (cb * jnp.exp(seg_a) * dt_t[ha:ha + 1, :]).astype(BF16)
            m_b = (cb * jnp.exp(seg_b) * dt_t[hb:hb + 1, :]).astype(BF16)
            xp = xs_g[:, j * V7X_LANES:(j + 1) * V7X_LANES]
            x_a = jnp.where(first_head, xp, 0.0).astype(BF16)
            x_b = jnp.where(first_head, 0.0, xp).astype(BF16)
            y_diag = _dot(jnp.concatenate([m_a, m_b], axis=1),
                          jnp.concatenate([x_a, x_b], axis=0))
            off_scale = jnp.where(first_head,
                                  jnp.broadcast_to(jnp.exp(col_a), (T, V7X_LANES)),
                                  jnp.broadcast_to(jnp.exp(col_b), (T, V7X_LANES)))
            w2_pair = jnp.where(first_head,
                                jnp.broadcast_to(w2[:, ha:ha + 1], (T, V7X_LANES)),
                                jnp.broadcast_to(w2[:, hb:hb + 1], (T, V7X_LANES)))
            y_pairs.append(y_diag + off_scale * y_off[:, j * V7X_LANES:(j + 1) * V7X_LANES])
            xd_pairs.append((xp * w2_pair).astype(BF16))

        y_g = jnp.concatenate(y_pairs, axis=1) + dskip_ref[:, g * GW:(g + 1) * GW] * xs_g
        xd_g = jnp.concatenate(xd_pairs, axis=1)
        states_g = _dot(b_f.T.astype(BF16), xd_g)
        state_ref[g] = state_g * decay_wide[:, g * GW:(g + 1) * GW] + states_g

        zg = z_ref[:, g * GW:(g + 1) * GW].astype(F32)
        y_g = y_g * (zg * _sigmoid(zg))
        y_g = y_g * lax.rsqrt(jnp.mean(y_g * y_g, axis=-1, keepdims=True) + EPS)
        y_ref[:, g * GW:(g + 1) * GW] = (y_g * ng_ref[:, g * GW:(g + 1) * GW]).astype(y_ref.dtype)


def _ssd(proj, small, conv_w, conv_b, dt_bias_row, a_log_row, d_skip_row, norm_g_row,
         z_block, xbc_block):
    seq = proj.shape[0]
    T = SSD_CHUNK
    full = lambda shape: pl.BlockSpec(shape, lambda c: (0, 0))
    return pl.pallas_call(
        _ssd_kernel,
        grid=(seq // T,),
        in_specs=[
            pl.BlockSpec((T, SSD_WIDTH), lambda c: (c, z_block)),
            pl.BlockSpec((T, SSD_CONV_WIDTH), lambda c: (c, xbc_block)),
            pl.BlockSpec((T, SMALL_WIDTH), lambda c: (c, 0)),
            full((SSD_CONV, SSD_CONV_WIDTH)),
            full((1, SSD_CONV_WIDTH)),
            full((1, SMALL_WIDTH)),
            full((1, SMALL_WIDTH)),
            full((1, SSD_WIDTH)),
            full((1, SSD_WIDTH)),
        ],
        out_specs=pl.BlockSpec((T, SSD_WIDTH), lambda c: (c, 0)),
        out_shape=jax.ShapeDtypeStruct((seq, SSD_WIDTH), BF16),
        scratch_shapes=[
            pltpu.VMEM((T + 2 * V7X_SUBLANES, SSD_CONV_WIDTH), F32),
            pltpu.VMEM((SSD_GROUPS, SSD_STATE, SSD_GROUP_WIDTH), F32),
        ],
        compiler_params=pltpu.CompilerParams(
            dimension_semantics=("arbitrary",),
            vmem_limit_bytes=VMEM_LIMIT_BYTES),
        name="ssd_mixer",
    )(proj, proj, small, conv_w, conv_b, dt_bias_row, a_log_row, d_skip_row, norm_g_row)


def _mlstm_kernel(q_ref, k_ref, v_ref, o_ref, small_ref, ib_row_ref, fb_row_ref,
                  ib_col_ref, fb_col_ref, ng_ref, h_ref, ct_ref, n_ref, m_ref):
    T = MLSTM_CHUNK
    DK = MLSTM_QK_DIM
    DV = MLSTM_V_DIM
    scale = DK ** -0.5

    @pl.when(pl.program_id(0) == 0)
    def _():
        ct_ref[...] = jnp.zeros(ct_ref.shape, F32)
        n_ref[...] = jnp.zeros(n_ref.shape, F32)
        m_ref[...] = jnp.zeros(m_ref.shape, F32)

    sm = small_ref[...]
    sm_t = sm.T
    row_i = lax.broadcasted_iota(jnp.int32, (T, T), 0)
    col_i = lax.broadcasted_iota(jnp.int32, (T, T), 1)
    causal = row_i >= col_i
    tri = causal.astype(BF16)
    tri_t = (row_i <= col_i).astype(BF16)
    log_i_c = _softcap(sm + ib_row_ref[...])
    log_f_c = _log_sigmoid(_softcap(sm + fb_row_ref[...]))
    bcum_c = _dot01_left(tri, log_f_c)
    log_i_r = _softcap(sm_t + ib_col_ref[...])
    log_f_r = _log_sigmoid(_softcap(sm_t + fb_col_ref[...]))
    bcum_r = _dot01_right(log_f_r, tri_t)

    for h in range(MLSTM_HEADS):
        q = q_ref[:, h * DK:(h + 1) * DK]
        k = k_ref[:, h * DK:(h + 1) * DK]
        v = v_ref[:, h * DV:(h + 1) * DV]
        li_c = log_i_c[:, I_LANE + h:I_LANE + h + 1]
        bc_c = bcum_c[:, F_LANE + h:F_LANE + h + 1]
        li_r = log_i_r[I_LANE + h:I_LANE + h + 1, :]
        bc_r = bcum_r[F_LANE + h:F_LANE + h + 1, :]
        b_last = bc_c[T - 1:T, :]
        m_prev = m_ref[h][0:1, 0:1]
        ct = ct_ref[h]
        n_row = n_ref[h][0:1, :]

        a_c = b_last - bc_c + li_c
        m_loc = jnp.max(a_c, axis=0, keepdims=True)
        w_c = jnp.exp(a_c - m_loc) * scale
        w_r = jnp.exp(b_last - bc_r + li_r - m_loc) * scale
        k_f = k.astype(F32)
        kw_t = (k_f.T * w_r).astype(BF16)
        c_loc_t = _dot(kw_t, v)
        n_loc = jnp.sum(k_f * w_c, axis=0, keepdims=True)

        d_mat = jnp.where(causal, bc_c - bc_r + li_r, -jnp.inf)
        m_intra = jnp.max(d_mat, axis=1, keepdims=True)
        inter_log = bc_c + m_prev
        m_t = jnp.maximum(inter_log, m_intra)
        s_mat = _dot_nt(q, k) * scale * jnp.exp(d_mat - m_t)
        w_inter = jnp.exp(inter_log - m_t)
        num = _dot(s_mat.astype(BF16), v) + w_inter * _dot(q, ct.astype(BF16))
        nq = (jnp.sum(s_mat, axis=1, keepdims=True)
              + w_inter * jnp.sum(q.astype(F32) * n_row, axis=1, keepdims=True))
        den = jnp.maximum(jnp.abs(nq), jnp.exp(-m_t))
        hh = num * (1.0 / den)
        hh = hh * lax.rsqrt(jnp.mean(hh * hh, axis=-1, keepdims=True) + EPS)
        og = o_ref[:, h * DV:(h + 1) * DV].astype(F32)
        hh = hh * ng_ref[:, h * DV:(h + 1) * DV] * _sigmoid(og)
        h_ref[:, h * DV:(h + 1) * DV] = hh.astype(h_ref.dtype)

        m_new = jnp.maximum(b_last + m_prev, m_loc)
        s_old = jnp.exp(b_last + m_prev - m_new)
        s_new = jnp.exp(m_loc - m_new)
        ct_ref[h] = s_old * ct + s_new * c_loc_t
        n_ref[h] = jnp.broadcast_to(s_old * n_row + s_new * n_loc, (V7X_SUBLANES, DK))
        m_ref[h] = jnp.broadcast_to(m_new, (V7X_SUBLANES, V7X_LANES))


def _mlstm(proj, small, ib_row, fb_row, ib_col, fb_col, norm_g_row,
           q_block, k_block, v_block, o_block):
    seq = proj.shape[0]
    T = MLSTM_CHUNK
    full = lambda shape: pl.BlockSpec(shape, lambda c: (0, 0))
    return pl.pallas_call(
        _mlstm_kernel,
        grid=(seq // T,),
        in_specs=[
            pl.BlockSpec((T, MLSTM_QK_WIDTH), lambda c: (c, q_block)),
            pl.BlockSpec((T, MLSTM_QK_WIDTH), lambda c: (c, k_block)),
            pl.BlockSpec((T, MLSTM_WIDTH), lambda c: (c, v_block)),
            pl.BlockSpec((T, MLSTM_WIDTH), lambda c: (c, o_block)),
            pl.BlockSpec((T, SMALL_WIDTH), lambda c: (c, 0)),
            full((1, SMALL_WIDTH)),
            full((1, SMALL_WIDTH)),
            full((SMALL_WIDTH, 1)),
            full((SMALL_WIDTH, 1)),
            full((1, MLSTM_WIDTH)),
        ],
        out_specs=pl.BlockSpec((T, MLSTM_WIDTH), lambda c: (c, 0)),
        out_shape=jax.ShapeDtypeStruct((seq, MLSTM_WIDTH), BF16),
        scratch_shapes=[
            pltpu.VMEM((MLSTM_HEADS, MLSTM_QK_DIM, MLSTM_V_DIM), F32),
            pltpu.VMEM((MLSTM_HEADS, V7X_SUBLANES, MLSTM_QK_DIM), F32),
            pltpu.VMEM((MLSTM_HEADS, V7X_SUBLANES, V7X_LANES), F32),
        ],
        compiler_params=pltpu.CompilerParams(
            dimension_semantics=("arbitrary",),
            vmem_limit_bytes=VMEM_LIMIT_BYTES),
        name="mlstm_mixer",
    )(proj, proj, proj, proj, small, ib_row, fb_row, ib_col, fb_col, norm_g_row)


def _out_proj_kernel(x_ref, ys_ref, ym_ref, wt_ref, wb_ref, h_ref):
    h_ref[...] = x_ref[...] + _dot(ys_ref[...], wt_ref[...]) + _dot(ym_ref[...], wb_ref[...])


def _out_proj(x, y_ssd, y_ml, w_out):
    seq, d = x.shape
    grid = (seq // OUT_TM, d // OUT_TN)
    return pl.pallas_call(
        _out_proj_kernel,
        grid=grid,
        in_specs=[
            pl.BlockSpec((OUT_TM, OUT_TN), lambda m, n: (m, n)),
            pl.BlockSpec((OUT_TM, SSD_WIDTH), lambda m, n: (m, 0)),
            pl.BlockSpec((OUT_TM, MLSTM_WIDTH), lambda m, n: (m, 0)),
            pl.BlockSpec((SSD_WIDTH, OUT_TN), lambda m, n: (0, n)),
            pl.BlockSpec((MLSTM_WIDTH, OUT_TN), lambda m, n: (1, n)),
        ],
        out_specs=pl.BlockSpec((OUT_TM, OUT_TN), lambda m, n: (m, n)),
        out_shape=jax.ShapeDtypeStruct((seq, d), F32),
        compiler_params=pltpu.CompilerParams(
            dimension_semantics=("arbitrary", "arbitrary"),
            vmem_limit_bytes=VMEM_LIMIT_BYTES),
        name="out_proj",
    )(x, y_ssd, y_ml, w_out, w_out)


def _ffn_kernel(h_ref, g_ref, wg_ref, wu_ref, wd_ref, gf_ref, out_ref, u_ref):
    f = pl.program_id(1)

    @pl.when(f == 0)
    def _():
        _rmsnorm_rows(h_ref, g_ref, u_ref, h_ref.shape[0])
        out_ref[...] = h_ref[...]

    u = u_ref[...]
    gate = _dot(u, wg_ref[...])
    up = _dot(u, wu_ref[...])
    a = (gate * _sigmoid(gate) * up).astype(BF16)
    out_ref[...] += _dot(a, wd_ref[...])

    @pl.when(f == pl.num_programs(1) - 1)
    def _():
        _rmsnorm_rows(out_ref, gf_ref, out_ref, out_ref.shape[0])


def _ffn(h1, g, w_gate, w_up, w_down, g_final):
    seq, d = h1.shape
    d_ff = w_gate.shape[1]
    grid = (seq // FFN_TM, d_ff // FFN_TF)
    return pl.pallas_call(
        _ffn_kernel,
        grid=grid,
        in_specs=[
            pl.BlockSpec((FFN_TM, d), lambda m, f: (m, 0)),
            pl.BlockSpec((1, d), lambda m, f: (0, 0)),
            pl.BlockSpec((d, FFN_TF), lambda m, f: (0, f)),
            pl.BlockSpec((d, FFN_TF), lambda m, f: (0, f)),
            pl.BlockSpec((FFN_TF, d), lambda m, f: (f, 0)),
            pl.BlockSpec((1, d), lambda m, f: (0, 0)),
        ],
        out_specs=pl.BlockSpec((FFN_TM, d), lambda m, f: (m, 0)),
        out_shape=jax.ShapeDtypeStruct((seq, d), F32),
        scratch_shapes=[pltpu.VMEM((FFN_TM, d), BF16)],
        compiler_params=pltpu.CompilerParams(
            dimension_semantics=("arbitrary", "arbitrary"),
            vmem_limit_bytes=VMEM_LIMIT_BYTES),
        name="ffn",
    )(h1, g, w_gate, w_up, w_down, g_final)


def _pad_lanes(vec, lane0):
    return jnp.zeros((1, SMALL_WIDTH), F32).at[0, lane0:lane0 + vec.shape[0]].set(vec.astype(F32))


def _layer(h, norm_mix_g, w_in, conv_w, conv_b, dt_bias, a_log, d_skip, ssd_norm_g,
           i_bias, f_bias, mlstm_norm_g, w_out, norm_ffn_g, w_gate, w_up, w_down, out_g):
    d = h.shape[1]
    o_z, o_xbc = 0, SSD_WIDTH
    o_dt = o_xbc + SSD_CONV_WIDTH
    o_q = o_dt + SSD_HEADS
    o_k = o_q + MLSTM_QK_WIDTH
    o_v = o_k + MLSTM_QK_WIDTH
    o_o = o_v + MLSTM_WIDTH
    o_i = o_o + MLSTM_WIDTH
    o_f = o_i + MLSTM_HEADS
    cols = lambda a, n: w_in[:, a:a + n]
    w_big = jnp.concatenate(
        [cols(o_z, SSD_WIDTH), cols(o_v, MLSTM_WIDTH), cols(o_o, MLSTM_WIDTH),
         cols(o_xbc, SSD_CONV_WIDTH), cols(o_q, MLSTM_QK_WIDTH), cols(o_k, MLSTM_QK_WIDTH)],
        axis=1).astype(BF16)
    n_small = SSD_HEADS + 2 * MLSTM_HEADS
    w_small = jnp.concatenate(
        [cols(o_dt, SSD_HEADS), cols(o_i, MLSTM_HEADS), cols(o_f, MLSTM_HEADS),
         jnp.zeros((d, SMALL_WIDTH - n_small), w_in.dtype)], axis=1).astype(BF16)
    z_block, v_block, o_block = 0, 1, 2
    xbc_block = (SSD_WIDTH + 2 * MLSTM_WIDTH) // SSD_CONV_WIDTH
    q_block = (SSD_WIDTH + 2 * MLSTM_WIDTH + SSD_CONV_WIDTH) // MLSTM_QK_WIDTH
    k_block = q_block + 1

    proj, small = _in_proj(h, norm_mix_g.reshape(1, d), w_big, w_small)

    y_ssd = _ssd(proj, small, conv_w.astype(F32), conv_b.reshape(1, -1).astype(F32),
                 _pad_lanes(dt_bias, DT_LANE), _pad_lanes(a_log, DT_LANE),
                 jnp.repeat(d_skip.astype(F32), SSD_HEAD_DIM).reshape(1, -1),
                 ssd_norm_g.reshape(1, -1).astype(F32), z_block, xbc_block)

    ib_row = _pad_lanes(i_bias, I_LANE)
    fb_row = _pad_lanes(f_bias, F_LANE)
    y_ml = _mlstm(proj, small, ib_row, fb_row, ib_row.reshape(-1, 1), fb_row.reshape(-1, 1),
                  mlstm_norm_g.reshape(1, -1).astype(F32), q_block, k_block, v_block, o_block)

    h1 = _out_proj(h, y_ssd, y_ml, w_out.astype(BF16))
    return _ffn(h1, norm_ffn_g.reshape(1, d), w_gate.astype(BF16), w_up.astype(BF16),
                w_down.astype(BF16), out_g.reshape(1, d))


def kernel(x, norm_mix_g, w_in, conv_w, conv_b, dt_bias, a_log, d_skip, ssd_norm_g, i_bias,
           f_bias, mlstm_norm_g, w_out, norm_ffn_g, w_gate, w_up, w_down, final_norm_g):
    batch, seq, d = x.shape
    depth = w_in.shape[0]
    assert batch == 1 and depth == 1, "single-sequence, single-layer problem"
    out = _layer(x.reshape(seq, d), norm_mix_g[0], w_in[0], conv_w[0], conv_b[0], dt_bias[0],
                 a_log[0], d_skip[0], ssd_norm_g[0], i_bias[0], f_bias[0], mlstm_norm_g[0],
                 w_out[0], norm_ffn_g[0], w_gate[0], w_up[0], w_down[0], final_norm_g)
    return out.reshape(batch, seq, d)
```

```python
import functools

import jax
import jax.numpy as jnp
from jax import lax
from jax.experimental import pallas as pl
from jax.experimental.pallas import tpu as pltpu

F32 = jnp.float32
BF16 = jnp.bfloat16
EPS = 1e-6

V7X_LANES = 128
V7X_SUBLANES = 8
V7X_VMEM_BYTES = 64 * 1024 * 1024
VMEM_LIMIT_BYTES = 56 * 1024 * 1024

SSD_HEADS = 32
SSD_HEAD_DIM = 64
SSD_WIDTH = SSD_HEADS * SSD_HEAD_DIM
SSD_GROUPS = 4
SSD_STATE = 128
SSD_CONV = 4
SSD_GROUP_WIDTH = SSD_WIDTH // SSD_GROUPS
SSD_BC_WIDTH = SSD_GROUPS * SSD_STATE
SSD_CONV_WIDTH = SSD_WIDTH + 2 * SSD_BC_WIDTH
MLSTM_HEADS = 8
MLSTM_QK_DIM = 128
MLSTM_V_DIM = 256
MLSTM_WIDTH = MLSTM_HEADS * MLSTM_V_DIM
MLSTM_QK_WIDTH = MLSTM_HEADS * MLSTM_QK_DIM
GATE_SOFTCAP = 15.0

IN_WIDE_BEFORE_DT = SSD_WIDTH + SSD_CONV_WIDTH
IN_WIDE = IN_WIDE_BEFORE_DT + 2 * MLSTM_QK_WIDTH + 2 * MLSTM_WIDTH
IN_I_COL = IN_WIDE + SSD_HEADS
DT_LANE = 0
I_LANE = SSD_HEADS
F_LANE = SSD_HEADS + MLSTM_HEADS
SMALL_WIDTH = V7X_LANES

SSD_CHUNK = 128
MLSTM_CHUNK = 256
PROJ_TM = 1024
PROJ_TN = 1024
OUT_TM = 1024
OUT_TN = 512
FFN_TM = 512
FFN_TF = 512
NORM_ROWS = 128
CAST_ROWS = 256


def _dot(a, b):
    return jnp.dot(a, b, preferred_element_type=F32)


def _dot_nt(a, b):
    return lax.dot_general(a, b, (((1,), (1,)), ((), ())), preferred_element_type=F32)


def _split3(x):
    hi = x.astype(BF16)
    r1 = x - hi.astype(F32)
    mid = r1.astype(BF16)
    lo = (r1 - mid.astype(F32)).astype(BF16)
    return hi, mid, lo


def _dot01_left(m01, x):
    hi, mid, lo = _split3(x)
    return _dot(m01, hi) + _dot(m01, mid) + _dot(m01, lo)


def _dot01_right(x, m01):
    hi, mid, lo = _split3(x)
    return _dot(hi, m01) + _dot(mid, m01) + _dot(lo, m01)


def _sigmoid(x):
    return 1.0 / (1.0 + jnp.exp(-x))


def _softplus(x):
    return jnp.maximum(x, 0.0) + jnp.log1p(jnp.exp(-jnp.abs(x)))


def _log_sigmoid(x):
    return jnp.minimum(x, 0.0) - jnp.log1p(jnp.exp(-jnp.abs(x)))


def _softcap(x):
    return GATE_SOFTCAP * jnp.tanh(x / GATE_SOFTCAP)


def _rmsnorm_rows(src_ref, g_ref, dst_ref, rows):
    g = g_ref[...]

    def body(i, carry):
        r0 = pl.multiple_of(i * NORM_ROWS, NORM_ROWS)
        x = src_ref[pl.ds(r0, NORM_ROWS), :].astype(F32)
        y = x * lax.rsqrt(jnp.mean(x * x, axis=-1, keepdims=True) + EPS)
        dst_ref[pl.ds(r0, NORM_ROWS), :] = (y * g).astype(dst_ref.dtype)
        return carry

    lax.fori_loop(0, rows // NORM_ROWS, body, 0)


def _cast_block(src_ref, dst_ref):
    dst_ref[...] = src_ref[...].astype(dst_ref.dtype)


def _in_proj_first_kernel(x_ref, g_ref, w_ref, wnext_ref, wdt_ref, wif_ref,
                          proj_ref, small_ref, wb_ref, ws_ref, u_ref):
    n = pl.program_id(0)
    gap = IN_I_COL - IN_WIDE
    rows = w_ref.shape[0]

    @pl.when(n == 0)
    def _():
        _rmsnorm_rows(x_ref, g_ref, u_ref, x_ref.shape[0])
        lane = lax.broadcasted_iota(jnp.int32, wdt_ref.shape, 1)
        w_small = jnp.where(lane < I_LANE, wdt_ref[...],
                            jnp.where(lane < F_LANE + MLSTM_HEADS, wif_ref[...], 0.0))
        ws_ref[...] = w_small.astype(BF16)
        small_ref[...] = _dot(u_ref[...], ws_ref[...])

    @pl.when(n * PROJ_TN < IN_WIDE_BEFORE_DT)
    def _():
        def body(i, carry):
            r0 = pl.multiple_of(i * CAST_ROWS, CAST_ROWS)
            wb_ref[pl.ds(r0, CAST_ROWS), :] = w_ref[pl.ds(r0, CAST_ROWS), :].astype(BF16)
            return carry
        lax.fori_loop(0, rows // CAST_ROWS, body, 0)

    @pl.when(n * PROJ_TN >= IN_WIDE_BEFORE_DT)
    def _():
        def body(i, carry):
            r0 = pl.multiple_of(i * CAST_ROWS, CAST_ROWS)
            both = jnp.concatenate(
                [w_ref[pl.ds(r0, CAST_ROWS), :], wnext_ref[pl.ds(r0, CAST_ROWS), :]], axis=1)
            wb_ref[pl.ds(r0, CAST_ROWS), :] = both[:, gap:gap + PROJ_TN].astype(BF16)
            return carry
        lax.fori_loop(0, rows // CAST_ROWS, body, 0)

    proj_ref[...] = _dot(u_ref[...], wb_ref[...]).astype(proj_ref.dtype)


def _in_proj_rest_kernel(x_ref, g_ref, w_ref, ws_ref, proj_in_ref, small_in_ref,
                         proj_ref, small_ref, u_ref):
    del proj_in_ref, small_in_ref

    @pl.when(pl.program_id(1) == 0)
    def _():
        _rmsnorm_rows(x_ref, g_ref, u_ref, x_ref.shape[0])
        small_ref[...] = _dot(u_ref[...], ws_ref[...])

    proj_ref[...] = _dot(u_ref[...], w_ref[...]).astype(proj_ref.dtype)


def _in_proj(x, g, w_in):
    seq, d = x.shape
    n_blocks = IN_WIDE // PROJ_TN
    lanes_per_tn = PROJ_TN // V7X_LANES
    params = pltpu.CompilerParams(dimension_semantics=("arbitrary",),
                                  vmem_limit_bytes=VMEM_LIMIT_BYTES)
    proj, small, w_wide, w_small = pl.pallas_call(
        _in_proj_first_kernel,
        grid=(n_blocks,),
        in_specs=[
            pl.BlockSpec((PROJ_TM, d), lambda n: (0, 0)),
            pl.BlockSpec((1, d), lambda n: (0, 0)),
            pl.BlockSpec((d, PROJ_TN), lambda n: (0, n)),
            pl.BlockSpec((d, V7X_LANES), lambda n: (0, (n + 1) * lanes_per_tn)),
            pl.BlockSpec((d, V7X_LANES), lambda n: (0, IN_WIDE_BEFORE_DT // V7X_LANES)),
            pl.BlockSpec((d, V7X_LANES), lambda n: (0, IN_WIDE // V7X_LANES)),
        ],
        out_specs=[
            pl.BlockSpec((PROJ_TM, PROJ_TN), lambda n: (0, n)),
            pl.BlockSpec((PROJ_TM, SMALL_WIDTH), lambda n: (0, 0)),
            pl.BlockSpec((d, PROJ_TN), lambda n: (0, n)),
            pl.BlockSpec((d, SMALL_WIDTH), lambda n: (0, 0)),
        ],
        out_shape=[
            jax.ShapeDtypeStruct((seq, IN_WIDE), BF16),
            jax.ShapeDtypeStruct((seq, SMALL_WIDTH), F32),
            jax.ShapeDtypeStruct((d, IN_WIDE), BF16),
            jax.ShapeDtypeStruct((d, SMALL_WIDTH), BF16),
        ],
        scratch_shapes=[pltpu.VMEM((PROJ_TM, d), BF16)],
        compiler_params=params,
        name="in_proj_first",
    )(x, g, w_in, w_in, w_in, w_in)

    return pl.pallas_call(
        _in_proj_rest_kernel,
        grid=(seq // PROJ_TM - 1, n_blocks),
        in_specs=[
            pl.BlockSpec((PROJ_TM, d), lambda m, n: (m + 1, 0)),
            pl.BlockSpec((1, d), lambda m, n: (0, 0)),
            pl.BlockSpec((d, PROJ_TN), lambda m, n: (0, n)),
            pl.BlockSpec((d, SMALL_WIDTH), lambda m, n: (0, 0)),
            pl.BlockSpec(memory_space=pl.ANY),
            pl.BlockSpec(memory_space=pl.ANY),
        ],
        out_specs=[
            pl.BlockSpec((PROJ_TM, PROJ_TN), lambda m, n: (m + 1, n)),
            pl.BlockSpec((PROJ_TM, SMALL_WIDTH), lambda m, n: (m + 1, 0)),
        ],
        out_shape=[
            jax.ShapeDtypeStruct((seq, IN_WIDE), BF16),
            jax.ShapeDtypeStruct((seq, SMALL_WIDTH), F32),
        ],
        scratch_shapes=[pltpu.VMEM((PROJ_TM, d), BF16)],
        input_output_aliases={4: 0, 5: 1},
        compiler_params=pltpu.CompilerParams(
            dimension_semantics=("arbitrary", "arbitrary"),
            vmem_limit_bytes=VMEM_LIMIT_BYTES),
        name="in_proj_rest",
    )(x, g, w_wide, w_small, proj, small)


def _ssd_kernel(z_ref, xs_ref, bc_ref, small_ref, convw_ref, convb_ref, dtb_ref, alog_ref,
                dskip_ref, ng_ref, y_ref, xpad_ref, state_ref):
    T = SSD_CHUNK
    P = SSD_HEAD_DIM
    NS = SSD_STATE
    GW = SSD_GROUP_WIDTH
    PAD = V7X_SUBLANES

    @pl.when(pl.program_id(0) == 0)
    def _():
        xpad_ref[0:PAD, :] = jnp.zeros((PAD, SSD_CONV_WIDTH), F32)
        state_ref[...] = jnp.zeros(state_ref.shape, F32)

    xpad_ref[PAD:PAD + T, 0:SSD_WIDTH] = xs_ref[...].astype(F32)
    xpad_ref[PAD:PAD + T, SSD_WIDTH:SSD_CONV_WIDTH] = bc_ref[...].astype(F32)
    conv = convb_ref[...]
    for tap in range(SSD_CONV):
        off = PAD - (SSD_CONV - 1) + tap
        conv = conv + xpad_ref[off:off + T, :] * convw_ref[tap:tap + 1, :]
    xpad_ref[0:PAD, :] = xpad_ref[T:T + PAD, :]
    act = conv * _sigmoid(conv)

    lane = lax.broadcasted_iota(jnp.int32, (T, SMALL_WIDTH), 1)
    head_lane = lane < SSD_HEADS
    dt = jnp.where(head_lane, _softplus(small_ref[...] + dtb_ref[...]), 0.0)
    a_dt = dt * (-jnp.exp(alog_ref[...]))
    row_i = lax.broadcasted_iota(jnp.int32, (T, T), 0)
    col_i = lax.broadcasted_iota(jnp.int32, (T, T), 1)
    causal = row_i >= col_i
    tri = causal.astype(BF16)
    tri_t = (row_i <= col_i).astype(BF16)
    a_cum = _dot01_left(tri, a_dt)
    a_cum_t = _dot01_right(a_dt.T, tri_t)
    dt_t = dt.T
    a_last = a_cum[T - 1:T, :]
    w2 = dt * jnp.exp(a_last - a_cum)

    e_row = lax.broadcasted_iota(jnp.int32, (SMALL_WIDTH, SSD_WIDTH), 0)
    e_col = lax.broadcasted_iota(jnp.int32, (SMALL_WIDTH, SSD_WIDTH), 1)
    expand = (e_row == e_col // P).astype(BF16)
    decay_wide = _dot01_right(
        jnp.broadcast_to(jnp.exp(a_last), (V7X_SUBLANES, SMALL_WIDTH)), expand)[0:1, :]

    lane_t = lax.broadcasted_iota(jnp.int32, (T, V7X_LANES), 1)
    first_head = lane_t < P

    for g in range(SSD_GROUPS):
        xs_g = act[:, g * GW:(g + 1) * GW]
        b_f = act[:, SSD_WIDTH + g * NS:SSD_WIDTH + (g + 1) * NS]
        c_f = act[:, SSD_WIDTH + SSD_BC_WIDTH + g * NS:SSD_WIDTH + SSD_BC_WIDTH + (g + 1) * NS]
        b_bf = b_f.astype(BF16)
        c_bf = c_f.astype(BF16)
        cb = _dot_nt(c_bf, b_bf)
        state_g = state_ref[g]
        y_off = _dot(c_bf, state_g.astype(BF16))

        y_pairs = []
        xd_pairs = []
        for j in range(GW // V7X_LANES):
            ha = g * (GW // P) + 2 * j
            hb = ha + 1
            col_a = a_cum[:, ha:ha + 1]
            col_b = a_cum[:, hb:hb + 1]
            seg_a = jnp.where(causal, col_a - a_cum_t[ha:ha + 1, :], -jnp.inf)
            seg_b = jnp.where(causal, col_b - a_cum_t[hb:hb + 1, :], -jnp.inf)
            m_a = (cb * jnp.exp(seg_a) * dt_t[ha:ha + 1, :]).astype(BF16)
            m_b = (cb * jnp.exp(seg_b) * dt_t[hb:hb + 1, :]).astype(BF16)
            xp = xs_g[:, j * V7X_LANES:(j + 1) * V7X_LANES]
            x_a = jnp.where(first_head, xp, 0.0).astype(BF16)
            x_b = jnp.where(first_head, 0.0, xp).astype(BF16)
            y_diag = _dot(jnp.concatenate([m_a, m_b], axis=1),
                          jnp.concatenate([x_a, x_b], axis=0))
            off_scale = jnp.where(first_head,
                                  jnp.broadcast_to(jnp.exp(col_a), (T, V7X_LANES)),
                                  jnp.broadcast_to(jnp.exp(col_b), (T, V7X_LANES)))
            w2_pair = jnp.where(first_head,
                                jnp.broadcast_to(w2[:, ha:ha + 1], (T, V7X_LANES)),
                                jnp.broadcast_to(w2[:, hb:hb + 1], (T, V7X_LANES)))
            y_pairs.append(y_diag + off_scale * y_off[:, j * V7X_LANES:(j + 1) * V7X_LANES])
            xd_pairs.append((xp * w2_pair).astype(BF16))

        y_g = jnp.concatenate(y_pairs, axis=1) + dskip_ref[:, g * GW:(g + 1) * GW] * xs_g
        xd_g = jnp.concatenate(xd_pairs, axis=1)
        states_g = _dot(b_f.T.astype(BF16), xd_g)
        state_ref[g] = state_g * decay_wide[:, g * GW:(g + 1) * GW] + states_g

        zg = z_ref[:, g * GW:(g + 1) * GW].astype(F32)
        y_g = y_g * (zg * _sigmoid(zg))
        y_g = y_g * lax.rsqrt(jnp.mean(y_g * y_g, axis=-1, keepdims=True) + EPS)
        y_ref[:, g * GW:(g + 1) * GW] = (y_g * ng_ref[:, g * GW:(g + 1) * GW]).astype(y_ref.dtype)


def _ssd(proj, small, conv_w, conv_b, dt_bias_row, a_log_row, d_skip_row, norm_g_row):
    seq = proj.shape[0]
    T = SSD_CHUNK
    full = lambda shape: pl.BlockSpec(shape, lambda c: (0, 0))
    return pl.pallas_call(
        _ssd_kernel,
        grid=(seq // T,),
        in_specs=[
            pl.BlockSpec((T, SSD_WIDTH), lambda c: (c, 0)),
            pl.BlockSpec((T, SSD_WIDTH), lambda c: (c, 1)),
            pl.BlockSpec((T, 2 * SSD_BC_WIDTH),
                         lambda c: (c, 2 * SSD_WIDTH // (2 * SSD_BC_WIDTH))),
            pl.BlockSpec((T, SMALL_WIDTH), lambda c: (c, 0)),
            full((SSD_CONV, SSD_CONV_WIDTH)),
            full((1, SSD_CONV_WIDTH)),
            full((1, SMALL_WIDTH)),
            full((1, SMALL_WIDTH)),
            full((1, SSD_WIDTH)),
            full((1, SSD_WIDTH)),
        ],
        out_specs=pl.BlockSpec((T, SSD_WIDTH), lambda c: (c, 0)),
        out_shape=jax.ShapeDtypeStruct((seq, SSD_WIDTH), BF16),
        scratch_shapes=[
            pltpu.VMEM((T + 2 * V7X_SUBLANES, SSD_CONV_WIDTH), F32),
            pltpu.VMEM((SSD_GROUPS, SSD_STATE, SSD_GROUP_WIDTH), F32),
        ],
        compiler_params=pltpu.CompilerParams(
            dimension_semantics=("arbitrary",),
            vmem_limit_bytes=VMEM_LIMIT_BYTES),
        name="ssd_mixer",
    )(proj, proj, proj, small, conv_w, conv_b, dt_bias_row, a_log_row, d_skip_row, norm_g_row)


def _mlstm_kernel(q_ref, k_ref, v0_ref, v1_ref, o0_ref, o1_ref, small_ref, ib_row_ref,
                  fb_row_ref, ib_col_ref, fb_col_ref, ng_ref, wout_ref,
                  h_ref, wout_bf_ref, ct_ref, n_ref, m_ref):
    T = MLSTM_CHUNK
    DK = MLSTM_QK_DIM
    DV = MLSTM_V_DIM
    scale = DK ** -0.5
    half = MLSTM_HEADS // 2

    _cast_block(wout_ref, wout_bf_ref)

    @pl.when(pl.program_id(0) == 0)
    def _():
        ct_ref[...] = jnp.zeros(ct_ref.shape, F32)
        n_ref[...] = jnp.zeros(n_ref.shape, F32)
        m_ref[...] = jnp.zeros(m_ref.shape, F32)

    sm = small_ref[...]
    sm_t = sm.T
    row_i = lax.broadcasted_iota(jnp.int32, (T, T), 0)
    col_i = lax.broadcasted_iota(jnp.int32, (T, T), 1)
    causal = row_i >= col_i
    tri = causal.astype(BF16)
    tri_t = (row_i <= col_i).astype(BF16)
    log_i_c = _softcap(sm + ib_row_ref[...])
    log_f_c = _log_sigmoid(_softcap(sm + fb_row_ref[...]))
    bcum_c = _dot01_left(tri, log_f_c)
    log_i_r = _softcap(sm_t + ib_col_ref[...])
    log_f_r = _log_sigmoid(_softcap(sm_t + fb_col_ref[...]))
    bcum_r = _dot01_right(log_f_r, tri_t)

    for h in range(MLSTM_HEADS):
        v_ref, o_ref = (v0_ref, o0_ref) if h < half else (v1_ref, o1_ref)
        hv = h % half
        q = q_ref[:, h * DK:(h + 1) * DK]
        k = k_ref[:, h * DK:(h + 1) * DK]
        v = v_ref[:, hv * DV:(hv + 1) * DV]
        li_c = log_i_c[:, I_LANE + h:I_LANE + h + 1]
        bc_c = bcum_c[:, F_LANE + h:F_LANE + h + 1]
        li_r = log_i_r[I_LANE + h:I_LANE + h + 1, :]
        bc_r = bcum_r[F_LANE + h:F_LANE + h + 1, :]
        b_last = bc_c[T - 1:T, :]
        m_prev = m_ref[h][0:1, 0:1]
        ct = ct_ref[h]
        n_row = n_ref[h][0:1, :]

        a_c = b_last - bc_c + li_c
        m_loc = jnp.max(a_c, axis=0, keepdims=True)
        w_c = jnp.exp(a_c - m_loc) * scale
        w_r = jnp.exp(b_last - bc_r + li_r - m_loc) * scale
        k_f = k.astype(F32)
        kw_t = (k_f.T * w_r).astype(BF16)
        c_loc_t = _dot(kw_t, v)
        n_loc = jnp.sum(k_f * w_c, axis=0, keepdims=True)

        d_mat = jnp.where(causal, bc_c - bc_r + li_r, -jnp.inf)
        m_intra = jnp.max(d_mat, axis=1, keepdims=True)
        inter_log = bc_c + m_prev
        m_t = jnp.maximum(inter_log, m_intra)
        s_mat = _dot_nt(q, k) * scale * jnp.exp(d_mat - m_t)
        w_inter = jnp.exp(inter_log - m_t)
        num = _dot(s_mat.astype(BF16), v) + w_inter * _dot(q, ct.astype(BF16))
        nq = (jnp.sum(s_mat, axis=1, keepdims=True)
              + w_inter * jnp.sum(q.astype(F32) * n_row, axis=1, keepdims=True))
        den = jnp.maximum(jnp.abs(nq), jnp.exp(-m_t))
        hh = num * (1.0 / den)
        hh = hh * lax.rsqrt(jnp.mean(hh * hh, axis=-1, keepdims=True) + EPS)
        og = o_ref[:, hv * DV:(hv + 1) * DV].astype(F32)
        hh = hh * ng_ref[:, h * DV:(h + 1) * DV] * _sigmoid(og)
        h_ref[:, h * DV:(h + 1) * DV] = hh.astype(h_ref.dtype)

        m_new = jnp.maximum(b_last + m_prev, m_loc)
        s_old = jnp.exp(b_last + m_prev - m_new)
        s_new = jnp.exp(m_loc - m_new)
        ct_ref[h] = s_old * ct + s_new * c_loc_t
        n_ref[h] = jnp.broadcast_to(s_old * n_row + s_new * n_loc, (V7X_SUBLANES, DK))
        m_ref[h] = jnp.broadcast_to(m_new, (V7X_SUBLANES, V7X_LANES))


def _mlstm(proj, small, ib_row, fb_row, ib_col, fb_col, norm_g_row, w_out):
    seq = proj.shape[0]
    T = MLSTM_CHUNK
    steps = seq // T
    out_rows, d = w_out.shape
    slab = out_rows // steps
    assert slab * steps == out_rows and slab % 16 == 0
    half_w = MLSTM_WIDTH // 2
    q_blk = IN_WIDE_BEFORE_DT // MLSTM_QK_WIDTH
    v_blk = (IN_WIDE_BEFORE_DT + 2 * MLSTM_QK_WIDTH) // half_w
    full = lambda shape: pl.BlockSpec(shape, lambda c: (0, 0))
    return pl.pallas_call(
        _mlstm_kernel,
        grid=(steps,),
        in_specs=[
            pl.BlockSpec((T, MLSTM_QK_WIDTH), lambda c: (c, q_blk)),
            pl.BlockSpec((T, MLSTM_QK_WIDTH), lambda c: (c, q_blk + 1)),
            pl.BlockSpec((T, half_w), lambda c: (c, v_blk)),
            pl.BlockSpec((T, half_w), lambda c: (c, v_blk + 1)),
            pl.BlockSpec((T, half_w), lambda c: (c, v_blk + 2)),
            pl.BlockSpec((T, half_w), lambda c: (c, v_blk + 3)),
            pl.BlockSpec((T, SMALL_WIDTH), lambda c: (c, 0)),
            full((1, SMALL_WIDTH)),
            full((1, SMALL_WIDTH)),
            full((SMALL_WIDTH, 1)),
            full((SMALL_WIDTH, 1)),
            full((1, MLSTM_WIDTH)),
            pl.BlockSpec((slab, d), lambda c: (c, 0)),
        ],
        out_specs=[
            pl.BlockSpec((T, MLSTM_WIDTH), lambda c: (c, 0)),
            pl.BlockSpec((slab, d), lambda c: (c, 0)),
        ],
        out_shape=[
            jax.ShapeDtypeStruct((seq, MLSTM_WIDTH), BF16),
            jax.ShapeDtypeStruct((out_rows, d), BF16),
        ],
        scratch_shapes=[
            pltpu.VMEM((MLSTM_HEADS, MLSTM_QK_DIM, MLSTM_V_DIM), F32),
            pltpu.VMEM((MLSTM_HEADS, V7X_SUBLANES, MLSTM_QK_DIM), F32),
            pltpu.VMEM((MLSTM_HEADS, V7X_SUBLANES, V7X_LANES), F32),
        ],
        compiler_params=pltpu.CompilerParams(
            dimension_semantics=("arbitrary",),
            vmem_limit_bytes=VMEM_LIMIT_BYTES),
        name="mlstm_mixer",
    )(proj, proj, proj, proj, proj, proj, small, ib_row, fb_row, ib_col, fb_col, norm_g_row, w_out)


def _out_proj_kernel(x_ref, ys_ref, ym_ref, wt_ref, wb_ref, wg_ref, wu_ref, wd_ref,
                     h_ref, wg_bf_ref, wu_bf_ref, wd_bf_ref):
    _cast_block(wg_ref, wg_bf_ref)
    _cast_block(wu_ref, wu_bf_ref)
    _cast_block(wd_ref, wd_bf_ref)
    h_ref[...] = x_ref[...] + _dot(ys_ref[...], wt_ref[...]) + _dot(ym_ref[...], wb_ref[...])


def _out_proj(x, y_ssd, y_ml, w_out_bf, w_gate, w_up, w_down):
    seq, d = x.shape
    d_ff = w_gate.shape[1]
    n_tiles = d // OUT_TN
    grid = (seq // OUT_TM, n_tiles)
    steps = grid[0] * grid[1]
    slab_gu = d // steps
    slab_d = d_ff // steps
    assert slab_gu * steps == d and slab_gu % 16 == 0
    assert slab_d * steps == d_ff and slab_d % 16 == 0
    step = lambda m, n: (m * n_tiles + n, 0)
    return pl.pallas_call(
        _out_proj_kernel,
        grid=grid,
        in_specs=[
            pl.BlockSpec((OUT_TM, OUT_TN), lambda m, n: (m, n)),
            pl.BlockSpec((OUT_TM, SSD_WIDTH), lambda m, n: (m, 0)),
            pl.BlockSpec((OUT_TM, MLSTM_WIDTH), lambda m, n: (m, 0)),
            pl.BlockSpec((SSD_WIDTH, OUT_TN), lambda m, n: (0, n)),
            pl.BlockSpec((MLSTM_WIDTH, OUT_TN), lambda m, n: (1, n)),
            pl.BlockSpec((slab_gu, d_ff), step),
            pl.BlockSpec((slab_gu, d_ff), step),
            pl.BlockSpec((slab_d, d), step),
        ],
        out_specs=[
            pl.BlockSpec((OUT_TM, OUT_TN), lambda m, n: (m, n)),
            pl.BlockSpec((slab_gu, d_ff), step),
            pl.BlockSpec((slab_gu, d_ff), step),
            pl.BlockSpec((slab_d, d), step),
        ],
        out_shape=[
            jax.ShapeDtypeStruct((seq, d), F32),
            jax.ShapeDtypeStruct((d, d_ff), BF16),
            jax.ShapeDtypeStruct((d, d_ff), BF16),
            jax.ShapeDtypeStruct((d_ff, d), BF16),
        ],
        compiler_params=pltpu.CompilerParams(
            dimension_semantics=("arbitrary", "arbitrary"),
            vmem_limit_bytes=VMEM_LIMIT_BYTES),
        name="out_proj",
    )(x, y_ssd, y_ml, w_out_bf, w_out_bf, w_gate, w_up, w_down)


def _ffn_kernel(h_ref, g_ref, wg_ref, wu_ref, wd_ref, gf_ref, out_ref, u_ref):
    f = pl.program_id(1)

    @pl.when(f == 0)
    def _():
        _rmsnorm_rows(h_ref, g_ref, u_ref, h_ref.shape[0])
        out_ref[...] = h_ref[...]

    u = u_ref[...]
    gate = _dot(u, wg_ref[...])
    up = _dot(u, wu_ref[...])
    a = (gate * _sigmoid(gate) * up).astype(BF16)
    out_ref[...] += _dot(a, wd_ref[...])

    @pl.when(f == pl.num_programs(1) - 1)
    def _():
        _rmsnorm_rows(out_ref, gf_ref, out_ref, out_ref.shape[0])


def _ffn(h1, g, w_gate, w_up, w_down, g_final):
    seq, d = h1.shape
    d_ff = w_gate.shape[1]
    grid = (seq // FFN_TM, d_ff // FFN_TF)
    return pl.pallas_call(
        _ffn_kernel,
        grid=grid,
        in_specs=[
            pl.BlockSpec((FFN_TM, d), lambda m, f: (m, 0)),
            pl.BlockSpec((1, d), lambda m, f: (0, 0)),
            pl.BlockSpec((d, FFN_TF), lambda m, f: (0, f)),
            pl.BlockSpec((d, FFN_TF), lambda m, f: (0, f)),
            pl.BlockSpec((FFN_TF, d), lambda m, f: (f, 0)),
            pl.BlockSpec((1, d), lambda m, f: (0, 0)),
        ],
        out_specs=pl.BlockSpec((FFN_TM, d), lambda m, f: (m, 0)),
        out_shape=jax.ShapeDtypeStruct((seq, d), F32),
        scratch_shapes=[pltpu.VMEM((FFN_TM, d), BF16)],
        compiler_params=pltpu.CompilerParams(
            dimension_semantics=("arbitrary", "arbitrary"),
            vmem_limit_bytes=VMEM_LIMIT_BYTES),
        name="ffn",
    )(h1, g, w_gate, w_up, w_down, g_final)


def _pad_lanes(vec, lane0):
    return jnp.zeros((1, SMALL_WIDTH), F32).at[0, lane0:lane0 + vec.shape[0]].set(vec.astype(F32))


def _layer(h, norm_mix_g, w_in, conv_w, conv_b, dt_bias, a_log, d_skip, ssd_norm_g,
           i_bias, f_bias, mlstm_norm_g, w_out, norm_ffn_g, w_gate, w_up, w_down, out_g):
    d = h.shape[1]
    proj, small = _in_proj(h, norm_mix_g.reshape(1, d), w_in)

    y_ssd = _ssd(proj, small, conv_w.astype(F32), conv_b.reshape(1, -1).astype(F32),
                 _pad_lanes(dt_bias, DT_LANE), _pad_lanes(a_log, DT_LANE),
                 jnp.repeat(d_skip.astype(F32), SSD_HEAD_DIM).reshape(1, -1),
                 ssd_norm_g.reshape(1, -1).astype(F32))

    ib_row = _pad_lanes(i_bias, I_LANE)
    fb_row = _pad_lanes(f_bias, F_LANE)
    y_ml, w_out_bf = _mlstm(proj, small, ib_row, fb_row, ib_row.reshape(-1, 1),
                            fb_row.reshape(-1, 1), mlstm_norm_g.reshape(1, -1).astype(F32), w_out)

    h1, wg_bf, wu_bf, wd_bf = _out_proj(h, y_ssd, y_ml, w_out_bf, w_gate, w_up, w_down)
    return _ffn(h1, norm_ffn_g.reshape(1, d), wg_bf, wu_bf, wd_bf, out_g.reshape(1, d))


def kernel(x, norm_mix_g, w_in, conv_w, conv_b, dt_bias, a_log, d_skip, ssd_norm_g, i_bias,
           f_bias, mlstm_norm_g, w_out, norm_ffn_g, w_gate, w_up, w_down, final_norm_g):
    batch, seq, d = x.shape
    depth = w_in.shape[0]
    assert batch == 1 and depth == 1, "single-sequence, single-layer problem"
    out = _layer(x.reshape(seq, d), norm_mix_g[0], w_in[0], conv_w[0], conv_b[0], dt_bias[0],
                 a_log[0], d_skip[0], ssd_norm_g[0], i_bias[0], f_bias[0], mlstm_norm_g[0],
                 w_out[0], norm_ffn_g[0], w_gate[0], w_up[0], w_down[0], final_norm_g)
    return out.reshape(batch, seq, d)
```

```python
import math

import jax
import jax.numpy as jnp
from jax import lax
from jax.experimental import pallas as pl
from jax.experimental.pallas import tpu as pltpu

F32 = jnp.float32
BF16 = jnp.bfloat16
EPS = 1e-6
LOG2E = math.log2(math.e)

V7X_LANES = 128
V7X_SUBLANES = 8
V7X_BF16_ROWS = 16
VMEM_LIMIT_BYTES = 56 * 1024 * 1024

SSD_HEADS = 32
SSD_HEAD_DIM = 64
SSD_WIDTH = SSD_HEADS * SSD_HEAD_DIM
SSD_GROUPS = 4
SSD_STATE = 128
SSD_CONV = 4
SSD_GROUP_WIDTH = SSD_WIDTH // SSD_GROUPS
SSD_BC_WIDTH = SSD_GROUPS * SSD_STATE
SSD_CONV_WIDTH = SSD_WIDTH + 2 * SSD_BC_WIDTH
MLSTM_HEADS = 8
MLSTM_QK_DIM = 128
MLSTM_V_DIM = 256
MLSTM_WIDTH = MLSTM_HEADS * MLSTM_V_DIM
MLSTM_QK_WIDTH = MLSTM_HEADS * MLSTM_QK_DIM
GATE_SOFTCAP = 15.0

IN_WIDE_BEFORE_DT = SSD_WIDTH + SSD_CONV_WIDTH
IN_WIDE = IN_WIDE_BEFORE_DT + 2 * MLSTM_QK_WIDTH + 2 * MLSTM_WIDTH
IN_I_COL = IN_WIDE + SSD_HEADS
DT_LANE = 0
I_LANE = SSD_HEADS
F_LANE = SSD_HEADS + MLSTM_HEADS
SMALL_WIDTH = V7X_LANES

SSD_CHUNK = 128
MLSTM_CHUNK = 256
PROJ_TM = 1024
PROJ_TN = 1024
OUT_TM = 1024
OUT_TN = 512
FFN_TM = 512
FFN_TF = 512
NORM_ROWS = 128
CAST_ROWS = 256


def _dot(a, b):
    return jnp.dot(a, b, preferred_element_type=F32)


def _dot_nt(a, b):
    return lax.dot_general(a, b, (((1,), (1,)), ((), ())), preferred_element_type=F32)


def _split3(x):
    hi = x.astype(BF16)
    r1 = x - hi.astype(F32)
    mid = r1.astype(BF16)
    lo = (r1 - mid.astype(F32)).astype(BF16)
    return hi, mid, lo


def _dot01_left(m01, x):
    hi, mid, lo = _split3(x)
    return _dot(m01, hi) + _dot(m01, mid) + _dot(m01, lo)


def _dot01_right(x, m01):
    hi, mid, lo = _split3(x)
    return _dot(hi, m01) + _dot(mid, m01) + _dot(lo, m01)


def _sigmoid(x):
    return 0.5 * jnp.tanh(0.5 * x) + 0.5


def _softplus(x):
    return jnp.maximum(x, 0.0) + jnp.log1p(jnp.exp(-jnp.abs(x)))


def _log_sigmoid(x):
    return jnp.minimum(x, 0.0) - jnp.log1p(jnp.exp(-jnp.abs(x)))


def _softcap(x):
    return GATE_SOFTCAP * jnp.tanh(x / GATE_SOFTCAP)


def _rmsnorm_rows(src_ref, g_ref, dst_ref, rows):
    g = g_ref[...]

    def body(i, carry):
        r0 = pl.multiple_of(i * NORM_ROWS, NORM_ROWS)
        x = src_ref[pl.ds(r0, NORM_ROWS), :].astype(F32)
        y = x * lax.rsqrt(jnp.mean(x * x, axis=-1, keepdims=True) + EPS)
        dst_ref[pl.ds(r0, NORM_ROWS), :] = (y * g).astype(dst_ref.dtype)
        return carry

    lax.fori_loop(0, rows // NORM_ROWS, body, 0)


def _cast_block(src_ref, dst_ref):
    dst_ref[...] = src_ref[...].astype(dst_ref.dtype)


def _causal_constants(t):
    row = lax.broadcasted_iota(jnp.int32, (t, t), 0)
    col = lax.broadcasted_iota(jnp.int32, (t, t), 1)
    tri = (row >= col).astype(BF16)
    tri_t = (row <= col).astype(BF16)
    neg_mask = jnp.where(row >= col, 0.0, -jnp.inf).astype(F32)
    return tri, tri_t, neg_mask


def _in_proj_first_kernel(x_ref, g_ref, w_ref, wnext_ref, wdt_ref, wif_ref,
                          proj_ref, small_ref, wb_ref, ws_ref, u_ref):
    n = pl.program_id(0)
    gap = wnext_ref.shape[0]

    @pl.when(n == 0)
    def _():
        _rmsnorm_rows(x_ref, g_ref, u_ref, x_ref.shape[0])
        ws_ref[DT_LANE:I_LANE, :] = wdt_ref[...].astype(BF16)
        ws_ref[I_LANE:F_LANE + MLSTM_HEADS, :] = wif_ref[...].astype(BF16)
        ws_ref[F_LANE + MLSTM_HEADS:, :] = jnp.zeros(
            (SMALL_WIDTH - F_LANE - MLSTM_HEADS, ws_ref.shape[1]), BF16)
        small_ref[...] = _dot_nt(u_ref[...], ws_ref[...])

    @pl.when(n * PROJ_TN < IN_WIDE_BEFORE_DT)
    def _():
        def body(i, carry):
            r0 = pl.multiple_of(i * CAST_ROWS, CAST_ROWS)
            wb_ref[pl.ds(r0, CAST_ROWS), :] = w_ref[pl.ds(r0, CAST_ROWS), :].astype(BF16)
            return carry
        lax.fori_loop(0, PROJ_TN // CAST_ROWS, body, 0)

    @pl.when(n * PROJ_TN >= IN_WIDE_BEFORE_DT)
    def _():
        def body(i, carry):
            r0 = pl.multiple_of(i * gap, gap)
            wb_ref[pl.ds(r0, gap), :] = w_ref[pl.ds(r0 + gap, gap), :].astype(BF16)
            return carry
        lax.fori_loop(0, PROJ_TN // gap - 1, body, 0)
        wb_ref[PROJ_TN - gap:, :] = wnext_ref[...].astype(BF16)

    proj_ref[...] = _dot_nt(u_ref[...], wb_ref[...]).astype(proj_ref.dtype)


def _in_proj_rest_kernel(x_ref, g_ref, w_ref, ws_ref, proj_in_ref, small_in_ref,
                         proj_ref, small_ref, u_ref):
    del proj_in_ref, small_in_ref

    @pl.when(pl.program_id(1) == 0)
    def _():
        _rmsnorm_rows(x_ref, g_ref, u_ref, x_ref.shape[0])
        small_ref[...] = _dot_nt(u_ref[...], ws_ref[...])

    proj_ref[...] = _dot_nt(u_ref[...], w_ref[...]).astype(proj_ref.dtype)


def _in_proj(x, g, w_in_t):
    seq, d = x.shape
    n_blocks = IN_WIDE // PROJ_TN
    gap = SSD_HEADS
    if_rows = 2 * MLSTM_HEADS
    proj, small, w_wide, w_small = pl.pallas_call(
        _in_proj_first_kernel,
        grid=(n_blocks,),
        in_specs=[
            pl.BlockSpec((PROJ_TM, d), lambda n: (0, 0)),
            pl.BlockSpec((1, d), lambda n: (0, 0)),
            pl.BlockSpec((PROJ_TN, d), lambda n: (n, 0)),
            pl.BlockSpec((gap, d), lambda n: ((n + 1) * (PROJ_TN // gap), 0)),
            pl.BlockSpec((gap, d), lambda n: (IN_WIDE_BEFORE_DT // gap, 0)),
            pl.BlockSpec((if_rows, d), lambda n: (IN_I_COL // if_rows, 0)),
        ],
        out_specs=[
            pl.BlockSpec((PROJ_TM, PROJ_TN), lambda n: (0, n)),
            pl.BlockSpec((PROJ_TM, SMALL_WIDTH), lambda n: (0, 0)),
            pl.BlockSpec((PROJ_TN, d), lambda n: (n, 0)),
            pl.BlockSpec((SMALL_WIDTH, d), lambda n: (0, 0)),
        ],
        out_shape=[
            jax.ShapeDtypeStruct((seq, IN_WIDE), BF16),
            jax.ShapeDtypeStruct((seq, SMALL_WIDTH), F32),
            jax.ShapeDtypeStruct((IN_WIDE, d), BF16),
            jax.ShapeDtypeStruct((SMALL_WIDTH, d), BF16),
        ],
        scratch_shapes=[pltpu.VMEM((PROJ_TM, d), BF16)],
        compiler_params=pltpu.CompilerParams(
            dimension_semantics=("arbitrary",),
            vmem_limit_bytes=VMEM_LIMIT_BYTES),
        name="in_proj_first",
    )(x, g, w_in_t, w_in_t, w_in_t, w_in_t)

    return pl.pallas_call(
        _in_proj_rest_kernel,
        grid=(seq // PROJ_TM - 1, n_blocks),
        in_specs=[
            pl.BlockSpec((PROJ_TM, d), lambda m, n: (m + 1, 0)),
            pl.BlockSpec((1, d), lambda m, n: (0, 0)),
            pl.BlockSpec((PROJ_TN, d), lambda m, n: (n, 0)),
            pl.BlockSpec((SMALL_WIDTH, d), lambda m, n: (0, 0)),
            pl.BlockSpec(memory_space=pl.ANY),
            pl.BlockSpec(memory_space=pl.ANY),
        ],
        out_specs=[
            pl.BlockSpec((PROJ_TM, PROJ_TN), lambda m, n: (m + 1, n)),
            pl.BlockSpec((PROJ_TM, SMALL_WIDTH), lambda m, n: (m + 1, 0)),
        ],
        out_shape=[
            jax.ShapeDtypeStruct((seq, IN_WIDE), BF16),
            jax.ShapeDtypeStruct((seq, SMALL_WIDTH), F32),
        ],
        scratch_shapes=[pltpu.VMEM((PROJ_TM, d), BF16)],
        input_output_aliases={4: 0, 5: 1},
        compiler_params=pltpu.CompilerParams(
            dimension_semantics=("arbitrary", "arbitrary"),
            vmem_limit_bytes=VMEM_LIMIT_BYTES),
        name="in_proj_rest",
    )(x, g, w_wide, w_small, proj, small)


def _ssd_kernel(z_ref, xs_ref, bc_ref, small_ref, convw_ref, convb_ref, dtb_ref, alog_ref,
                dskip_ref, ng_ref, tri_ref, trit_ref, negmask_ref, expand_ref, shift_ref,
                lanemask_ref, wg_ref, wu_ref,
                y_ref, wg_bf_ref, wu_bf_ref, xcat_ref, state_ref):
    T = SSD_CHUNK
    P = SSD_HEAD_DIM
    NS = SSD_STATE
    GW = SSD_GROUP_WIDTH
    TAIL = V7X_BF16_ROWS

    _cast_block(wg_ref, wg_bf_ref)
    _cast_block(wu_ref, wu_bf_ref)

    @pl.when(pl.program_id(0) == 0)
    def _():
        xcat_ref[0:T, :] = jnp.zeros((T, SSD_CONV_WIDTH), BF16)
        state_ref[...] = jnp.zeros(state_ref.shape, F32)

    xcat_ref[T:2 * T, 0:SSD_WIDTH] = xs_ref[...]
    xcat_ref[T:2 * T, SSD_WIDTH:SSD_CONV_WIDTH] = bc_ref[...]

    def conv_silu(c0, width):
        shifted = _dot(shift_ref[...], xcat_ref[:, c0:c0 + width])
        conv = convb_ref[:, c0:c0 + width] + (
            xcat_ref[T:2 * T, c0:c0 + width].astype(F32) * convw_ref[SSD_CONV - 1:SSD_CONV, c0:c0 + width])
        for back in range(1, SSD_CONV):
            tap = SSD_CONV - 1 - back
            conv = conv + shifted[(back - 1) * T:back * T, :] * convw_ref[tap:tap + 1, c0:c0 + width]
        return conv * _sigmoid(conv)

    lane = lax.broadcasted_iota(jnp.int32, (T, SMALL_WIDTH), 1)
    dt = jnp.where(lane < SSD_HEADS, _softplus(small_ref[...] + dtb_ref[...]), 0.0)
    a_dt = dt * (-jnp.exp(alog_ref[...]))
    col_l = _dot01_left(tri_ref[...], a_dt) * LOG2E
    dt_t = dt.T
    row_l = (_dot01_right(a_dt.T, trit_ref[...]) - jnp.log(dt_t)) * LOG2E
    e_col = jnp.exp2(col_l)
    last_l = col_l[T - 1:T, :]
    w2 = dt * jnp.exp2(last_l - col_l)
    decay_wide = _dot01_right(
        jnp.broadcast_to(jnp.exp2(last_l), (V7X_SUBLANES, SMALL_WIDTH)), expand_ref[...])[0:1, :]

    neg_mask = negmask_ref[...]
    lane_t = lax.broadcasted_iota(jnp.int32, (T, V7X_LANES), 1)
    first_head = lane_t < P
    mask_a = lanemask_ref[0:1, :]
    mask_b = lanemask_ref[1:2, :]

    bc_act = conv_silu(SSD_WIDTH, 2 * SSD_BC_WIDTH)
    for g in range(SSD_GROUPS):
        xs_g = conv_silu(g * GW, GW)
        b_f = bc_act[:, g * NS:(g + 1) * NS]
        c_bf = bc_act[:, SSD_BC_WIDTH + g * NS:SSD_BC_WIDTH + (g + 1) * NS].astype(BF16)
        cb = _dot_nt(c_bf, b_f.astype(BF16))
        state_g = state_ref[g]
        y_off = _dot(c_bf, state_g.astype(BF16))

        y_pairs = []
        xd_pairs = []
        for j in range(GW // V7X_LANES):
            ha = g * (GW // P) + 2 * j
            hb = ha + 1
            m_a = (cb * jnp.exp2(col_l[:, ha:ha + 1] - row_l[ha:ha + 1, :] + neg_mask)).astype(BF16)
            m_b = (cb * jnp.exp2(col_l[:, hb:hb + 1] - row_l[hb:hb + 1, :] + neg_mask)).astype(BF16)
            xp = xs_g[:, j * V7X_LANES:(j + 1) * V7X_LANES]
            xp_bf = xp.astype(BF16)
            y_diag = _dot(jnp.concatenate([m_a, m_b], axis=1),
                          jnp.concatenate([xp_bf * mask_a, xp_bf * mask_b], axis=0))
            off_scale = jnp.where(first_head,
                                  jnp.broadcast_to(e_col[:, ha:ha + 1], (T, V7X_LANES)),
                                  jnp.broadcast_to(e_col[:, hb:hb + 1], (T, V7X_LANES)))
            w2_pair = jnp.where(first_head,
                                jnp.broadcast_to(w2[:, ha:ha + 1], (T, V7X_LANES)),
                                jnp.broadcast_to(w2[:, hb:hb + 1], (T, V7X_LANES)))
            y_pairs.append(y_diag + off_scale * y_off[:, j * V7X_LANES:(j + 1) * V7X_LANES])
            xd_pairs.append((xp * w2_pair).astype(BF16))

        y_g = jnp.concatenate(y_pairs, axis=1) + dskip_ref[:, g * GW:(g + 1) * GW] * xs_g
        xd_g = jnp.concatenate(xd_pairs, axis=1)
        states_g = _dot(b_f.T.astype(BF16), xd_g)
        state_ref[g] = state_g * decay_wide[:, g * GW:(g + 1) * GW] + states_g

        zg = z_ref[:, g * GW:(g + 1) * GW].astype(F32)
        y_g = y_g * (zg * _sigmoid(zg))
        y_g = y_g * lax.rsqrt(jnp.mean(y_g * y_g, axis=-1, keepdims=True) + EPS)
        y_ref[:, g * GW:(g + 1) * GW] = (y_g * ng_ref[:, g * GW:(g + 1) * GW]).astype(y_ref.dtype)

    xcat_ref[T - TAIL:T, :] = xcat_ref[2 * T - TAIL:2 * T, :]


def _ssd(proj, small, conv_w, conv_b, dt_bias_row, a_log_row, d_skip_row, norm_g_row,
         w_gate, w_up):
    seq = proj.shape[0]
    T = SSD_CHUNK
    steps = seq // T
    d, d_ff = w_gate.shape
    slab = d // steps
    assert slab * steps == d and slab % V7X_BF16_ROWS == 0
    tri, tri_t, neg_mask = _causal_constants(T)
    e_row = lax.broadcasted_iota(jnp.int32, (SMALL_WIDTH, SSD_WIDTH), 0)
    e_col = lax.broadcasted_iota(jnp.int32, (SMALL_WIDTH, SSD_WIDTH), 1)
    expand = (e_row == e_col // SSD_HEAD_DIM).astype(BF16)
    s_row = lax.broadcasted_iota(jnp.int32, ((SSD_CONV - 1) * T, 2 * T), 0)
    s_col = lax.broadcasted_iota(jnp.int32, ((SSD_CONV - 1) * T, 2 * T), 1)
    shift = (s_col == T + s_row % T - (s_row // T + 1)).astype(BF16)
    m_lane = lax.broadcasted_iota(jnp.int32, (2, V7X_LANES), 1)
    m_row = lax.broadcasted_iota(jnp.int32, (2, V7X_LANES), 0)
    lane_mask = ((m_lane < SSD_HEAD_DIM) == (m_row == 0)).astype(BF16)

    full = lambda a: pl.BlockSpec(a.shape, lambda c: (0, 0))
    consts = (conv_w, conv_b, dt_bias_row, a_log_row, d_skip_row, norm_g_row,
              tri, tri_t, neg_mask, expand, shift, lane_mask)
    return pl.pallas_call(
        _ssd_kernel,
        grid=(steps,),
        in_specs=[
            pl.BlockSpec((T, SSD_WIDTH), lambda c: (c, 0)),
            pl.BlockSpec((T, SSD_WIDTH), lambda c: (c, 1)),
            pl.BlockSpec((T, 2 * SSD_BC_WIDTH),
                         lambda c: (c, 2 * SSD_WIDTH // (2 * SSD_BC_WIDTH))),
            pl.BlockSpec((T, SMALL_WIDTH), lambda c: (c, 0)),
            *[full(a) for a in consts],
            pl.BlockSpec((slab, d_ff), lambda c: (c, 0)),
            pl.BlockSpec((slab, d_ff), lambda c: (c, 0)),
        ],
        out_specs=[
            pl.BlockSpec((T, SSD_WIDTH), lambda c: (c, 0)),
            pl.BlockSpec((slab, d_ff), lambda c: (c, 0)),
            pl.BlockSpec((slab, d_ff), lambda c: (c, 0)),
        ],
        out_shape=[
            jax.ShapeDtypeStruct((seq, SSD_WIDTH), BF16),
            jax.ShapeDtypeStruct((d, d_ff), BF16),
            jax.ShapeDtypeStruct((d, d_ff), BF16),
        ],
        scratch_shapes=[
            pltpu.VMEM((2 * T, SSD_CONV_WIDTH), BF16),
            pltpu.VMEM((SSD_GROUPS, SSD_STATE, SSD_GROUP_WIDTH), F32),
        ],
        compiler_params=pltpu.CompilerParams(
            dimension_semantics=("arbitrary",),
            vmem_limit_bytes=VMEM_LIMIT_BYTES),
        name="ssd_mixer",
    )(proj, proj, proj, small, *consts, w_gate, w_up)


def _mlstm_kernel(q_ref, k_ref, v0_ref, v1_ref, o0_ref, o1_ref, small_ref, ib_row_ref,
                  fb_row_ref, ib_col_ref, fb_col_ref, ng_ref, tri_ref, trit_ref, negmask_ref,
                  wout_ref, wd_ref,
                  h_ref, wout_bf_ref, wd_bf_ref, ct_ref, n_ref, m_ref):
    T = MLSTM_CHUNK
    DK = MLSTM_QK_DIM
    DV = MLSTM_V_DIM
    scale = DK ** -0.5
    half = MLSTM_HEADS // 2

    _cast_block(wout_ref, wout_bf_ref)
    _cast_block(wd_ref, wd_bf_ref)

    @pl.when(pl.program_id(0) == 0)
    def _():
        ct_ref[...] = jnp.zeros(ct_ref.shape, F32)
        n_ref[...] = jnp.zeros(n_ref.shape, F32)
        m_ref[...] = jnp.zeros(m_ref.shape, F32)

    sm = small_ref[...]
    sm_t = sm.T
    neg_mask = negmask_ref[...]
    log_i_c = _softcap(sm + ib_row_ref[...])
    log_f_c = _log_sigmoid(_softcap(sm + fb_row_ref[...]))
    bcum_c = _dot01_left(tri_ref[...], log_f_c)
    log_i_r = _softcap(sm_t + ib_col_ref[...])
    log_f_r = _log_sigmoid(_softcap(sm_t + fb_col_ref[...]))
    bcum_r = _dot01_right(log_f_r, trit_ref[...])

    for h in range(MLSTM_HEADS):
        v_ref, o_ref = (v0_ref, o0_ref) if h < half else (v1_ref, o1_ref)
        hv = h % half
        q = q_ref[:, h * DK:(h + 1) * DK]
        k = k_ref[:, h * DK:(h + 1) * DK]
        v = v_ref[:, hv * DV:(hv + 1) * DV]
        li_c = log_i_c[:, I_LANE + h:I_LANE + h + 1]
        bc_c = bcum_c[:, F_LANE + h:F_LANE + h + 1]
        li_r = log_i_r[I_LANE + h:I_LANE + h + 1, :]
        bc_r = bcum_r[F_LANE + h:F_LANE + h + 1, :]
        b_last = bc_c[T - 1:T, :]
        m_prev = m_ref[h][0:1, 0:1]
        ct = ct_ref[h]
        n_row = n_ref[h][0:1, :]

        a_c = b_last - bc_c + li_c
        m_loc = jnp.max(a_c, axis=0, keepdims=True)
        w_c = jnp.exp(a_c - m_loc) * scale
        w_r = jnp.exp(b_last - bc_r + li_r - m_loc) * scale
        k_f = k.astype(F32)
        kw_t = (k_f.T * w_r).astype(BF16)
        c_loc_t = _dot(kw_t, v)
        n_loc = jnp.sum(k_f * w_c, axis=0, keepdims=True)

        d_mat = bc_c + (li_r - bc_r) + neg_mask
        m_intra = jnp.max(d_mat, axis=1, keepdims=True)
        inter_log = bc_c + m_prev
        m_t = jnp.maximum(inter_log, m_intra)
        s_mat = _dot_nt(q, k) * scale * jnp.exp(d_mat - m_t)
        w_inter = jnp.exp(inter_log - m_t)
        num = _dot(s_mat.astype(BF16), v) + w_inter * _dot(q, ct.astype(BF16))
        nq = (jnp.sum(s_mat, axis=1, keepdims=True)
              + w_inter * jnp.sum(q.astype(F32) * n_row, axis=1, keepdims=True))
        den = jnp.maximum(jnp.abs(nq), jnp.exp(-m_t))
        hh = num * (1.0 / den)
        hh = hh * lax.rsqrt(jnp.mean(hh * hh, axis=-1, keepdims=True) + EPS)
        og = o_ref[:, hv * DV:(hv + 1) * DV].astype(F32)
        hh = hh * ng_ref[:, h * DV:(h + 1) * DV] * _sigmoid(og)
        h_ref[:, h * DV:(h + 1) * DV] = hh.astype(h_ref.dtype)

        m_new = jnp.maximum(b_last + m_prev, m_loc)
        s_old = jnp.exp(b_last + m_prev - m_new)
        s_new = jnp.exp(m_loc - m_new)
        ct_ref[h] = s_old * ct + s_new * c_loc_t
        n_ref[h] = jnp.broadcast_to(s_old * n_row + s_new * n_loc, (V7X_SUBLANES, DK))
        m_ref[h] = jnp.broadcast_to(m_new, (V7X_SUBLANES, V7X_LANES))


def _mlstm(proj, small, ib_row, fb_row, ib_col, fb_col, norm_g_row, w_out, w_down):
    seq = proj.shape[0]
    T = MLSTM_CHUNK
    steps = seq // T
    out_rows, d = w_out.shape
    d_ff = w_down.shape[0]
    slab = out_rows // steps
    slab_d = d_ff // steps
    assert slab * steps == out_rows and slab % V7X_BF16_ROWS == 0
    assert slab_d * steps == d_ff and slab_d % V7X_BF16_ROWS == 0
    half_w = MLSTM_WIDTH // 2
    q_blk = IN_WIDE_BEFORE_DT // MLSTM_QK_WIDTH
    v_blk = (IN_WIDE_BEFORE_DT + 2 * MLSTM_QK_WIDTH) // half_w
    consts = (ib_row, fb_row, ib_col, fb_col, norm_g_row) + _causal_constants(T)
    full = lambda a: pl.BlockSpec(a.shape, lambda c: (0, 0))
    return pl.pallas_call(
        _mlstm_kernel,
        grid=(steps,),
        in_specs=[
            pl.BlockSpec((T, MLSTM_QK_WIDTH), lambda c: (c, q_blk)),
            pl.BlockSpec((T, MLSTM_QK_WIDTH), lambda c: (c, q_blk + 1)),
            pl.BlockSpec((T, half_w), lambda c: (c, v_blk)),
            pl.BlockSpec((T, half_w), lambda c: (c, v_blk + 1)),
            pl.BlockSpec((T, half_w), lambda c: (c, v_blk + 2)),
            pl.BlockSpec((T, half_w), lambda c: (c, v_blk + 3)),
            pl.BlockSpec((T, SMALL_WIDTH), lambda c: (c, 0)),
            *[full(a) for a in consts],
            pl.BlockSpec((slab, d), lambda c: (c, 0)),
            pl.BlockSpec((slab_d, d), lambda c: (c, 0)),
        ],
        out_specs=[
            pl.BlockSpec((T, MLSTM_WIDTH), lambda c: (c, 0)),
            pl.BlockSpec((slab, d), lambda c: (c, 0)),
            pl.BlockSpec((slab_d, d), lambda c: (c, 0)),
        ],
        out_shape=[
            jax.ShapeDtypeStruct((seq, MLSTM_WIDTH), BF16),
            jax.ShapeDtypeStruct((out_rows, d), BF16),
            jax.ShapeDtypeStruct((d_ff, d), BF16),
        ],
        scratch_shapes=[
            pltpu.VMEM((MLSTM_HEADS, MLSTM_QK_DIM, MLSTM_V_DIM), F32),
            pltpu.VMEM((MLSTM_HEADS, V7X_SUBLANES, MLSTM_QK_DIM), F32),
            pltpu.VMEM((MLSTM_HEADS, V7X_SUBLANES, V7X_LANES), F32),
        ],
        compiler_params=pltpu.CompilerParams(
            dimension_semantics=("arbitrary",),
            vmem_limit_bytes=VMEM_LIMIT_BYTES),
        name="mlstm_mixer",
    )(proj, proj, proj, proj, proj, proj, small, *consts, w_out, w_down)


def _out_proj_kernel(x_ref, ys_ref, ym_ref, wt_ref, wb_ref, h_ref):
    h_ref[...] = x_ref[...] + _dot(ys_ref[...], wt_ref[...]) + _dot(ym_ref[...], wb_ref[...])


def _out_proj(x, y_ssd, y_ml, w_out_bf):
    seq, d = x.shape
    grid = (seq // OUT_TM, d // OUT_TN)
    return pl.pallas_call(
        _out_proj_kernel,
        grid=grid,
        in_specs=[
            pl.BlockSpec((OUT_TM, OUT_TN), lambda m, n: (m, n)),
            pl.BlockSpec((OUT_TM, SSD_WIDTH), lambda m, n: (m, 0)),
            pl.BlockSpec((OUT_TM, MLSTM_WIDTH), lambda m, n: (m, 0)),
            pl.BlockSpec((SSD_WIDTH, OUT_TN), lambda m, n: (0, n)),
            pl.BlockSpec((MLSTM_WIDTH, OUT_TN), lambda m, n: (1, n)),
        ],
        out_specs=pl.BlockSpec((OUT_TM, OUT_TN), lambda m, n: (m, n)),
        out_shape=jax.ShapeDtypeStruct((seq, d), F32),
        compiler_params=pltpu.CompilerParams(
            dimension_semantics=("arbitrary", "arbitrary"),
            vmem_limit_bytes=VMEM_LIMIT_BYTES),
        name="out_proj",
    )(x, y_ssd, y_ml, w_out_bf, w_out_bf)


def _ffn_kernel(h_ref, g_ref, wg_ref, wu_ref, wd_ref, gf_ref, out_ref, u_ref):
    f = pl.program_id(1)

    @pl.when(f == 0)
    def _():
        _rmsnorm_rows(h_ref, g_ref, u_ref, h_ref.shape[0])
        out_ref[...] = h_ref[...]

    u = u_ref[...]
    gate = _dot(u, wg_ref[...])
    up = _dot(u, wu_ref[...])
    a = (gate * _sigmoid(gate) * up).astype(BF16)
    out_ref[...] += _dot(a, wd_ref[...])

    @pl.when(f == pl.num_programs(1) - 1)
    def _():
        _rmsnorm_rows(out_ref, gf_ref, out_ref, out_ref.shape[0])


def _ffn(h1, g, w_gate, w_up, w_down, g_final):
    seq, d = h1.shape
    d_ff = w_gate.shape[1]
    grid = (seq // FFN_TM, d_ff // FFN_TF)
    return pl.pallas_call(
        _ffn_kernel,
        grid=grid,
        in_specs=[
            pl.BlockSpec((FFN_TM, d), lambda m, f: (m, 0)),
            pl.BlockSpec((1, d), lambda m, f: (0, 0)),
            pl.BlockSpec((d, FFN_TF), lambda m, f: (0, f)),
            pl.BlockSpec((d, FFN_TF), lambda m, f: (0, f)),
            pl.BlockSpec((FFN_TF, d), lambda m, f: (f, 0)),
            pl.BlockSpec((1, d), lambda m, f: (0, 0)),
        ],
        out_specs=pl.BlockSpec((FFN_TM, d), lambda m, f: (m, 0)),
        out_shape=jax.ShapeDtypeStruct((seq, d), F32),
        scratch_shapes=[pltpu.VMEM((FFN_TM, d), BF16)],
        compiler_params=pltpu.CompilerParams(
            dimension_semantics=("arbitrary", "arbitrary"),
            vmem_limit_bytes=VMEM_LIMIT_BYTES),
        name="ffn",
    )(h1, g, w_gate, w_up, w_down, g_final)


def _pad_lanes(vec, lane0):
    return jnp.zeros((1, SMALL_WIDTH), F32).at[0, lane0:lane0 + vec.shape[0]].set(vec.astype(F32))


def _layer(h, norm_mix_g, w_in, conv_w, conv_b, dt_bias, a_log, d_skip, ssd_norm_g,
           i_bias, f_bias, mlstm_norm_g, w_out, norm_ffn_g, w_gate, w_up, w_down, out_g):
    d = h.shape[1]
    proj, small = _in_proj(h, norm_mix_g.reshape(1, d), w_in.T)

    y_ssd, wg_bf, wu_bf = _ssd(
        proj, small, conv_w.astype(F32), conv_b.reshape(1, -1).astype(F32),
        _pad_lanes(dt_bias, DT_LANE), _pad_lanes(a_log, DT_LANE),
        jnp.repeat(d_skip.astype(F32), SSD_HEAD_DIM).reshape(1, -1),
        ssd_norm_g.reshape(1, -1).astype(F32), w_gate, w_up)

    ib_row = _pad_lanes(i_bias, I_LANE)
    fb_row = _pad_lanes(f_bias, F_LANE)
    y_ml, w_out_bf, wd_bf = _mlstm(
        proj, small, ib_row, fb_row, ib_row.reshape(-1, 1), fb_row.reshape(-1, 1),
        mlstm_norm_g.reshape(1, -1).astype(F32), w_out, w_down)

    h1 = _out_proj(h, y_ssd, y_ml, w_out_bf)
    return _ffn(h1, norm_ffn_g.reshape(1, d), wg_bf, wu_bf, wd_bf, out_g.reshape(1, d))


def kernel(x, norm_mix_g, w_in, conv_w, conv_b, dt_bias, a_log, d_skip, ssd_norm_g, i_bias,
           f_bias, mlstm_norm_g, w_out, norm_ffn_g, w_gate, w_up, w_down, final_norm_g):
    batch, seq, d = x.shape
    depth = w_in.shape[0]
    assert batch == 1 and depth == 1, "single-sequence, single-layer problem"
    out = _layer(x.reshape(seq, d), norm_mix_g[0], w_in[0], conv_w[0], conv_b[0], dt_bias[0],
                 a_log[0], d_skip[0], ssd_norm_g[0], i_bias[0], f_bias[0], mlstm_norm_g[0],
                 w_out[0], norm_ffn_g[0], w_gate[0], w_up[0], w_down[0], final_norm_g)
    return out.reshape(batch, seq, d)
```

```python
import math

import jax
import jax.numpy as jnp
from jax import lax
from jax.experimental import pallas as pl
from jax.experimental.pallas import tpu as pltpu

F32 = jnp.float32
BF16 = jnp.bfloat16
EPS = 1e-6
LOG2E = math.log2(math.e)

V7X_LANES = 128
V7X_SUBLANES = 8
V7X_BF16_ROWS = 16
VMEM_LIMIT_BYTES = 56 * 1024 * 1024

SSD_HEADS = 32
SSD_HEAD_DIM = 64
SSD_WIDTH = SSD_HEADS * SSD_HEAD_DIM
SSD_GROUPS = 4
SSD_STATE = 128
SSD_CONV = 4
SSD_GROUP_WIDTH = SSD_WIDTH // SSD_GROUPS
SSD_BC_WIDTH = SSD_GROUPS * SSD_STATE
SSD_CONV_WIDTH = SSD_WIDTH + 2 * SSD_BC_WIDTH
MLSTM_HEADS = 8
MLSTM_QK_DIM = 128
MLSTM_V_DIM = 256
MLSTM_WIDTH = MLSTM_HEADS * MLSTM_V_DIM
MLSTM_QK_WIDTH = MLSTM_HEADS * MLSTM_QK_DIM
GATE_SOFTCAP = 15.0

IN_WIDE_BEFORE_DT = SSD_WIDTH + SSD_CONV_WIDTH
IN_WIDE = IN_WIDE_BEFORE_DT + 2 * MLSTM_QK_WIDTH + 2 * MLSTM_WIDTH
IN_I_COL = IN_WIDE + SSD_HEADS
DT_LANE = 0
I_LANE = SSD_HEADS
F_LANE = SSD_HEADS + MLSTM_HEADS
SMALL_WIDTH = V7X_LANES

SSD_CHUNK = 128
MLSTM_CHUNK = 256
MLSTM_SUB = 128
PROJ_TM = 1024
PROJ_TN = 1024
OUT_TM = 1024
OUT_TN = 512
FFN_TM = 512
FFN_TF = 512
FFN_SPLIT = 2
NORM_ROWS = 128
CAST_ROWS = 256


def _dot(a, b):
    return jnp.dot(a, b, preferred_element_type=F32)


def _dot_nt(a, b):
    return lax.dot_general(a, b, (((1,), (1,)), ((), ())), preferred_element_type=F32)


def _split3(x):
    hi = x.astype(BF16)
    r1 = x - hi.astype(F32)
    mid = r1.astype(BF16)
    lo = (r1 - mid.astype(F32)).astype(BF16)
    return hi, mid, lo


def _dot01_left(m01, x):
    hi, mid, lo = _split3(x)
    return _dot(m01, hi) + _dot(m01, mid) + _dot(m01, lo)


def _dot01_right(x, m01):
    hi, mid, lo = _split3(x)
    return _dot(hi, m01) + _dot(mid, m01) + _dot(lo, m01)


def _sigmoid(x):
    return 0.5 * jnp.tanh(0.5 * x) + 0.5


def _softplus(x):
    return jnp.maximum(x, 0.0) + jnp.log1p(jnp.exp(-jnp.abs(x)))


def _log_sigmoid(x):
    return jnp.minimum(x, 0.0) - jnp.log1p(jnp.exp(-jnp.abs(x)))


def _softcap(x):
    return GATE_SOFTCAP * jnp.tanh(x / GATE_SOFTCAP)


def _rmsnorm_rows(src_ref, g_ref, dst_ref, rows):
    g = g_ref[...]

    def body(i, carry):
        r0 = pl.multiple_of(i * NORM_ROWS, NORM_ROWS)
        x = src_ref[pl.ds(r0, NORM_ROWS), :].astype(F32)
        y = x * lax.rsqrt(jnp.mean(x * x, axis=-1, keepdims=True) + EPS)
        dst_ref[pl.ds(r0, NORM_ROWS), :] = (y * g).astype(dst_ref.dtype)
        return carry

    lax.fori_loop(0, rows // NORM_ROWS, body, 0)


def _cast_block(src_ref, dst_ref):
    dst_ref[...] = src_ref[...].astype(dst_ref.dtype)


def _causal_constants(t):
    row = lax.broadcasted_iota(jnp.int32, (t, t), 0)
    col = lax.broadcasted_iota(jnp.int32, (t, t), 1)
    tri = (row >= col).astype(BF16)
    tri_t = (row <= col).astype(BF16)
    neg_mask = jnp.where(row >= col, 0.0, -jnp.inf).astype(F32)
    return tri, tri_t, neg_mask


def _in_proj_first_kernel(x_ref, g_ref, w_ref, wnext_ref, wdt_ref, wif_ref,
                          proj_ref, small_ref, wb_ref, ws_ref, u_ref):
    n = pl.program_id(0)
    gap = wnext_ref.shape[0]

    @pl.when(n == 0)
    def _():
        _rmsnorm_rows(x_ref, g_ref, u_ref, x_ref.shape[0])
        ws_ref[DT_LANE:I_LANE, :] = wdt_ref[...].astype(BF16)
        ws_ref[I_LANE:F_LANE + MLSTM_HEADS, :] = wif_ref[...].astype(BF16)
        ws_ref[F_LANE + MLSTM_HEADS:, :] = jnp.zeros(
            (SMALL_WIDTH - F_LANE - MLSTM_HEADS, ws_ref.shape[1]), BF16)
        small_ref[...] = _dot_nt(u_ref[...], ws_ref[...])

    @pl.when(n * PROJ_TN < IN_WIDE_BEFORE_DT)
    def _():
        def body(i, carry):
            r0 = pl.multiple_of(i * CAST_ROWS, CAST_ROWS)
            wb_ref[pl.ds(r0, CAST_ROWS), :] = w_ref[pl.ds(r0, CAST_ROWS), :].astype(BF16)
            return carry
        lax.fori_loop(0, PROJ_TN // CAST_ROWS, body, 0)

    @pl.when(n * PROJ_TN >= IN_WIDE_BEFORE_DT)
    def _():
        def body(i, carry):
            r0 = pl.multiple_of(i * gap, gap)
            wb_ref[pl.ds(r0, gap), :] = w_ref[pl.ds(r0 + gap, gap), :].astype(BF16)
            return carry
        lax.fori_loop(0, PROJ_TN // gap - 1, body, 0)
        wb_ref[PROJ_TN - gap:, :] = wnext_ref[...].astype(BF16)

    proj_ref[...] = _dot_nt(u_ref[...], wb_ref[...]).astype(proj_ref.dtype)


def _in_proj_rest_kernel(x_ref, g_ref, w_ref, ws_ref, proj_in_ref, small_in_ref,
                         proj_ref, small_ref, u_ref):
    del proj_in_ref, small_in_ref

    @pl.when(pl.program_id(1) == 0)
    def _():
        _rmsnorm_rows(x_ref, g_ref, u_ref, x_ref.shape[0])
        small_ref[...] = _dot_nt(u_ref[...], ws_ref[...])

    proj_ref[...] = _dot_nt(u_ref[...], w_ref[...]).astype(proj_ref.dtype)


def _in_proj(x, g, w_in_t):
    seq, d = x.shape
    n_blocks = IN_WIDE // PROJ_TN
    gap = SSD_HEADS
    if_rows = 2 * MLSTM_HEADS
    proj, small, w_wide, w_small = pl.pallas_call(
        _in_proj_first_kernel,
        grid=(n_blocks,),
        in_specs=[
            pl.BlockSpec((PROJ_TM, d), lambda n: (0, 0)),
            pl.BlockSpec((1, d), lambda n: (0, 0)),
            pl.BlockSpec((PROJ_TN, d), lambda n: (n, 0)),
            pl.BlockSpec((gap, d), lambda n: ((n + 1) * (PROJ_TN // gap), 0)),
            pl.BlockSpec((gap, d), lambda n: (IN_WIDE_BEFORE_DT // gap, 0)),
            pl.BlockSpec((if_rows, d), lambda n: (IN_I_COL // if_rows, 0)),
        ],
        out_specs=[
            pl.BlockSpec((PROJ_TM, PROJ_TN), lambda n: (0, n)),
            pl.BlockSpec((PROJ_TM, SMALL_WIDTH), lambda n: (0, 0)),
            pl.BlockSpec((PROJ_TN, d), lambda n: (n, 0)),
            pl.BlockSpec((SMALL_WIDTH, d), lambda n: (0, 0)),
        ],
        out_shape=[
            jax.ShapeDtypeStruct((seq, IN_WIDE), BF16),
            jax.ShapeDtypeStruct((seq, SMALL_WIDTH), F32),
            jax.ShapeDtypeStruct((IN_WIDE, d), BF16),
            jax.ShapeDtypeStruct((SMALL_WIDTH, d), BF16),
        ],
        scratch_shapes=[pltpu.VMEM((PROJ_TM, d), BF16)],
        compiler_params=pltpu.CompilerParams(
            dimension_semantics=("arbitrary",),
            vmem_limit_bytes=VMEM_LIMIT_BYTES),
        name="in_proj_first",
    )(x, g, w_in_t, w_in_t, w_in_t, w_in_t)

    return pl.pallas_call(
        _in_proj_rest_kernel,
        grid=(seq // PROJ_TM - 1, n_blocks),
        in_specs=[
            pl.BlockSpec((PROJ_TM, d), lambda m, n: (m + 1, 0)),
            pl.BlockSpec((1, d), lambda m, n: (0, 0)),
            pl.BlockSpec((PROJ_TN, d), lambda m, n: (n, 0)),
            pl.BlockSpec((SMALL_WIDTH, d), lambda m, n: (0, 0)),
            pl.BlockSpec(memory_space=pl.ANY),
            pl.BlockSpec(memory_space=pl.ANY),
        ],
        out_specs=[
            pl.BlockSpec((PROJ_TM, PROJ_TN), lambda m, n: (m + 1, n)),
            pl.BlockSpec((PROJ_TM, SMALL_WIDTH), lambda m, n: (m + 1, 0)),
        ],
        out_shape=[
            jax.ShapeDtypeStruct((seq, IN_WIDE), BF16),
            jax.ShapeDtypeStruct((seq, SMALL_WIDTH), F32),
        ],
        scratch_shapes=[pltpu.VMEM((PROJ_TM, d), BF16)],
        input_output_aliases={4: 0, 5: 1},
        compiler_params=pltpu.CompilerParams(
            dimension_semantics=("arbitrary", "arbitrary"),
            vmem_limit_bytes=VMEM_LIMIT_BYTES),
        name="in_proj_rest",
    )(x, g, w_wide, w_small, proj, small)


def _ssd_kernel(z_ref, xs_ref, bc_ref, small_ref, convw_ref, convb_ref, dtb_ref, alog_ref,
                dskip_ref, ng_ref, tri_ref, trit_ref, negmask_ref, expand_ref, shift_ref,
                lanemask_ref, wg_ref, wu_ref,
                y_ref, wg_bf_ref, wu_bf_ref, xcat_ref, state_ref):
    T = SSD_CHUNK
    P = SSD_HEAD_DIM
    NS = SSD_STATE
    GW = SSD_GROUP_WIDTH
    TAIL = V7X_BF16_ROWS

    _cast_block(wg_ref, wg_bf_ref)
    _cast_block(wu_ref, wu_bf_ref)

    @pl.when(pl.program_id(0) == 0)
    def _():
        xcat_ref[0:T, :] = jnp.zeros((T, SSD_CONV_WIDTH), BF16)
        state_ref[...] = jnp.zeros(state_ref.shape, F32)

    xcat_ref[T:2 * T, 0:SSD_WIDTH] = xs_ref[...]
    xcat_ref[T:2 * T, SSD_WIDTH:SSD_CONV_WIDTH] = bc_ref[...]

    def conv_silu(c0, width):
        shifted = _dot(shift_ref[...], xcat_ref[:, c0:c0 + width])
        conv = convb_ref[:, c0:c0 + width] + (
            xcat_ref[T:2 * T, c0:c0 + width].astype(F32) * convw_ref[SSD_CONV - 1:SSD_CONV, c0:c0 + width])
        for back in range(1, SSD_CONV):
            tap = SSD_CONV - 1 - back
            conv = conv + shifted[(back - 1) * T:back * T, :] * convw_ref[tap:tap + 1, c0:c0 + width]
        return conv * _sigmoid(conv)

    lane = lax.broadcasted_iota(jnp.int32, (T, SMALL_WIDTH), 1)
    dt = jnp.where(lane < SSD_HEADS, _softplus(small_ref[...] + dtb_ref[...]), 0.0)
    a_dt = dt * (-jnp.exp(alog_ref[...]))
    col_l = _dot01_left(tri_ref[...], a_dt) * LOG2E
    dt_t = dt.T
    row_l = (_dot01_right(a_dt.T, trit_ref[...]) - jnp.log(dt_t)) * LOG2E
    e_col = jnp.exp2(col_l)
    last_l = col_l[T - 1:T, :]
    w2 = dt * jnp.exp2(last_l - col_l)
    decay_wide = _dot01_right(
        jnp.broadcast_to(jnp.exp2(last_l), (V7X_SUBLANES, SMALL_WIDTH)), expand_ref[...])[0:1, :]

    neg_mask = negmask_ref[...]
    lane_t = lax.broadcasted_iota(jnp.int32, (T, V7X_LANES), 1)
    first_head = lane_t < P
    mask_a = lanemask_ref[0:1, :]
    mask_b = lanemask_ref[1:2, :]

    bc_act = conv_silu(SSD_WIDTH, 2 * SSD_BC_WIDTH)
    for g in range(SSD_GROUPS):
        xs_g = conv_silu(g * GW, GW)
        b_f = bc_act[:, g * NS:(g + 1) * NS]
        c_bf = bc_act[:, SSD_BC_WIDTH + g * NS:SSD_BC_WIDTH + (g + 1) * NS].astype(BF16)
        cb = _dot_nt(c_bf, b_f.astype(BF16))
        state_g = state_ref[g]
        y_off = _dot(c_bf, state_g.astype(BF16))

        y_pairs = []
        xd_pairs = []
        for j in range(GW // V7X_LANES):
            ha = g * (GW // P) + 2 * j
            hb = ha + 1
            m_a = (cb * jnp.exp2(col_l[:, ha:ha + 1] - row_l[ha:ha + 1, :] + neg_mask)).astype(BF16)
            m_b = (cb * jnp.exp2(col_l[:, hb:hb + 1] - row_l[hb:hb + 1, :] + neg_mask)).astype(BF16)
            xp = xs_g[:, j * V7X_LANES:(j + 1) * V7X_LANES]
            xp_bf = xp.astype(BF16)
            y_diag = _dot(jnp.concatenate([m_a, m_b], axis=1),
                          jnp.concatenate([xp_bf * mask_a, xp_bf * mask_b], axis=0))
            off_scale = jnp.where(first_head,
                                  jnp.broadcast_to(e_col[:, ha:ha + 1], (T, V7X_LANES)),
                                  jnp.broadcast_to(e_col[:, hb:hb + 1], (T, V7X_LANES)))
            w2_pair = jnp.where(first_head,
                                jnp.broadcast_to(w2[:, ha:ha + 1], (T, V7X_LANES)),
                                jnp.broadcast_to(w2[:, hb:hb + 1], (T, V7X_LANES)))
            y_pairs.append(y_diag + off_scale * y_off[:, j * V7X_LANES:(j + 1) * V7X_LANES])
            xd_pairs.append((xp * w2_pair).astype(BF16))

        y_g = jnp.concatenate(y_pairs, axis=1) + dskip_ref[:, g * GW:(g + 1) * GW] * xs_g
        xd_g = jnp.concatenate(xd_pairs, axis=1)
        states_g = _dot(b_f.T.astype(BF16), xd_g)
        state_ref[g] = state_g * decay_wide[:, g * GW:(g + 1) * GW] + states_g

        zg = z_ref[:, g * GW:(g + 1) * GW].astype(F32)
        y_g = y_g * (zg * _sigmoid(zg))
        y_g = y_g * lax.rsqrt(jnp.mean(y_g * y_g, axis=-1, keepdims=True) + EPS)
        y_ref[:, g * GW:(g + 1) * GW] = (y_g * ng_ref[:, g * GW:(g + 1) * GW]).astype(y_ref.dtype)

    xcat_ref[T - TAIL:T, :] = xcat_ref[2 * T - TAIL:2 * T, :]


def _ssd(proj, small, conv_w, conv_b, dt_bias_row, a_log_row, d_skip_row, norm_g_row,
         w_gate, w_up):
    seq = proj.shape[0]
    T = SSD_CHUNK
    steps = seq // T
    d, d_ff = w_gate.shape
    slab = d // steps
    assert slab * steps == d and slab % V7X_BF16_ROWS == 0
    tri, tri_t, neg_mask = _causal_constants(T)
    e_row = lax.broadcasted_iota(jnp.int32, (SMALL_WIDTH, SSD_WIDTH), 0)
    e_col = lax.broadcasted_iota(jnp.int32, (SMALL_WIDTH, SSD_WIDTH), 1)
    expand = (e_row == e_col // SSD_HEAD_DIM).astype(BF16)
    s_row = lax.broadcasted_iota(jnp.int32, ((SSD_CONV - 1) * T, 2 * T), 0)
    s_col = lax.broadcasted_iota(jnp.int32, ((SSD_CONV - 1) * T, 2 * T), 1)
    shift = (s_col == T + s_row % T - (s_row // T + 1)).astype(BF16)
    m_lane = lax.broadcasted_iota(jnp.int32, (2, V7X_LANES), 1)
    m_row = lax.broadcasted_iota(jnp.int32, (2, V7X_LANES), 0)
    lane_mask = ((m_lane < SSD_HEAD_DIM) == (m_row == 0)).astype(BF16)

    full = lambda a: pl.BlockSpec(a.shape, lambda c: (0, 0))
    consts = (conv_w, conv_b, dt_bias_row, a_log_row, d_skip_row, norm_g_row,
              tri, tri_t, neg_mask, expand, shift, lane_mask)
    return pl.pallas_call(
        _ssd_kernel,
        grid=(steps,),
        in_specs=[
            pl.BlockSpec((T, SSD_WIDTH), lambda c: (c, 0)),
            pl.BlockSpec((T, SSD_WIDTH), lambda c: (c, 1)),
            pl.BlockSpec((T, 2 * SSD_BC_WIDTH),
                         lambda c: (c, 2 * SSD_WIDTH // (2 * SSD_BC_WIDTH))),
            pl.BlockSpec((T, SMALL_WIDTH), lambda c: (c, 0)),
            *[full(a) for a in consts],
            pl.BlockSpec((slab, d_ff), lambda c: (c, 0)),
            pl.BlockSpec((slab, d_ff), lambda c: (c, 0)),
        ],
        out_specs=[
            pl.BlockSpec((T, SSD_WIDTH), lambda c: (c, 0)),
            pl.BlockSpec((slab, d_ff), lambda c: (c, 0)),
            pl.BlockSpec((slab, d_ff), lambda c: (c, 0)),
        ],
        out_shape=[
            jax.ShapeDtypeStruct((seq, SSD_WIDTH), BF16),
            jax.ShapeDtypeStruct((d, d_ff), BF16),
            jax.ShapeDtypeStruct((d, d_ff), BF16),
        ],
        scratch_shapes=[
            pltpu.VMEM((2 * T, SSD_CONV_WIDTH), BF16),
            pltpu.VMEM((SSD_GROUPS, SSD_STATE, SSD_GROUP_WIDTH), F32),
        ],
        compiler_params=pltpu.CompilerParams(
            dimension_semantics=("arbitrary",),
            vmem_limit_bytes=VMEM_LIMIT_BYTES),
        name="ssd_mixer",
    )(proj, proj, proj, small, *consts, w_gate, w_up)


def _cummax_rows(x):
    rows = x.shape[0]
    row = lax.broadcasted_iota(jnp.int32, x.shape, 0)
    d = 1
    while d < rows:
        if d < V7X_SUBLANES:
            shifted = jnp.where(row < d, -jnp.inf, pltpu.roll(x, d, axis=0))
        else:
            shifted = jnp.concatenate(
                [jnp.full((d, x.shape[1]), -jnp.inf, x.dtype), x[:rows - d]], axis=0)
        x = jnp.maximum(x, shifted)
        d *= 2
    return x


def _mlstm_kernel(q_ref, k_ref, v0_ref, v1_ref, o0_ref, o1_ref, small_ref, ib_ref, fb_ref,
                  ng_ref, tri_ref, negmask_ref, wout_ref, wd_ref,
                  h_ref, wout_bf_ref, wd_bf_ref, ct_ref, nmat_ref, m_ref):
    T = MLSTM_SUB
    DK = MLSTM_QK_DIM
    DV = MLSTM_V_DIM
    log2_scale = math.log2(DK ** -0.5)
    half = MLSTM_HEADS // 2

    _cast_block(wout_ref, wout_bf_ref)
    _cast_block(wd_ref, wd_bf_ref)

    @pl.when(pl.program_id(0) == 0)
    def _():
        ct_ref[...] = jnp.zeros(ct_ref.shape, F32)
        nmat_ref[...] = jnp.zeros(nmat_ref.shape, F32)
        m_ref[...] = jnp.zeros(m_ref.shape, F32)

    neg_mask = negmask_ref[...]
    half_ng = 0.5 * ng_ref[...]
    lane = lax.broadcasted_iota(jnp.int32, (T, SMALL_WIDTH), 1)
    head_lane = (lane >= F_LANE) & (lane < F_LANE + MLSTM_HEADS)
    lane_dk = lax.broadcasted_iota(jnp.int32, (DK, SMALL_WIDTH), 1)

    for sub in range(MLSTM_CHUNK // T):
        r0 = sub * T
        sm = small_ref[r0:r0 + T, :]
        log_i = pltpu.roll(_softcap(sm + ib_ref[...]) * LOG2E, F_LANE - I_LANE, axis=1)
        log_i = jnp.where(head_lane, log_i, 0.0)
        log_f = jnp.where(head_lane, _log_sigmoid(_softcap(sm + fb_ref[...])) * LOG2E, 0.0)
        bcum = _dot01_left(tri_ref[...], log_f)
        r_c = log_i - bcum
        m_prev = m_ref[0:1, :]
        inter_log = bcum + m_prev
        m_t = jnp.maximum(inter_log, bcum + _cummax_rows(r_c))
        col_term = bcum - m_t
        w_inter = jnp.exp2(inter_log - m_t)
        e_neg_m = jnp.exp2(-m_t)
        b_last = bcum[T - 1:T, :]
        a_c = b_last + r_c
        m_loc = jnp.max(a_c, axis=0, keepdims=True)
        m_new = jnp.maximum(b_last + m_prev, m_loc)
        s_old = jnp.exp2(b_last + m_prev - m_new)
        s_new = jnp.exp2(m_loc - m_new)
        w_rows = (jnp.exp2(a_c - m_loc + log2_scale) * s_new).T
        row_term = (r_c + log2_scale).T
        q_all = q_ref[r0:r0 + T, :]
        nq_inter = _dot(q_all, nmat_ref[...].astype(BF16))

        for h in range(MLSTM_HEADS):
            v_ref, o_ref = (v0_ref, o0_ref) if h < half else (v1_ref, o1_ref)
            hv = h % half
            hl = F_LANE + h
            q = q_all[:, h * DK:(h + 1) * DK]
            k = k_ref[r0:r0 + T, h * DK:(h + 1) * DK]
            v = v_ref[r0:r0 + T, hv * DV:(hv + 1) * DV]
            ct = ct_ref[h]

            kw = k.astype(F32).T * w_rows[hl:hl + 1, :]
            c_loc = _dot(kw.astype(BF16), v)
            n_loc = jnp.sum(kw, axis=1, keepdims=True)

            p = _dot_nt(q, k) * jnp.exp2(
                col_term[:, hl:hl + 1] + row_term[hl:hl + 1, :] + neg_mask)
            wi = w_inter[:, hl:hl + 1]
            num = _dot(p.astype(BF16), v) + wi * _dot(q, ct.astype(BF16))
            nq = jnp.sum(p, axis=1, keepdims=True) + wi * nq_inter[:, hl:hl + 1]
            inv_den = 1.0 / jnp.maximum(jnp.abs(nq), e_neg_m[:, hl:hl + 1])
            ms = jnp.mean(num * num, axis=-1, keepdims=True)
            out_scale = inv_den * lax.rsqrt(inv_den * inv_den * ms + EPS)
            hng = half_ng[:, h * DV:(h + 1) * DV]
            og = o_ref[r0:r0 + T, hv * DV:(hv + 1) * DV].astype(F32)
            gate = jnp.tanh(0.5 * og) * hng + hng
            h_ref[r0:r0 + T, h * DV:(h + 1) * DV] = (num * out_scale * gate).astype(h_ref.dtype)

            so = s_old[:, hl:hl + 1]
            ct_ref[h] = so * ct + c_loc
            n_rows = nmat_ref[h * DK:(h + 1) * DK, :]
            nmat_ref[h * DK:(h + 1) * DK, :] = so * n_rows + jnp.where(lane_dk == hl, n_loc, 0.0)

        m_ref[...] = jnp.broadcast_to(m_new, m_ref.shape)


def _mlstm(proj, small, ib_row, fb_row, norm_g_row, w_out, w_down):
    seq = proj.shape[0]
    T = MLSTM_CHUNK
    steps = seq // T
    out_rows, d = w_out.shape
    d_ff = w_down.shape[0]
    slab = out_rows // steps
    slab_d = d_ff // steps
    assert slab * steps == out_rows and slab % V7X_BF16_ROWS == 0
    assert slab_d * steps == d_ff and slab_d % V7X_BF16_ROWS == 0
    half_w = MLSTM_WIDTH // 2
    q_blk = IN_WIDE_BEFORE_DT // MLSTM_QK_WIDTH
    v_blk = (IN_WIDE_BEFORE_DT + 2 * MLSTM_QK_WIDTH) // half_w
    tri, _, neg_mask = _causal_constants(MLSTM_SUB)
    consts = (ib_row, fb_row, norm_g_row, tri, neg_mask)
    full = lambda a: pl.BlockSpec(a.shape, lambda c: (0, 0))
    return pl.pallas_call(
        _mlstm_kernel,
        grid=(steps,),
        in_specs=[
            pl.BlockSpec((T, MLSTM_QK_WIDTH), lambda c: (c, q_blk)),
            pl.BlockSpec((T, MLSTM_QK_WIDTH), lambda c: (c, q_blk + 1)),
            pl.BlockSpec((T, half_w), lambda c: (c, v_blk)),
            pl.BlockSpec((T, half_w), lambda c: (c, v_blk + 1)),
            pl.BlockSpec((T, half_w), lambda c: (c, v_blk + 2)),
            pl.BlockSpec((T, half_w), lambda c: (c, v_blk + 3)),
            pl.BlockSpec((T, SMALL_WIDTH), lambda c: (c, 0)),
            *[full(a) for a in consts],
            pl.BlockSpec((slab, d), lambda c: (c, 0)),
            pl.BlockSpec((slab_d, d), lambda c: (c, 0)),
        ],
        out_specs=[
            pl.BlockSpec((T, MLSTM_WIDTH), lambda c: (c, 0)),
            pl.BlockSpec((slab, d), lambda c: (c, 0)),
            pl.BlockSpec((slab_d, d), lambda c: (c, 0)),
        ],
        out_shape=[
            jax.ShapeDtypeStruct((seq, MLSTM_WIDTH), BF16),
            jax.ShapeDtypeStruct((out_rows, d), BF16),
            jax.ShapeDtypeStruct((d_ff, d), BF16),
        ],
        scratch_shapes=[
            pltpu.VMEM((MLSTM_HEADS, MLSTM_QK_DIM, MLSTM_V_DIM), F32),
            pltpu.VMEM((MLSTM_QK_WIDTH, SMALL_WIDTH), F32),
            pltpu.VMEM((V7X_SUBLANES, SMALL_WIDTH), F32),
        ],
        compiler_params=pltpu.CompilerParams(
            dimension_semantics=("arbitrary",),
            vmem_limit_bytes=VMEM_LIMIT_BYTES),
        name="mlstm_mixer",
    )(proj, proj, proj, proj, proj, proj, small, *consts, w_out, w_down)


def _out_proj_kernel(x_ref, ys_ref, ym_ref, wt_ref, wb_ref, h_ref):
    h_ref[...] = x_ref[...] + _dot(ys_ref[...], wt_ref[...]) + _dot(ym_ref[...], wb_ref[...])


def _out_proj(x, y_ssd, y_ml, w_out_bf):
    seq, d = x.shape
    grid = (seq // OUT_TM, d // OUT_TN)
    return pl.pallas_call(
        _out_proj_kernel,
        grid=grid,
        in_specs=[
            pl.BlockSpec((OUT_TM, OUT_TN), lambda m, n: (m, n)),
            pl.BlockSpec((OUT_TM, SSD_WIDTH), lambda m, n: (m, 0)),
            pl.BlockSpec((OUT_TM, MLSTM_WIDTH), lambda m, n: (m, 0)),
            pl.BlockSpec((SSD_WIDTH, OUT_TN), lambda m, n: (0, n)),
            pl.BlockSpec((MLSTM_WIDTH, OUT_TN), lambda m, n: (1, n)),
        ],
        out_specs=pl.BlockSpec((OUT_TM, OUT_TN), lambda m, n: (m, n)),
        out_shape=jax.ShapeDtypeStruct((seq, d), F32),
        compiler_params=pltpu.CompilerParams(
            dimension_semantics=("arbitrary", "arbitrary"),
            vmem_limit_bytes=VMEM_LIMIT_BYTES),
        name="out_proj",
    )(x, y_ssd, y_ml, w_out_bf, w_out_bf)


def _ffn_kernel(h_ref, g_ref, wg_ref, wu_ref, wd_ref, gf_ref, out_ref, u_ref):
    f = pl.program_id(1)

    @pl.when(f == 0)
    def _():
        _rmsnorm_rows(h_ref, g_ref, u_ref, h_ref.shape[0])
        out_ref[...] = h_ref[...]

    u = u_ref[...]
    sub = wg_ref.shape[1] // FFN_SPLIT
    acc = None
    for s in range(FFN_SPLIT):
        gate = _dot(u, wg_ref[:, s * sub:(s + 1) * sub])
        up = _dot(u, wu_ref[:, s * sub:(s + 1) * sub])
        a = (gate * _sigmoid(gate) * up).astype(BF16)
        part = _dot(a, wd_ref[s * sub:(s + 1) * sub, :])
        acc = part if acc is None else acc + part
    out_ref[...] += acc

    @pl.when(f == pl.num_programs(1) - 1)
    def _():
        _rmsnorm_rows(out_ref, gf_ref, out_ref, out_ref.shape[0])


def _ffn(h1, g, w_gate, w_up, w_down, g_final):
    seq, d = h1.shape
    d_ff = w_gate.shape[1]
    grid = (seq // FFN_TM, d_ff // FFN_TF)
    return pl.pallas_call(
        _ffn_kernel,
        grid=grid,
        in_specs=[
            pl.BlockSpec((FFN_TM, d), lambda m, f: (m, 0)),
            pl.BlockSpec((1, d), lambda m, f: (0, 0)),
            pl.BlockSpec((d, FFN_TF), lambda m, f: (0, f)),
            pl.BlockSpec((d, FFN_TF), lambda m, f: (0, f)),
            pl.BlockSpec((FFN_TF, d), lambda m, f: (f, 0)),
            pl.BlockSpec((1, d), lambda m, f: (0, 0)),
        ],
        out_specs=pl.BlockSpec((FFN_TM, d), lambda m, f: (m, 0)),
        out_shape=jax.ShapeDtypeStruct((seq, d), F32),
        scratch_shapes=[pltpu.VMEM((FFN_TM, d), BF16)],
        compiler_params=pltpu.CompilerParams(
            dimension_semantics=("arbitrary", "arbitrary"),
            vmem_limit_bytes=VMEM_LIMIT_BYTES),
        name="ffn",
    )(h1, g, w_gate, w_up, w_down, g_final)


def _pad_lanes(vec, lane0):
    return jnp.zeros((1, SMALL_WIDTH), F32).at[0, lane0:lane0 + vec.shape[0]].set(vec.astype(F32))


def _layer(h, norm_mix_g, w_in, conv_w, conv_b, dt_bias, a_log, d_skip, ssd_norm_g,
           i_bias, f_bias, mlstm_norm_g, w_out, norm_ffn_g, w_gate, w_up, w_down, out_g):
    d = h.shape[1]
    proj, small = _in_proj(h, norm_mix_g.reshape(1, d), w_in.T)

    y_ssd, wg_bf, wu_bf = _ssd(
        proj, small, conv_w.astype(F32), conv_b.reshape(1, -1).astype(F32),
        _pad_lanes(dt_bias, DT_LANE), _pad_lanes(a_log, DT_LANE),
        jnp.repeat(d_skip.astype(F32), SSD_HEAD_DIM).reshape(1, -1),
        ssd_norm_g.reshape(1, -1).astype(F32), w_gate, w_up)

    ib_row = _pad_lanes(i_bias, I_LANE)
    fb_row = _pad_lanes(f_bias, F_LANE)
    y_ml, w_out_bf, wd_bf = _mlstm(
        proj, small, ib_row, fb_row, mlstm_norm_g.reshape(1, -1).astype(F32), w_out, w_down)

    h1 = _out_proj(h, y_ssd, y_ml, w_out_bf)
    return _ffn(h1, norm_ffn_g.reshape(1, d), wg_bf, wu_bf, wd_bf, out_g.reshape(1, d))


def kernel(x, norm_mix_g, w_in, conv_w, conv_b, dt_bias, a_log, d_skip, ssd_norm_g, i_bias,
           f_bias, mlstm_norm_g, w_out, norm_ffn_g, w_gate, w_up, w_down, final_norm_g):
    batch, seq, d = x.shape
    depth = w_in.shape[0]
    assert batch == 1 and depth == 1, "single-sequence, single-layer problem"
    out = _layer(x.reshape(seq, d), norm_mix_g[0], w_in[0], conv_w[0], conv_b[0], dt_bias[0],
                 a_log[0], d_skip[0], ssd_norm_g[0], i_bias[0], f_bias[0], mlstm_norm_g[0],
                 w_out[0], norm_ffn_g[0], w_gate[0], w_up[0], w_down[0], final_norm_g)
    return out.reshape(batch, seq, d)
```

```python
import math

import jax
import jax.numpy as jnp
from jax import lax
from jax.experimental import pallas as pl
from jax.experimental.pallas import tpu as pltpu

F32 = jnp.float32
BF16 = jnp.bfloat16
EPS = 1e-6
LOG2E = math.log2(math.e)

V7X_LANES = 128
V7X_SUBLANES = 8
V7X_BF16_ROWS = 16
VMEM_LIMIT_BYTES = 56 * 1024 * 1024

SSD_HEADS = 32
SSD_HEAD_DIM = 64
SSD_WIDTH = SSD_HEADS * SSD_HEAD_DIM
SSD_GROUPS = 4
SSD_STATE = 128
SSD_CONV = 4
SSD_GROUP_WIDTH = SSD_WIDTH // SSD_GROUPS
SSD_BC_WIDTH = SSD_GROUPS * SSD_STATE
SSD_CONV_WIDTH = SSD_WIDTH + 2 * SSD_BC_WIDTH
MLSTM_HEADS = 8
MLSTM_QK_DIM = 128
MLSTM_V_DIM = 256
MLSTM_WIDTH = MLSTM_HEADS * MLSTM_V_DIM
MLSTM_QK_WIDTH = MLSTM_HEADS * MLSTM_QK_DIM
GATE_SOFTCAP = 15.0

IN_WIDE_BEFORE_DT = SSD_WIDTH + SSD_CONV_WIDTH
IN_WIDE = IN_WIDE_BEFORE_DT + 2 * MLSTM_QK_WIDTH + 2 * MLSTM_WIDTH
IN_I_COL = IN_WIDE + SSD_HEADS
DT_LANE = 0
I_LANE = SSD_HEADS
F_LANE = SSD_HEADS + MLSTM_HEADS
SMALL_WIDTH = V7X_LANES

SSD_CHUNK = 128
MLSTM_CHUNK = 256
MLSTM_SUB = 128
PROJ_TM = 1024
PROJ_TN = 1024
OUT_TM = 1024
OUT_TN = 512
OUT_PIECES = 16
FFN_TM = 512
FFN_TF = 512
FFN_SPLIT = 2
NORM_ROWS = 128
CAST_ROWS = 256


def _dot(a, b):
    return jnp.dot(a, b, preferred_element_type=F32)


def _dot_nt(a, b):
    return lax.dot_general(a, b, (((1,), (1,)), ((), ())), preferred_element_type=F32)


def _split3(x):
    hi = x.astype(BF16)
    r1 = x - hi.astype(F32)
    mid = r1.astype(BF16)
    lo = (r1 - mid.astype(F32)).astype(BF16)
    return hi, mid, lo


def _dot01_left(m01, x):
    hi, mid, lo = _split3(x)
    return _dot(m01, hi) + _dot(m01, mid) + _dot(m01, lo)


def _dot01_right(x, m01):
    hi, mid, lo = _split3(x)
    return _dot(hi, m01) + _dot(mid, m01) + _dot(lo, m01)


def _sigmoid(x):
    return 0.5 * jnp.tanh(0.5 * x) + 0.5


def _softplus(x):
    return jnp.maximum(x, 0.0) + jnp.log1p(jnp.exp(-jnp.abs(x)))


def _log_sigmoid(x):
    return jnp.minimum(x, 0.0) - jnp.log1p(jnp.exp(-jnp.abs(x)))


def _softcap(x):
    return GATE_SOFTCAP * jnp.tanh(x / GATE_SOFTCAP)


def _rmsnorm_rows(src_ref, g_ref, dst_ref, rows):
    g = g_ref[...]

    def body(i, carry):
        r0 = pl.multiple_of(i * NORM_ROWS, NORM_ROWS)
        x = src_ref[pl.ds(r0, NORM_ROWS), :].astype(F32)
        y = x * lax.rsqrt(jnp.mean(x * x, axis=-1, keepdims=True) + EPS)
        dst_ref[pl.ds(r0, NORM_ROWS), :] = (y * g).astype(dst_ref.dtype)
        return carry

    lax.fori_loop(0, rows // NORM_ROWS, body, 0)


def _cast_block(src_ref, dst_ref):
    dst_ref[...] = src_ref[...].astype(dst_ref.dtype)


def _causal_constants(t):
    row = lax.broadcasted_iota(jnp.int32, (t, t), 0)
    col = lax.broadcasted_iota(jnp.int32, (t, t), 1)
    tri = (row >= col).astype(BF16)
    tri_t = (row <= col).astype(BF16)
    neg_mask = jnp.where(row >= col, 0.0, -jnp.inf).astype(F32)
    return tri, tri_t, neg_mask


def _in_proj_first_kernel(x_ref, g_ref, w_ref, wnext_ref, wdt_ref, wif_ref,
                          proj_ref, small_ref, wb_ref, ws_ref, u_ref):
    n = pl.program_id(0)
    gap = wnext_ref.shape[0]

    @pl.when(n == 0)
    def _():
        _rmsnorm_rows(x_ref, g_ref, u_ref, x_ref.shape[0])
        ws_ref[DT_LANE:I_LANE, :] = wdt_ref[...].astype(BF16)
        ws_ref[I_LANE:F_LANE + MLSTM_HEADS, :] = wif_ref[...].astype(BF16)
        ws_ref[F_LANE + MLSTM_HEADS:, :] = jnp.zeros(
            (SMALL_WIDTH - F_LANE - MLSTM_HEADS, ws_ref.shape[1]), BF16)
        small_ref[...] = _dot_nt(u_ref[...], ws_ref[...])

    @pl.when(n * PROJ_TN < IN_WIDE_BEFORE_DT)
    def _():
        def body(i, carry):
            r0 = pl.multiple_of(i * CAST_ROWS, CAST_ROWS)
            wb_ref[pl.ds(r0, CAST_ROWS), :] = w_ref[pl.ds(r0, CAST_ROWS), :].astype(BF16)
            return carry
        lax.fori_loop(0, PROJ_TN // CAST_ROWS, body, 0)

    @pl.when(n * PROJ_TN >= IN_WIDE_BEFORE_DT)
    def _():
        def body(i, carry):
            r0 = pl.multiple_of(i * gap, gap)
            wb_ref[pl.ds(r0, gap), :] = w_ref[pl.ds(r0 + gap, gap), :].astype(BF16)
            return carry
        lax.fori_loop(0, PROJ_TN // gap - 1, body, 0)
        wb_ref[PROJ_TN - gap:, :] = wnext_ref[...].astype(BF16)

    proj_ref[...] = _dot_nt(u_ref[...], wb_ref[...]).astype(proj_ref.dtype)


def _in_proj_rest_kernel(x_ref, g_ref, w_ref, ws_ref, proj_in_ref, small_in_ref,
                         proj_ref, small_ref, u_ref):
    del proj_in_ref, small_in_ref

    @pl.when(pl.program_id(1) == 0)
    def _():
        _rmsnorm_rows(x_ref, g_ref, u_ref, x_ref.shape[0])
        small_ref[...] = _dot_nt(u_ref[...], ws_ref[...])

    proj_ref[...] = _dot_nt(u_ref[...], w_ref[...]).astype(proj_ref.dtype)


def _in_proj(x, g, w_in_t):
    seq, d = x.shape
    n_blocks = IN_WIDE // PROJ_TN
    gap = SSD_HEADS
    if_rows = 2 * MLSTM_HEADS
    proj, small, w_wide, w_small = pl.pallas_call(
        _in_proj_first_kernel,
        grid=(n_blocks,),
        in_specs=[
            pl.BlockSpec((PROJ_TM, d), lambda n: (0, 0)),
            pl.BlockSpec((1, d), lambda n: (0, 0)),
            pl.BlockSpec((PROJ_TN, d), lambda n: (n, 0)),
            pl.BlockSpec((gap, d), lambda n: ((n + 1) * (PROJ_TN // gap), 0)),
            pl.BlockSpec((gap, d), lambda n: (IN_WIDE_BEFORE_DT // gap, 0)),
            pl.BlockSpec((if_rows, d), lambda n: (IN_I_COL // if_rows, 0)),
        ],
        out_specs=[
            pl.BlockSpec((PROJ_TM, PROJ_TN), lambda n: (0, n)),
            pl.BlockSpec((PROJ_TM, SMALL_WIDTH), lambda n: (0, 0)),
            pl.BlockSpec((PROJ_TN, d), lambda n: (n, 0)),
            pl.BlockSpec((SMALL_WIDTH, d), lambda n: (0, 0)),
        ],
        out_shape=[
            jax.ShapeDtypeStruct((seq, IN_WIDE), BF16),
            jax.ShapeDtypeStruct((seq, SMALL_WIDTH), F32),
            jax.ShapeDtypeStruct((IN_WIDE, d), BF16),
            jax.ShapeDtypeStruct((SMALL_WIDTH, d), BF16),
        ],
        scratch_shapes=[pltpu.VMEM((PROJ_TM, d), BF16)],
        compiler_params=pltpu.CompilerParams(
            dimension_semantics=("arbitrary",),
            vmem_limit_bytes=VMEM_LIMIT_BYTES),
        name="in_proj_first",
    )(x, g, w_in_t, w_in_t, w_in_t, w_in_t)

    return pl.pallas_call(
        _in_proj_rest_kernel,
        grid=(seq // PROJ_TM - 1, n_blocks),
        in_specs=[
            pl.BlockSpec((PROJ_TM, d), lambda m, n: (m + 1, 0)),
            pl.BlockSpec((1, d), lambda m, n: (0, 0)),
            pl.BlockSpec((PROJ_TN, d), lambda m, n: (n, 0)),
            pl.BlockSpec((SMALL_WIDTH, d), lambda m, n: (0, 0)),
            pl.BlockSpec(memory_space=pl.ANY),
            pl.BlockSpec(memory_space=pl.ANY),
        ],
        out_specs=[
            pl.BlockSpec((PROJ_TM, PROJ_TN), lambda m, n: (m + 1, n)),
            pl.BlockSpec((PROJ_TM, SMALL_WIDTH), lambda m, n: (m + 1, 0)),
        ],
        out_shape=[
            jax.ShapeDtypeStruct((seq, IN_WIDE), BF16),
            jax.ShapeDtypeStruct((seq, SMALL_WIDTH), F32),
        ],
        scratch_shapes=[pltpu.VMEM((PROJ_TM, d), BF16)],
        input_output_aliases={4: 0, 5: 1},
        compiler_params=pltpu.CompilerParams(
            dimension_semantics=("arbitrary", "arbitrary"),
            vmem_limit_bytes=VMEM_LIMIT_BYTES),
        name="in_proj_rest",
    )(x, g, w_wide, w_small, proj, small)


def _ssd_kernel(z_ref, xs_ref, bc_ref, small_ref, convw_ref, convb_ref, dtb_ref, alog_ref,
                dskip_ref, ng_ref, tri_ref, trit_ref, negmask_ref, expand_ref, shift_ref,
                lanemask_ref, wg_ref, wu_ref, wo_ref,
                y_ref, wg_bf_ref, wu_bf_ref, wo_bf_ref, xcat_ref, state_ref):
    T = SSD_CHUNK
    P = SSD_HEAD_DIM
    NS = SSD_STATE
    GW = SSD_GROUP_WIDTH
    TAIL = V7X_BF16_ROWS

    _cast_block(wg_ref, wg_bf_ref)
    _cast_block(wu_ref, wu_bf_ref)
    _cast_block(wo_ref, wo_bf_ref)

    @pl.when(pl.program_id(0) == 0)
    def _():
        xcat_ref[0:T, :] = jnp.zeros((T, SSD_CONV_WIDTH), BF16)
        state_ref[...] = jnp.zeros(state_ref.shape, F32)

    xcat_ref[T:2 * T, 0:SSD_WIDTH] = xs_ref[...]
    xcat_ref[T:2 * T, SSD_WIDTH:SSD_CONV_WIDTH] = bc_ref[...]

    def conv_silu(c0, width):
        shifted = _dot(shift_ref[...], xcat_ref[:, c0:c0 + width])
        conv = convb_ref[:, c0:c0 + width] + (
            xcat_ref[T:2 * T, c0:c0 + width].astype(F32) * convw_ref[SSD_CONV - 1:SSD_CONV, c0:c0 + width])
        for back in range(1, SSD_CONV):
            tap = SSD_CONV - 1 - back
            conv = conv + shifted[(back - 1) * T:back * T, :] * convw_ref[tap:tap + 1, c0:c0 + width]
        return conv * _sigmoid(conv)

    lane = lax.broadcasted_iota(jnp.int32, (T, SMALL_WIDTH), 1)
    dt = jnp.where(lane < SSD_HEADS, _softplus(small_ref[...] + dtb_ref[...]), 0.0)
    a_dt = dt * (-jnp.exp(alog_ref[...]))
    col_l = _dot01_left(tri_ref[...], a_dt) * LOG2E
    dt_t = dt.T
    row_l = (_dot01_right(a_dt.T, trit_ref[...]) - jnp.log(dt_t)) * LOG2E
    e_col = jnp.exp2(col_l)
    last_l = col_l[T - 1:T, :]
    w2 = dt * jnp.exp2(last_l - col_l)
    decay_wide = _dot01_right(
        jnp.broadcast_to(jnp.exp2(last_l), (V7X_SUBLANES, SMALL_WIDTH)), expand_ref[...])[0:1, :]

    neg_mask = negmask_ref[...]
    lane_t = lax.broadcasted_iota(jnp.int32, (T, V7X_LANES), 1)
    first_head = lane_t < P
    mask_a = lanemask_ref[0:1, :]
    mask_b = lanemask_ref[1:2, :]

    bc_act = conv_silu(SSD_WIDTH, 2 * SSD_BC_WIDTH)
    for g in range(SSD_GROUPS):
        xs_g = conv_silu(g * GW, GW)
        b_f = bc_act[:, g * NS:(g + 1) * NS]
        c_bf = bc_act[:, SSD_BC_WIDTH + g * NS:SSD_BC_WIDTH + (g + 1) * NS].astype(BF16)
        cb = _dot_nt(c_bf, b_f.astype(BF16))
        state_g = state_ref[g]
        y_off = _dot(c_bf, state_g.astype(BF16))

        y_pairs = []
        xd_pairs = []
        for j in range(GW // V7X_LANES):
            ha = g * (GW // P) + 2 * j
            hb = ha + 1
            m_a = (cb * jnp.exp2(col_l[:, ha:ha + 1] - row_l[ha:ha + 1, :] + neg_mask)).astype(BF16)
            m_b = (cb * jnp.exp2(col_l[:, hb:hb + 1] - row_l[hb:hb + 1, :] + neg_mask)).astype(BF16)
            xp = xs_g[:, j * V7X_LANES:(j + 1) * V7X_LANES]
            xp_bf = xp.astype(BF16)
            y_diag = _dot(jnp.concatenate([m_a, m_b], axis=1),
                          jnp.concatenate([xp_bf * mask_a, xp_bf * mask_b], axis=0))
            off_scale = jnp.where(first_head,
                                  jnp.broadcast_to(e_col[:, ha:ha + 1], (T, V7X_LANES)),
                                  jnp.broadcast_to(e_col[:, hb:hb + 1], (T, V7X_LANES)))
            w2_pair = jnp.where(first_head,
                                jnp.broadcast_to(w2[:, ha:ha + 1], (T, V7X_LANES)),
                                jnp.broadcast_to(w2[:, hb:hb + 1], (T, V7X_LANES)))
            y_pairs.append(y_diag + off_scale * y_off[:, j * V7X_LANES:(j + 1) * V7X_LANES])
            xd_pairs.append((xp * w2_pair).astype(BF16))

        y_g = jnp.concatenate(y_pairs, axis=1) + dskip_ref[:, g * GW:(g + 1) * GW] * xs_g
        xd_g = jnp.concatenate(xd_pairs, axis=1)
        states_g = _dot(b_f.T.astype(BF16), xd_g)
        state_ref[g] = state_g * decay_wide[:, g * GW:(g + 1) * GW] + states_g

        zg = z_ref[:, g * GW:(g + 1) * GW].astype(F32)
        y_g = y_g * (zg * _sigmoid(zg))
        y_g = y_g * lax.rsqrt(jnp.mean(y_g * y_g, axis=-1, keepdims=True) + EPS)
        y_ref[:, g * GW:(g + 1) * GW] = (y_g * ng_ref[:, g * GW:(g + 1) * GW]).astype(y_ref.dtype)

    xcat_ref[T - TAIL:T, :] = xcat_ref[2 * T - TAIL:2 * T, :]


def _ssd(proj, small, conv_w, conv_b, dt_bias_row, a_log_row, d_skip_row, norm_g_row,
         w_gate, w_up, w_out):
    seq = proj.shape[0]
    T = SSD_CHUNK
    steps = seq // T
    d, d_ff = w_gate.shape
    slab = d // steps
    slab_o = w_out.shape[0] // steps
    assert slab * steps == d and slab % V7X_BF16_ROWS == 0
    assert slab_o * steps == w_out.shape[0] and slab_o % V7X_BF16_ROWS == 0
    tri, tri_t, neg_mask = _causal_constants(T)
    e_row = lax.broadcasted_iota(jnp.int32, (SMALL_WIDTH, SSD_WIDTH), 0)
    e_col = lax.broadcasted_iota(jnp.int32, (SMALL_WIDTH, SSD_WIDTH), 1)
    expand = (e_row == e_col // SSD_HEAD_DIM).astype(BF16)
    s_row = lax.broadcasted_iota(jnp.int32, ((SSD_CONV - 1) * T, 2 * T), 0)
    s_col = lax.broadcasted_iota(jnp.int32, ((SSD_CONV - 1) * T, 2 * T), 1)
    shift = (s_col == T + s_row % T - (s_row // T + 1)).astype(BF16)
    m_lane = lax.broadcasted_iota(jnp.int32, (2, V7X_LANES), 1)
    m_row = lax.broadcasted_iota(jnp.int32, (2, V7X_LANES), 0)
    lane_mask = ((m_lane < SSD_HEAD_DIM) == (m_row == 0)).astype(BF16)

    full = lambda a: pl.BlockSpec(a.shape, lambda c: (0, 0))
    consts = (conv_w, conv_b, dt_bias_row, a_log_row, d_skip_row, norm_g_row,
              tri, tri_t, neg_mask, expand, shift, lane_mask)
    return pl.pallas_call(
        _ssd_kernel,
        grid=(steps,),
        in_specs=[
            pl.BlockSpec((T, SSD_WIDTH), lambda c: (c, 0)),
            pl.BlockSpec((T, SSD_WIDTH), lambda c: (c, 1)),
            pl.BlockSpec((T, 2 * SSD_BC_WIDTH),
                         lambda c: (c, 2 * SSD_WIDTH // (2 * SSD_BC_WIDTH))),
            pl.BlockSpec((T, SMALL_WIDTH), lambda c: (c, 0)),
            *[full(a) for a in consts],
            pl.BlockSpec((slab, d_ff), lambda c: (c, 0)),
            pl.BlockSpec((slab, d_ff), lambda c: (c, 0)),
            pl.BlockSpec((slab_o, d), lambda c: (c, 0)),
        ],
        out_specs=[
            pl.BlockSpec((T, SSD_WIDTH), lambda c: (c, 0)),
            pl.BlockSpec((slab, d_ff), lambda c: (c, 0)),
            pl.BlockSpec((slab, d_ff), lambda c: (c, 0)),
            pl.BlockSpec((slab_o, d), lambda c: (c, 0)),
        ],
        out_shape=[
            jax.ShapeDtypeStruct((seq, SSD_WIDTH), BF16),
            jax.ShapeDtypeStruct((d, d_ff), BF16),
            jax.ShapeDtypeStruct((d, d_ff), BF16),
            jax.ShapeDtypeStruct(w_out.shape, BF16),
        ],
        scratch_shapes=[
            pltpu.VMEM((2 * T, SSD_CONV_WIDTH), BF16),
            pltpu.VMEM((SSD_GROUPS, SSD_STATE, SSD_GROUP_WIDTH), F32),
        ],
        compiler_params=pltpu.CompilerParams(
            dimension_semantics=("arbitrary",),
            vmem_limit_bytes=VMEM_LIMIT_BYTES),
        name="ssd_mixer",
    )(proj, proj, proj, small, *consts, w_gate, w_up, w_out)


def _cummax_rows(x):
    rows = x.shape[0]
    row = lax.broadcasted_iota(jnp.int32, x.shape, 0)
    d = 1
    while d < rows:
        if d < V7X_SUBLANES:
            shifted = jnp.where(row < d, -jnp.inf, pltpu.roll(x, d, axis=0))
        else:
            shifted = jnp.concatenate(
                [jnp.full((d, x.shape[1]), -jnp.inf, x.dtype), x[:rows - d]], axis=0)
        x = jnp.maximum(x, shifted)
        d *= 2
    return x


def _mlstm_chunk(q_ref, k_ref, v0_ref, v1_ref, o0_ref, o1_ref, small_ref, ib_ref, fb_ref,
                 ng_ref, tri_ref, negmask_ref, ct_ref, nmat_ref, m_ref, h_ref, h_row0):
    T = MLSTM_SUB
    DK = MLSTM_QK_DIM
    DV = MLSTM_V_DIM
    log2_scale = math.log2(DK ** -0.5)
    half = MLSTM_HEADS // 2

    neg_mask = negmask_ref[...]
    half_ng = 0.5 * ng_ref[...]
    lane = lax.broadcasted_iota(jnp.int32, (T, SMALL_WIDTH), 1)
    head_lane = (lane >= F_LANE) & (lane < F_LANE + MLSTM_HEADS)
    lane_dk = lax.broadcasted_iota(jnp.int32, (DK, SMALL_WIDTH), 1)

    for sub in range(MLSTM_CHUNK // T):
        r0 = sub * T
        sm = small_ref[r0:r0 + T, :]
        log_i = pltpu.roll(_softcap(sm + ib_ref[...]) * LOG2E, F_LANE - I_LANE, axis=1)
        log_i = jnp.where(head_lane, log_i, 0.0)
        log_f = jnp.where(head_lane, _log_sigmoid(_softcap(sm + fb_ref[...])) * LOG2E, 0.0)
        bcum = _dot01_left(tri_ref[...], log_f)
        r_c = log_i - bcum
        m_prev = m_ref[0:1, :]
        inter_log = bcum + m_prev
        m_t = jnp.maximum(inter_log, bcum + _cummax_rows(r_c))
        col_term = bcum - m_t
        w_inter = jnp.exp2(inter_log - m_t)
        e_neg_m = jnp.exp2(-m_t)
        b_last = bcum[T - 1:T, :]
        a_c = b_last + r_c
        m_loc = jnp.max(a_c, axis=0, keepdims=True)
        m_new = jnp.maximum(b_last + m_prev, m_loc)
        s_old = jnp.exp2(b_last + m_prev - m_new)
        s_new = jnp.exp2(m_loc - m_new)
        w_rows = (jnp.exp2(a_c - m_loc + log2_scale) * s_new).T
        row_term = (r_c + log2_scale).T
        q_all = q_ref[r0:r0 + T, :]
        nq_inter = _dot(q_all, nmat_ref[...].astype(BF16))
        yield

        for h in range(MLSTM_HEADS):
            v_ref, o_ref = (v0_ref, o0_ref) if h < half else (v1_ref, o1_ref)
            hv = h % half
            hl = F_LANE + h
            q = q_all[:, h * DK:(h + 1) * DK]
            k = k_ref[r0:r0 + T, h * DK:(h + 1) * DK]
            v = v_ref[r0:r0 + T, hv * DV:(hv + 1) * DV]
            ct = ct_ref[h]

            kw = k.astype(F32).T * w_rows[hl:hl + 1, :]
            c_loc = _dot(kw.astype(BF16), v)
            n_loc = jnp.sum(kw, axis=1, keepdims=True)

            p = _dot_nt(q, k) * jnp.exp2(
                col_term[:, hl:hl + 1] + row_term[hl:hl + 1, :] + neg_mask)
            wi = w_inter[:, hl:hl + 1]
            num = _dot(p.astype(BF16), v) + wi * _dot(q, ct.astype(BF16))
            nq = jnp.sum(p, axis=1, keepdims=True) + wi * nq_inter[:, hl:hl + 1]
            inv_den = 1.0 / jnp.maximum(jnp.abs(nq), e_neg_m[:, hl:hl + 1])
            ms = jnp.mean(num * num, axis=-1, keepdims=True)
            out_scale = inv_den * lax.rsqrt(inv_den * inv_den * ms + EPS)
            hng = half_ng[:, h * DV:(h + 1) * DV]
            og = o_ref[r0:r0 + T, hv * DV:(hv + 1) * DV].astype(F32)
            gate = jnp.tanh(0.5 * og) * hng + hng
            h_ref[pl.ds(h_row0 + r0, T), h * DV:(h + 1) * DV] = (
                num * out_scale * gate).astype(h_ref.dtype)

            so = s_old[:, hl:hl + 1]
            ct_ref[h] = so * ct + c_loc
            n_rows = nmat_ref[h * DK:(h + 1) * DK, :]
            nmat_ref[h * DK:(h + 1) * DK, :] = so * n_rows + jnp.where(lane_dk == hl, n_loc, 0.0)
            yield

        m_ref[...] = jnp.broadcast_to(m_new, m_ref.shape)


def _mlstm_out_kernel(x_ref, ys_ref, wt_ref, wb_ref,
                      q_ref, k_ref, v0_ref, v1_ref, o0_ref, o1_ref, small_ref, ib_ref, fb_ref,
                      ng_ref, tri_ref, negmask_ref, wd_ref,
                      h1_ref, wd_bf_ref,
                      yml_even_ref, yml_odd_ref, ct_ref, nmat_ref, m_ref):
    m = pl.program_id(0)
    n = pl.program_id(1)
    last = pl.num_programs(0) - 1

    _cast_block(wd_ref, wd_bf_ref)

    @pl.when((m == 0) & (n == 0))
    def _():
        ct_ref[...] = jnp.zeros(ct_ref.shape, F32)
        nmat_ref[...] = jnp.zeros(nmat_ref.shape, F32)
        m_ref[...] = jnp.zeros(m_ref.shape, F32)

    def project(yml_ref):
        col_w = OUT_TN // 2
        k_w = 2 * SSD_WIDTH // (OUT_PIECES // 2)
        for piece in range(OUT_PIECES):
            cols = slice((piece % 2) * col_w, (piece % 2 + 1) * col_w)
            k0 = (piece // 2) * k_w
            src, w_ref = (ys_ref, wt_ref) if k0 < SSD_WIDTH else (yml_ref, wb_ref)
            k0 %= SSD_WIDTH
            part = _dot(src[:, k0:k0 + k_w], w_ref[k0:k0 + k_w, cols])
            if piece < 2:
                h1_ref[:, cols] = x_ref[:, cols] + part
            else:
                h1_ref[:, cols] += part
            yield

    def mix(yml_ref):
        return _mlstm_chunk(q_ref, k_ref, v0_ref, v1_ref, o0_ref, o1_ref, small_ref, ib_ref,
                            fb_ref, ng_ref, tri_ref, negmask_ref, ct_ref, nmat_ref, m_ref,
                            yml_ref, pl.multiple_of(n * MLSTM_CHUNK, MLSTM_CHUNK))

    def run_both(read_ref, write_ref):
        seg_total = (MLSTM_CHUNK // MLSTM_SUB) * (MLSTM_HEADS + 1)
        segs = mix(write_ref)
        done = 0
        for i, _ in enumerate(project(read_ref)):
            want = -(-(i + 1) * seg_total // OUT_PIECES)
            while done < want and next(segs, "end") != "end":
                done += 1
        for _ in segs:
            pass

    @pl.when(m == 0)
    def _():
        for _ in mix(yml_even_ref):
            pass

    @pl.when((m > 0) & (m < last) & (m % 2 == 1))
    def _():
        run_both(yml_even_ref, yml_odd_ref)

    @pl.when((m > 0) & (m < last) & (m % 2 == 0))
    def _():
        run_both(yml_odd_ref, yml_even_ref)

    @pl.when(m == last)
    def _():
        for _ in project(yml_odd_ref):
            pass


def _mlstm_out(x, y_ssd, w_out_bf, proj, small, ib_row, fb_row, norm_g_row, w_down):
    seq, d = x.shape
    T = MLSTM_CHUNK
    tiles = seq // OUT_TM
    n_tiles = d // OUT_TN
    assert tiles % 2 == 0 and OUT_TM == n_tiles * T
    steps = (tiles + 1) * n_tiles
    d_ff = w_down.shape[0]
    slab_d = -(-d_ff // steps)
    slab_d += -slab_d % V7X_BF16_ROWS
    half_w = MLSTM_WIDTH // 2
    q_blk = IN_WIDE_BEFORE_DT // MLSTM_QK_WIDTH
    v_blk = (IN_WIDE_BEFORE_DT + 2 * MLSTM_QK_WIDTH) // half_w
    tri, _, neg_mask = _causal_constants(MLSTM_SUB)
    consts = (ib_row, fb_row, norm_g_row, tri, neg_mask)
    full = lambda a: pl.BlockSpec(a.shape, lambda m, n: (0, 0))
    prev = lambda m: jnp.maximum(m - 1, 0)
    chunk = lambda m, n: jnp.minimum(m * n_tiles + n, seq // T - 1)
    n_slabs = -(-d_ff // slab_d)
    slab = lambda m, n: (jnp.minimum(m * n_tiles + n, n_slabs - 1), 0)
    h1, wd_bf = pl.pallas_call(
        _mlstm_out_kernel,
        grid=(tiles + 1, n_tiles),
        in_specs=[
            pl.BlockSpec((OUT_TM, OUT_TN), lambda m, n: (prev(m), n)),
            pl.BlockSpec((OUT_TM, SSD_WIDTH), lambda m, n: (prev(m), 0)),
            pl.BlockSpec((SSD_WIDTH, OUT_TN), lambda m, n: (0, n)),
            pl.BlockSpec((MLSTM_WIDTH, OUT_TN), lambda m, n: (1, n)),
            pl.BlockSpec((T, MLSTM_QK_WIDTH), lambda m, n: (chunk(m, n), q_blk)),
            pl.BlockSpec((T, MLSTM_QK_WIDTH), lambda m, n: (chunk(m, n), q_blk + 1)),
            pl.BlockSpec((T, half_w), lambda m, n: (chunk(m, n), v_blk)),
            pl.BlockSpec((T, half_w), lambda m, n: (chunk(m, n), v_blk + 1)),
            pl.BlockSpec((T, half_w), lambda m, n: (chunk(m, n), v_blk + 2)),
            pl.BlockSpec((T, half_w), lambda m, n: (chunk(m, n), v_blk + 3)),
            pl.BlockSpec((T, SMALL_WIDTH), lambda m, n: (chunk(m, n), 0)),
            *[full(a) for a in consts],
            pl.BlockSpec((slab_d, d), slab),
        ],
        out_specs=[
            pl.BlockSpec((OUT_TM, OUT_TN), lambda m, n: (prev(m), jnp.where(m == 0, 0, n))),
            pl.BlockSpec((slab_d, d), slab),
        ],
        out_shape=[
            jax.ShapeDtypeStruct((seq, d), F32),
            jax.ShapeDtypeStruct((d_ff, d), BF16),
        ],
        scratch_shapes=[
            pltpu.VMEM((OUT_TM, MLSTM_WIDTH), BF16),
            pltpu.VMEM((OUT_TM, MLSTM_WIDTH), BF16),
            pltpu.VMEM((MLSTM_HEADS, MLSTM_QK_DIM, MLSTM_V_DIM), F32),
            pltpu.VMEM((MLSTM_QK_WIDTH, SMALL_WIDTH), F32),
            pltpu.VMEM((V7X_SUBLANES, SMALL_WIDTH), F32),
        ],
        compiler_params=pltpu.CompilerParams(
            dimension_semantics=("arbitrary", "arbitrary"),
            vmem_limit_bytes=VMEM_LIMIT_BYTES),
        name="mlstm_out",
    )(x, y_ssd, w_out_bf, w_out_bf, proj, proj, proj, proj, proj, proj, small, *consts, w_down)
    return h1, wd_bf


def _ffn_kernel(h_ref, g_ref, wg_ref, wu_ref, wd_ref, gf_ref, out_ref, u_ref):
    f = pl.program_id(1)

    @pl.when(f == 0)
    def _():
        _rmsnorm_rows(h_ref, g_ref, u_ref, h_ref.shape[0])
        out_ref[...] = h_ref[...]

    u = u_ref[...]
    sub = wg_ref.shape[1] // FFN_SPLIT
    acc = None
    for s in range(FFN_SPLIT):
        gate = _dot(u, wg_ref[:, s * sub:(s + 1) * sub])
        up = _dot(u, wu_ref[:, s * sub:(s + 1) * sub])
        a = (gate * _sigmoid(gate) * up).astype(BF16)
        part = _dot(a, wd_ref[s * sub:(s + 1) * sub, :])
        acc = part if acc is None else acc + part
    out_ref[...] += acc

    @pl.when(f == pl.num_programs(1) - 1)
    def _():
        _rmsnorm_rows(out_ref, gf_ref, out_ref, out_ref.shape[0])


def _ffn(h1, g, w_gate, w_up, w_down, g_final):
    seq, d = h1.shape
    d_ff = w_gate.shape[1]
    grid = (seq // FFN_TM, d_ff // FFN_TF)
    return pl.pallas_call(
        _ffn_kernel,
        grid=grid,
        in_specs=[
            pl.BlockSpec((FFN_TM, d), lambda m, f: (m, 0)),
            pl.BlockSpec((1, d), lambda m, f: (0, 0)),
            pl.BlockSpec((d, FFN_TF), lambda m, f: (0, f)),
            pl.BlockSpec((d, FFN_TF), lambda m, f: (0, f)),
            pl.BlockSpec((FFN_TF, d), lambda m, f: (f, 0)),
            pl.BlockSpec((1, d), lambda m, f: (0, 0)),
        ],
        out_specs=pl.BlockSpec((FFN_TM, d), lambda m, f: (m, 0)),
        out_shape=jax.ShapeDtypeStruct((seq, d), F32),
        scratch_shapes=[pltpu.VMEM((FFN_TM, d), BF16)],
        compiler_params=pltpu.CompilerParams(
            dimension_semantics=("arbitrary", "arbitrary"),
            vmem_limit_bytes=VMEM_LIMIT_BYTES),
        name="ffn",
    )(h1, g, w_gate, w_up, w_down, g_final)


def _pad_lanes(vec, lane0):
    return jnp.zeros((1, SMALL_WIDTH), F32).at[0, lane0:lane0 + vec.shape[0]].set(vec.astype(F32))


def _layer(h, norm_mix_g, w_in, conv_w, conv_b, dt_bias, a_log, d_skip, ssd_norm_g,
           i_bias, f_bias, mlstm_norm_g, w_out, norm_ffn_g, w_gate, w_up, w_down, out_g):
    d = h.shape[1]
    proj, small = _in_proj(h, norm_mix_g.reshape(1, d), w_in.T)

    y_ssd, wg_bf, wu_bf, w_out_bf = _ssd(
        proj, small, conv_w.astype(F32), conv_b.reshape(1, -1).astype(F32),
        _pad_lanes(dt_bias, DT_LANE), _pad_lanes(a_log, DT_LANE),
        jnp.repeat(d_skip.astype(F32), SSD_HEAD_DIM).reshape(1, -1),
        ssd_norm_g.reshape(1, -1).astype(F32), w_gate, w_up, w_out)

    h1, wd_bf = _mlstm_out(
        h, y_ssd, w_out_bf, proj, small, _pad_lanes(i_bias, I_LANE), _pad_lanes(f_bias, F_LANE),
        mlstm_norm_g.reshape(1, -1).astype(F32), w_down)
    return _ffn(h1, norm_ffn_g.reshape(1, d), wg_bf, wu_bf, wd_bf, out_g.reshape(1, d))


def kernel(x, norm_mix_g, w_in, conv_w, conv_b, dt_bias, a_log, d_skip, ssd_norm_g, i_bias,
           f_bias, mlstm_norm_g, w_out, norm_ffn_g, w_gate, w_up, w_down, final_norm_g):
    batch, seq, d = x.shape
    depth = w_in.shape[0]
    assert batch == 1 and depth == 1, "single-sequence, single-layer problem"
    out = _layer(x.reshape(seq, d), norm_mix_g[0], w_in[0], conv_w[0], conv_b[0], dt_bias[0],
                 a_log[0], d_skip[0], ssd_norm_g[0], i_bias[0], f_bias[0], mlstm_norm_g[0],
                 w_out[0], norm_ffn_g[0], w_gate[0], w_up[0], w_down[0], final_norm_g)
    return out.reshape(batch, seq, d)
```

```python
import functools
import math

import jax
import jax.numpy as jnp
from jax import lax
from jax.experimental import pallas as pl
from jax.experimental.pallas import tpu as pltpu

F32 = jnp.float32
BF16 = jnp.bfloat16
EPS = 1e-6
LOG2E = math.log2(math.e)

V7X_LANES = 128
V7X_SUBLANES = 8
V7X_BF16_ROWS = 16
VMEM_LIMIT_BYTES = 56 * 1024 * 1024

SSD_HEADS = 32
SSD_HEAD_DIM = 64
SSD_WIDTH = SSD_HEADS * SSD_HEAD_DIM
SSD_GROUPS = 4
SSD_STATE = 128
SSD_CONV = 4
SSD_GROUP_WIDTH = SSD_WIDTH // SSD_GROUPS
SSD_BC_WIDTH = SSD_GROUPS * SSD_STATE
SSD_CONV_WIDTH = SSD_WIDTH + 2 * SSD_BC_WIDTH
MLSTM_HEADS = 8
MLSTM_QK_DIM = 128
MLSTM_V_DIM = 256
MLSTM_WIDTH = MLSTM_HEADS * MLSTM_V_DIM
MLSTM_QK_WIDTH = MLSTM_HEADS * MLSTM_QK_DIM
GATE_SOFTCAP = 15.0

IN_WIDE_BEFORE_DT = SSD_WIDTH + SSD_CONV_WIDTH
IN_WIDE = IN_WIDE_BEFORE_DT + 2 * MLSTM_QK_WIDTH + 2 * MLSTM_WIDTH
IN_I_COL = IN_WIDE + SSD_HEADS
DT_LANE = 0
I_LANE = SSD_HEADS
F_LANE = SSD_HEADS + MLSTM_HEADS
SMALL_WIDTH = V7X_LANES

SSD_CHUNK = 128
MLSTM_CHUNK = 256
MLSTM_SUB = 128
PROJ_TM = 1024
PROJ_TN = 1024
OUT_TM = 1024
OUT_TN = 512
OUT_PIECES = 16
FFN_TM = 512
FFN_TF = 512
FFN_SPLIT = 2
NORM_ROWS = 128
CAST_ROWS = 256


def _dot(a, b):
    return jnp.dot(a, b, preferred_element_type=F32)


def _dot_nt(a, b):
    return lax.dot_general(a, b, (((1,), (1,)), ((), ())), preferred_element_type=F32)


def _split3(x):
    hi = x.astype(BF16)
    r1 = x - hi.astype(F32)
    mid = r1.astype(BF16)
    lo = (r1 - mid.astype(F32)).astype(BF16)
    return hi, mid, lo


def _dot01_left(m01, x):
    hi, mid, lo = _split3(x)
    return _dot(m01, hi) + _dot(m01, mid) + _dot(m01, lo)


def _dot01_right(x, m01):
    hi, mid, lo = _split3(x)
    return _dot(hi, m01) + _dot(mid, m01) + _dot(lo, m01)


def _sigmoid(x):
    return 0.5 * jnp.tanh(0.5 * x) + 0.5


def _softplus(x):
    return jnp.maximum(x, 0.0) + jnp.log1p(jnp.exp(-jnp.abs(x)))


def _log_sigmoid(x):
    return jnp.minimum(x, 0.0) - jnp.log1p(jnp.exp(-jnp.abs(x)))


def _softcap(x):
    return GATE_SOFTCAP * jnp.tanh(x / GATE_SOFTCAP)


def _rmsnorm_rows(src_ref, g_ref, dst_ref, rows):
    g = g_ref[...]

    def body(i, carry):
        r0 = pl.multiple_of(i * NORM_ROWS, NORM_ROWS)
        x = src_ref[pl.ds(r0, NORM_ROWS), :].astype(F32)
        y = x * lax.rsqrt(jnp.mean(x * x, axis=-1, keepdims=True) + EPS)
        dst_ref[pl.ds(r0, NORM_ROWS), :] = (y * g).astype(dst_ref.dtype)
        return carry

    lax.fori_loop(0, rows // NORM_ROWS, body, 0)


def _cast_block(src_ref, dst_ref):
    dst_ref[...] = src_ref[...].astype(dst_ref.dtype)


def _causal_constants(t):
    row = lax.broadcasted_iota(jnp.int32, (t, t), 0)
    col = lax.broadcasted_iota(jnp.int32, (t, t), 1)
    tri = (row >= col).astype(BF16)
    tri_t = (row <= col).astype(BF16)
    neg_mask = jnp.where(row >= col, 0.0, -jnp.inf).astype(F32)
    return tri, tri_t, neg_mask


def _in_proj_first_kernel(x_ref, g_ref, w_ref, wnext_ref, wdt_ref, wif_ref,
                          proj_ref, small_ref, wb_ref, ws_ref, u_ref):
    n = pl.program_id(0)
    gap = wnext_ref.shape[0]

    @pl.when(n == 0)
    def _():
        _rmsnorm_rows(x_ref, g_ref, u_ref, x_ref.shape[0])
        ws_ref[DT_LANE:I_LANE, :] = wdt_ref[...].astype(BF16)
        ws_ref[I_LANE:F_LANE + MLSTM_HEADS, :] = wif_ref[...].astype(BF16)
        ws_ref[F_LANE + MLSTM_HEADS:, :] = jnp.zeros(
            (SMALL_WIDTH - F_LANE - MLSTM_HEADS, ws_ref.shape[1]), BF16)
        small_ref[...] = _dot_nt(u_ref[...], ws_ref[...])

    @pl.when(n * PROJ_TN < IN_WIDE_BEFORE_DT)
    def _():
        def body(i, carry):
            r0 = pl.multiple_of(i * CAST_ROWS, CAST_ROWS)
            wb_ref[pl.ds(r0, CAST_ROWS), :] = w_ref[pl.ds(r0, CAST_ROWS), :].astype(BF16)
            return carry
        lax.fori_loop(0, PROJ_TN // CAST_ROWS, body, 0)

    @pl.when(n * PROJ_TN >= IN_WIDE_BEFORE_DT)
    def _():
        def body(i, carry):
            r0 = pl.multiple_of(i * gap, gap)
            wb_ref[pl.ds(r0, gap), :] = w_ref[pl.ds(r0 + gap, gap), :].astype(BF16)
            return carry
        lax.fori_loop(0, PROJ_TN // gap - 1, body, 0)
        wb_ref[PROJ_TN - gap:, :] = wnext_ref[...].astype(BF16)

    proj_ref[...] = _dot_nt(u_ref[...], wb_ref[...]).astype(proj_ref.dtype)


def _in_proj_rest_kernel(n_cast, x_ref, g_ref, w_ref, ws_ref, proj_in_ref, small_in_ref, *rest):
    del proj_in_ref, small_in_ref
    cast_in, (proj_ref, small_ref), cast_out, (u_ref,) = (
        rest[:n_cast], rest[n_cast:n_cast + 2], rest[n_cast + 2:2 * n_cast + 2], rest[2 * n_cast + 2:])
    for src_ref, dst_ref in zip(cast_in, cast_out):
        _cast_block(src_ref, dst_ref)

    @pl.when(pl.program_id(1) == 0)
    def _():
        _rmsnorm_rows(x_ref, g_ref, u_ref, x_ref.shape[0])
        small_ref[...] = _dot_nt(u_ref[...], ws_ref[...])

    proj_ref[...] = _dot_nt(u_ref[...], w_ref[...]).astype(proj_ref.dtype)


def _in_proj(x, g, w_in_t, later_weights):
    seq, d = x.shape
    n_blocks = IN_WIDE // PROJ_TN
    gap = SSD_HEADS
    if_rows = 2 * MLSTM_HEADS
    proj, small, w_wide, w_small = pl.pallas_call(
        _in_proj_first_kernel,
        grid=(n_blocks,),
        in_specs=[
            pl.BlockSpec((PROJ_TM, d), lambda n: (0, 0)),
            pl.BlockSpec((1, d), lambda n: (0, 0)),
            pl.BlockSpec((PROJ_TN, d), lambda n: (n, 0)),
            pl.BlockSpec((gap, d), lambda n: ((n + 1) * (PROJ_TN // gap), 0)),
            pl.BlockSpec((gap, d), lambda n: (IN_WIDE_BEFORE_DT // gap, 0)),
            pl.BlockSpec((if_rows, d), lambda n: (IN_I_COL // if_rows, 0)),
        ],
        out_specs=[
            pl.BlockSpec((PROJ_TM, PROJ_TN), lambda n: (0, n)),
            pl.BlockSpec((PROJ_TM, SMALL_WIDTH), lambda n: (0, 0)),
            pl.BlockSpec((PROJ_TN, d), lambda n: (n, 0)),
            pl.BlockSpec((SMALL_WIDTH, d), lambda n: (0, 0)),
        ],
        out_shape=[
            jax.ShapeDtypeStruct((seq, IN_WIDE), BF16),
            jax.ShapeDtypeStruct((seq, SMALL_WIDTH), F32),
            jax.ShapeDtypeStruct((IN_WIDE, d), BF16),
            jax.ShapeDtypeStruct((SMALL_WIDTH, d), BF16),
        ],
        scratch_shapes=[pltpu.VMEM((PROJ_TM, d), BF16)],
        compiler_params=pltpu.CompilerParams(
            dimension_semantics=("arbitrary",),
            vmem_limit_bytes=VMEM_LIMIT_BYTES),
        name="in_proj_first",
    )(x, g, w_in_t, w_in_t, w_in_t, w_in_t)

    m_tiles = seq // PROJ_TM - 1
    steps = m_tiles * n_blocks
    cast_specs = []
    for w in later_weights:
        slab = -(-w.shape[0] // steps)
        slab += -slab % V7X_BF16_ROWS
        last = -(-w.shape[0] // slab) - 1
        cast_specs.append(pl.BlockSpec(
            (slab, w.shape[1]),
            lambda m, n, last=last: (jnp.minimum(m * n_blocks + n, last), 0)))
    outs = pl.pallas_call(
        functools.partial(_in_proj_rest_kernel, len(later_weights)),
        grid=(m_tiles, n_blocks),
        in_specs=[
            pl.BlockSpec((PROJ_TM, d), lambda m, n: (m + 1, 0)),
            pl.BlockSpec((1, d), lambda m, n: (0, 0)),
            pl.BlockSpec((PROJ_TN, d), lambda m, n: (n, 0)),
            pl.BlockSpec((SMALL_WIDTH, d), lambda m, n: (0, 0)),
            pl.BlockSpec(memory_space=pl.ANY),
            pl.BlockSpec(memory_space=pl.ANY),
            *cast_specs,
        ],
        out_specs=[
            pl.BlockSpec((PROJ_TM, PROJ_TN), lambda m, n: (m + 1, n)),
            pl.BlockSpec((PROJ_TM, SMALL_WIDTH), lambda m, n: (m + 1, 0)),
            *cast_specs,
        ],
        out_shape=[
            jax.ShapeDtypeStruct((seq, IN_WIDE), BF16),
            jax.ShapeDtypeStruct((seq, SMALL_WIDTH), F32),
            *[jax.ShapeDtypeStruct(w.shape, BF16) for w in later_weights],
        ],
        scratch_shapes=[pltpu.VMEM((PROJ_TM, d), BF16)],
        input_output_aliases={4: 0, 5: 1},
        compiler_params=pltpu.CompilerParams(
            dimension_semantics=("arbitrary", "arbitrary"),
            vmem_limit_bytes=VMEM_LIMIT_BYTES),
        name="in_proj_rest",
    )(x, g, w_wide, w_small, proj, small, *later_weights)
    return outs[0], outs[1], outs[2:]


def _ssd_chunk(z_ref, xs_ref, bc_ref, small_ref, convw_ref, convb_ref, dtb_ref, alog_ref,
               dskip_ref, ng_ref, tri_ref, trit_ref, negmask_ref, expand_ref, shift_ref,
               lanemask_ref, xcat_ref, state_ref, row0, y_ref, y_row0):
    T = SSD_CHUNK
    P = SSD_HEAD_DIM
    NS = SSD_STATE
    GW = SSD_GROUP_WIDTH
    TAIL = V7X_BF16_ROWS

    xcat_ref[T:2 * T, 0:SSD_WIDTH] = xs_ref[row0:row0 + T, :]
    xcat_ref[T:2 * T, SSD_WIDTH:SSD_CONV_WIDTH] = bc_ref[row0:row0 + T, :]

    def conv_silu(c0, width):
        shifted = _dot(shift_ref[...], xcat_ref[:, c0:c0 + width])
        conv = convb_ref[:, c0:c0 + width] + (
            xcat_ref[T:2 * T, c0:c0 + width].astype(F32) * convw_ref[SSD_CONV - 1:SSD_CONV, c0:c0 + width])
        for back in range(1, SSD_CONV):
            tap = SSD_CONV - 1 - back
            conv = conv + shifted[(back - 1) * T:back * T, :] * convw_ref[tap:tap + 1, c0:c0 + width]
        return conv * _sigmoid(conv)

    lane = lax.broadcasted_iota(jnp.int32, (T, SMALL_WIDTH), 1)
    dt = jnp.where(lane < SSD_HEADS, _softplus(small_ref[row0:row0 + T, :] + dtb_ref[...]), 0.0)
    a_dt = dt * (-jnp.exp(alog_ref[...]))
    col_l = _dot01_left(tri_ref[...], a_dt) * LOG2E
    dt_t = dt.T
    row_l = (_dot01_right(a_dt.T, trit_ref[...]) - jnp.log(dt_t)) * LOG2E
    e_col = jnp.exp2(col_l)
    last_l = col_l[T - 1:T, :]
    w2 = dt * jnp.exp2(last_l - col_l)
    decay_wide = _dot01_right(
        jnp.broadcast_to(jnp.exp2(last_l), (V7X_SUBLANES, SMALL_WIDTH)), expand_ref[...])[0:1, :]

    neg_mask = negmask_ref[...]
    lane_t = lax.broadcasted_iota(jnp.int32, (T, V7X_LANES), 1)
    first_head = lane_t < P
    mask_a = lanemask_ref[0:1, :]
    mask_b = lanemask_ref[1:2, :]

    bc_act = conv_silu(SSD_WIDTH, 2 * SSD_BC_WIDTH)
    yield
    for g in range(SSD_GROUPS):
        xs_g = conv_silu(g * GW, GW)
        b_f = bc_act[:, g * NS:(g + 1) * NS]
        c_bf = bc_act[:, SSD_BC_WIDTH + g * NS:SSD_BC_WIDTH + (g + 1) * NS].astype(BF16)
        cb = _dot_nt(c_bf, b_f.astype(BF16))
        state_g = state_ref[g]
        y_off = _dot(c_bf, state_g.astype(BF16))

        y_pairs = []
        xd_pairs = []
        for j in range(GW // V7X_LANES):
            ha = g * (GW // P) + 2 * j
            hb = ha + 1
            m_a = (cb * jnp.exp2(col_l[:, ha:ha + 1] - row_l[ha:ha + 1, :] + neg_mask)).astype(BF16)
            m_b = (cb * jnp.exp2(col_l[:, hb:hb + 1] - row_l[hb:hb + 1, :] + neg_mask)).astype(BF16)
            xp = xs_g[:, j * V7X_LANES:(j + 1) * V7X_LANES]
            xp_bf = xp.astype(BF16)
            y_diag = _dot(jnp.concatenate([m_a, m_b], axis=1),
                          jnp.concatenate([xp_bf * mask_a, xp_bf * mask_b], axis=0))
            off_scale = jnp.where(first_head,
                                  jnp.broadcast_to(e_col[:, ha:ha + 1], (T, V7X_LANES)),
                                  jnp.broadcast_to(e_col[:, hb:hb + 1], (T, V7X_LANES)))
            w2_pair = jnp.where(first_head,
                                jnp.broadcast_to(w2[:, ha:ha + 1], (T, V7X_LANES)),
                                jnp.broadcast_to(w2[:, hb:hb + 1], (T, V7X_LANES)))
            y_pairs.append(y_diag + off_scale * y_off[:, j * V7X_LANES:(j + 1) * V7X_LANES])
            xd_pairs.append((xp * w2_pair).astype(BF16))

        y_g = jnp.concatenate(y_pairs, axis=1) + dskip_ref[:, g * GW:(g + 1) * GW] * xs_g
        xd_g = jnp.concatenate(xd_pairs, axis=1)
        states_g = _dot(b_f.T.astype(BF16), xd_g)
        state_ref[g] = state_g * decay_wide[:, g * GW:(g + 1) * GW] + states_g

        zg = z_ref[row0:row0 + T, g * GW:(g + 1) * GW].astype(F32)
        y_g = y_g * (zg * _sigmoid(zg))
        y_g = y_g * lax.rsqrt(jnp.mean(y_g * y_g, axis=-1, keepdims=True) + EPS)
        y_ref[pl.ds(y_row0, T), g * GW:(g + 1) * GW] = (
            y_g * ng_ref[:, g * GW:(g + 1) * GW]).astype(y_ref.dtype)
        if g == SSD_GROUPS - 1:
            xcat_ref[T - TAIL:T, :] = xcat_ref[2 * T - TAIL:2 * T, :]
        yield


def _ssd_constants():
    T = SSD_CHUNK
    e_row = lax.broadcasted_iota(jnp.int32, (SMALL_WIDTH, SSD_WIDTH), 0)
    e_col = lax.broadcasted_iota(jnp.int32, (SMALL_WIDTH, SSD_WIDTH), 1)
    expand = (e_row == e_col // SSD_HEAD_DIM).astype(BF16)
    s_row = lax.broadcasted_iota(jnp.int32, ((SSD_CONV - 1) * T, 2 * T), 0)
    s_col = lax.broadcasted_iota(jnp.int32, ((SSD_CONV - 1) * T, 2 * T), 1)
    shift = (s_col == T + s_row % T - (s_row // T + 1)).astype(BF16)
    m_lane = lax.broadcasted_iota(jnp.int32, (2, V7X_LANES), 1)
    m_row = lax.broadcasted_iota(jnp.int32, (2, V7X_LANES), 0)
    lane_mask = ((m_lane < SSD_HEAD_DIM) == (m_row == 0)).astype(BF16)
    return expand, shift, lane_mask


def _cummax_rows(x):
    rows = x.shape[0]
    row = lax.broadcasted_iota(jnp.int32, x.shape, 0)
    d = 1
    while d < rows:
        if d < V7X_SUBLANES:
            shifted = jnp.where(row < d, -jnp.inf, pltpu.roll(x, d, axis=0))
        else:
            shifted = jnp.concatenate(
                [jnp.full((d, x.shape[1]), -jnp.inf, x.dtype), x[:rows - d]], axis=0)
        x = jnp.maximum(x, shifted)
        d *= 2
    return x


def _mlstm_chunk(q_ref, k_ref, v0_ref, v1_ref, o0_ref, o1_ref, small_ref, ib_ref, fb_ref,
                 ng_ref, tri_ref, negmask_ref, ct_ref, nmat_ref, m_ref, h_ref, h_row0):
    T = MLSTM_SUB
    DK = MLSTM_QK_DIM
    DV = MLSTM_V_DIM
    log2_scale = math.log2(DK ** -0.5)
    half = MLSTM_HEADS // 2

    neg_mask = negmask_ref[...]
    half_ng = 0.5 * ng_ref[...]
    lane = lax.broadcasted_iota(jnp.int32, (T, SMALL_WIDTH), 1)
    head_lane = (lane >= F_LANE) & (lane < F_LANE + MLSTM_HEADS)
    lane_dk = lax.broadcasted_iota(jnp.int32, (DK, SMALL_WIDTH), 1)

    for sub in range(MLSTM_CHUNK // T):
        r0 = sub * T
        sm = small_ref[r0:r0 + T, :]
        log_i = pltpu.roll(_softcap(sm + ib_ref[...]) * LOG2E, F_LANE - I_LANE, axis=1)
        log_i = jnp.where(head_lane, log_i, 0.0)
        log_f = jnp.where(head_lane, _log_sigmoid(_softcap(sm + fb_ref[...])) * LOG2E, 0.0)
        bcum = _dot01_left(tri_ref[...], log_f)
        r_c = log_i - bcum
        m_prev = m_ref[0:1, :]
        inter_log = bcum + m_prev
        m_t = jnp.maximum(inter_log, bcum + _cummax_rows(r_c))
        col_term = bcum - m_t
        w_inter = jnp.exp2(inter_log - m_t)
        e_neg_m = jnp.exp2(-m_t)
        b_last = bcum[T - 1:T, :]
        a_c = b_last + r_c
        m_loc = jnp.max(a_c, axis=0, keepdims=True)
        m_new = jnp.maximum(b_last + m_prev, m_loc)
        s_old = jnp.exp2(b_last + m_prev - m_new)
        s_new = jnp.exp2(m_loc - m_new)
        w_rows = (jnp.exp2(a_c - m_loc + log2_scale) * s_new).T
        row_term = (r_c + log2_scale).T
        q_all = q_ref[r0:r0 + T, :]
        nq_inter = _dot(q_all, nmat_ref[...].astype(BF16))
        yield

        for h in range(MLSTM_HEADS):
            v_ref, o_ref = (v0_ref, o0_ref) if h < half else (v1_ref, o1_ref)
            hv = h % half
            hl = F_LANE + h
            q = q_all[:, h * DK:(h + 1) * DK]
            k = k_ref[r0:r0 + T, h * DK:(h + 1) * DK]
            v = v_ref[r0:r0 + T, hv * DV:(hv + 1) * DV]
            ct = ct_ref[h]

            kw = k.astype(F32).T * w_rows[hl:hl + 1, :]
            c_loc = _dot(kw.astype(BF16), v)
            n_loc = jnp.sum(kw, axis=1, keepdims=True)

            p = _dot_nt(q, k) * jnp.exp2(
                col_term[:, hl:hl + 1] + row_term[hl:hl + 1, :] + neg_mask)
            wi = w_inter[:, hl:hl + 1]
            num = _dot(p.astype(BF16), v) + wi * _dot(q, ct.astype(BF16))
            nq = jnp.sum(p, axis=1, keepdims=True) + wi * nq_inter[:, hl:hl + 1]
            inv_den = 1.0 / jnp.maximum(jnp.abs(nq), e_neg_m[:, hl:hl + 1])
            ms = jnp.mean(num * num, axis=-1, keepdims=True)
            out_scale = inv_den * lax.rsqrt(inv_den * inv_den * ms + EPS)
            hng = half_ng[:, h * DV:(h + 1) * DV]
            og = o_ref[r0:r0 + T, hv * DV:(hv + 1) * DV].astype(F32)
            gate = jnp.tanh(0.5 * og) * hng + hng
            h_ref[pl.ds(h_row0 + r0, T), h * DV:(h + 1) * DV] = (
                num * out_scale * gate).astype(h_ref.dtype)

            so = s_old[:, hl:hl + 1]
            ct_ref[h] = so * ct + c_loc
            n_rows = nmat_ref[h * DK:(h + 1) * DK, :]
            nmat_ref[h * DK:(h + 1) * DK, :] = so * n_rows + jnp.where(lane_dk == hl, n_loc, 0.0)
            yield

        m_ref[...] = jnp.broadcast_to(m_new, m_ref.shape)


def _run_interleaved(*streams):
    streams = [(gen, count) for gen, count in streams]
    done = [0] * len(streams)
    total = max(count for _, count in streams)
    for tick in range(1, total + 1):
        for i, (gen, count) in enumerate(streams):
            want = -(-tick * count // total)
            while done[i] < want:
                next(gen, None)
                done[i] += 1
    for gen, _ in streams:
        for _ in gen:
            pass


def _mixers_out_kernel(x_ref, wt_ref, wb_ref,
                       z_ref, xs_ref, bc_ref, small_ref, convw_ref, convb_ref, dtb_ref, alog_ref,
                       dskip_ref, sng_ref, tri_ref, trit_ref, negmask_ref, expand_ref, shift_ref,
                       lanemask_ref,
                       q_ref, k_ref, v0_ref, v1_ref, o0_ref, o1_ref, ib_ref, fb_ref, mng_ref,
                       h1_ref,
                       ys_even_ref, ys_odd_ref, ym_even_ref, ym_odd_ref,
                       xcat_ref, state_ref, ct_ref, nmat_ref, m_ref):
    m = pl.program_id(0)
    n = pl.program_id(1)
    last = pl.num_programs(0) - 1
    row0 = pl.multiple_of(n * MLSTM_CHUNK, MLSTM_CHUNK)
    ssd_chunks = MLSTM_CHUNK // SSD_CHUNK

    @pl.when((m == 0) & (n == 0))
    def _():
        xcat_ref[0:SSD_CHUNK, :] = jnp.zeros((SSD_CHUNK, SSD_CONV_WIDTH), BF16)
        state_ref[...] = jnp.zeros(state_ref.shape, F32)
        ct_ref[...] = jnp.zeros(ct_ref.shape, F32)
        nmat_ref[...] = jnp.zeros(nmat_ref.shape, F32)
        m_ref[...] = jnp.zeros(m_ref.shape, F32)

    def project(ys_ref, ym_ref):
        col_w = OUT_TN // 2
        k_w = 2 * SSD_WIDTH // (OUT_PIECES // 2)
        for piece in range(OUT_PIECES):
            cols = slice((piece % 2) * col_w, (piece % 2 + 1) * col_w)
            k0 = (piece // 2) * k_w
            src, w_ref = (ys_ref, wt_ref) if k0 < SSD_WIDTH else (ym_ref, wb_ref)
            k0 %= SSD_WIDTH
            part = _dot(src[:, k0:k0 + k_w], w_ref[k0:k0 + k_w, cols])
            if piece < 2:
                h1_ref[:, cols] = x_ref[:, cols] + part
            else:
                h1_ref[:, cols] += part
            yield

    def mix_ssd(ys_ref):
        for c in range(ssd_chunks):
            yield from _ssd_chunk(z_ref, xs_ref, bc_ref, small_ref, convw_ref, convb_ref, dtb_ref,
                                  alog_ref, dskip_ref, sng_ref, tri_ref, trit_ref, negmask_ref,
                                  expand_ref, shift_ref, lanemask_ref, xcat_ref, state_ref,
                                  c * SSD_CHUNK, ys_ref, row0 + c * SSD_CHUNK)

    def mix_mlstm(ym_ref):
        return _mlstm_chunk(q_ref, k_ref, v0_ref, v1_ref, o0_ref, o1_ref, small_ref, ib_ref,
                            fb_ref, mng_ref, tri_ref, negmask_ref, ct_ref, nmat_ref, m_ref,
                            ym_ref, row0)

    ssd_segs = ssd_chunks * (SSD_GROUPS + 1)
    mlstm_segs = (MLSTM_CHUNK // MLSTM_SUB) * (MLSTM_HEADS + 1)

    def run(read, write):
        streams = []
        if read is not None:
            streams.append((project(*read), OUT_PIECES))
        if write is not None:
            streams += [(mix_ssd(write[0]), ssd_segs), (mix_mlstm(write[1]), mlstm_segs)]
        _run_interleaved(*streams)

    even = (ys_even_ref, ym_even_ref)
    odd = (ys_odd_ref, ym_odd_ref)

    @pl.when(m == 0)
    def _():
        run(None, even)

    @pl.when((m > 0) & (m < last) & (m % 2 == 1))
    def _():
        run(even, odd)

    @pl.when((m > 0) & (m < last) & (m % 2 == 0))
    def _():
        run(odd, even)

    @pl.when(m == last)
    def _():
        run(odd, None)


def _mixers_out(x, w_out_bf, proj, small, ssd_params, mlstm_params):
    seq, d = x.shape
    T = MLSTM_CHUNK
    tiles = seq // OUT_TM
    n_tiles = d // OUT_TN
    assert tiles % 2 == 0 and OUT_TM == n_tiles * T and SSD_CHUNK == MLSTM_SUB
    half_w = MLSTM_WIDTH // 2
    q_blk = IN_WIDE_BEFORE_DT // MLSTM_QK_WIDTH
    v_blk = (IN_WIDE_BEFORE_DT + 2 * MLSTM_QK_WIDTH) // half_w
    tri, tri_t, neg_mask = _causal_constants(SSD_CHUNK)
    consts_ssd = (*ssd_params, tri, tri_t, neg_mask, *_ssd_constants())
    full = lambda a: pl.BlockSpec(a.shape, lambda m, n: (0, 0))
    prev = lambda m: jnp.maximum(m - 1, 0)
    chunk = lambda m, n: jnp.minimum(m * n_tiles + n, seq // T - 1)
    rows = lambda width, blk: pl.BlockSpec((T, width), lambda m, n: (chunk(m, n), blk))
    return pl.pallas_call(
        _mixers_out_kernel,
        grid=(tiles + 1, n_tiles),
        in_specs=[
            pl.BlockSpec((OUT_TM, OUT_TN), lambda m, n: (prev(m), n)),
            pl.BlockSpec((SSD_WIDTH, OUT_TN), lambda m, n: (0, n)),
            pl.BlockSpec((MLSTM_WIDTH, OUT_TN), lambda m, n: (1, n)),
            rows(SSD_WIDTH, 0),
            rows(SSD_WIDTH, 1),
            rows(2 * SSD_BC_WIDTH, 2 * SSD_WIDTH // (2 * SSD_BC_WIDTH)),
            rows(SMALL_WIDTH, 0),
            *[full(a) for a in consts_ssd],
            rows(MLSTM_QK_WIDTH, q_blk),
            rows(MLSTM_QK_WIDTH, q_blk + 1),
            rows(half_w, v_blk),
            rows(half_w, v_blk + 1),
            rows(half_w, v_blk + 2),
            rows(half_w, v_blk + 3),
            *[full(a) for a in mlstm_params],
        ],
        out_specs=pl.BlockSpec((OUT_TM, OUT_TN), lambda m, n: (prev(m), jnp.where(m == 0, 0, n))),
        out_shape=jax.ShapeDtypeStruct((seq, d), F32),
        scratch_shapes=[
            pltpu.VMEM((OUT_TM, SSD_WIDTH), BF16),
            pltpu.VMEM((OUT_TM, SSD_WIDTH), BF16),
            pltpu.VMEM((OUT_TM, MLSTM_WIDTH), BF16),
            pltpu.VMEM((OUT_TM, MLSTM_WIDTH), BF16),
            pltpu.VMEM((2 * SSD_CHUNK, SSD_CONV_WIDTH), BF16),
            pltpu.VMEM((SSD_GROUPS, SSD_STATE, SSD_GROUP_WIDTH), F32),
            pltpu.VMEM((MLSTM_HEADS, MLSTM_QK_DIM, MLSTM_V_DIM), F32),
            pltpu.VMEM((MLSTM_QK_WIDTH, SMALL_WIDTH), F32),
            pltpu.VMEM((V7X_SUBLANES, SMALL_WIDTH), F32),
        ],
        compiler_params=pltpu.CompilerParams(
            dimension_semantics=("arbitrary", "arbitrary"),
            vmem_limit_bytes=VMEM_LIMIT_BYTES),
        name="mixers_out",
    )(x, w_out_bf, w_out_bf, proj, proj, proj, small, *consts_ssd,
      proj, proj, proj, proj, proj, proj, *mlstm_params)


def _ffn_kernel(h_ref, g_ref, wg_ref, wu_ref, wd_ref, gf_ref, out_ref, u_ref):
    f = pl.program_id(1)

    @pl.when(f == 0)
    def _():
        _rmsnorm_rows(h_ref, g_ref, u_ref, h_ref.shape[0])
        out_ref[...] = h_ref[...]

    u = u_ref[...]
    sub = wg_ref.shape[1] // FFN_SPLIT
    acc = None
    for s in range(FFN_SPLIT):
        gate = _dot(u, wg_ref[:, s * sub:(s + 1) * sub])
        up = _dot(u, wu_ref[:, s * sub:(s + 1) * sub])
        a = (gate * _sigmoid(gate) * up).astype(BF16)
        part = _dot(a, wd_ref[s * sub:(s + 1) * sub, :])
        acc = part if acc is None else acc + part
    out_ref[...] += acc

    @pl.when(f == pl.num_programs(1) - 1)
    def _():
        _rmsnorm_rows(out_ref, gf_ref, out_ref, out_ref.shape[0])


def _ffn(h1, g, w_gate, w_up, w_down, g_final):
    seq, d = h1.shape
    d_ff = w_gate.shape[1]
    grid = (seq // FFN_TM, d_ff // FFN_TF)
    return pl.pallas_call(
        _ffn_kernel,
        grid=grid,
        in_specs=[
            pl.BlockSpec((FFN_TM, d), lambda m, f: (m, 0)),
            pl.BlockSpec((1, d), lambda m, f: (0, 0)),
            pl.BlockSpec((d, FFN_TF), lambda m, f: (0, f)),
            pl.BlockSpec((d, FFN_TF), lambda m, f: (0, f)),
            pl.BlockSpec((FFN_TF, d), lambda m, f: (f, 0)),
            pl.BlockSpec((1, d), lambda m, f: (0, 0)),
        ],
        out_specs=pl.BlockSpec((FFN_TM, d), lambda m, f: (m, 0)),
        out_shape=jax.ShapeDtypeStruct((seq, d), F32),
        scratch_shapes=[pltpu.VMEM((FFN_TM, d), BF16)],
        compiler_params=pltpu.CompilerParams(
            dimension_semantics=("arbitrary", "arbitrary"),
            vmem_limit_bytes=VMEM_LIMIT_BYTES),
        name="ffn",
    )(h1, g, w_gate, w_up, w_down, g_final)


def _pad_lanes(vec, lane0):
    return jnp.zeros((1, SMALL_WIDTH), F32).at[0, lane0:lane0 + vec.shape[0]].set(vec.astype(F32))


def _layer(h, norm_mix_g, w_in, conv_w, conv_b, dt_bias, a_log, d_skip, ssd_norm_g,
           i_bias, f_bias, mlstm_norm_g, w_out, norm_ffn_g, w_gate, w_up, w_down, out_g):
    d = h.shape[1]
    proj, small, (w_out_bf, wg_bf, wu_bf, wd_bf) = _in_proj(
        h, norm_mix_g.reshape(1, d), w_in.T, (w_out, w_gate, w_up, w_down))

    ssd_params = (conv_w.astype(F32), conv_b.reshape(1, -1).astype(F32),
                  _pad_lanes(dt_bias, DT_LANE), _pad_lanes(a_log, DT_LANE),
                  jnp.repeat(d_skip.astype(F32), SSD_HEAD_DIM).reshape(1, -1),
                  ssd_norm_g.reshape(1, -1).astype(F32))
    mlstm_params = (_pad_lanes(i_bias, I_LANE), _pad_lanes(f_bias, F_LANE),
                    mlstm_norm_g.reshape(1, -1).astype(F32))
    h1 = _mixers_out(h, w_out_bf, proj, small, ssd_params, mlstm_params)
    return _ffn(h1, norm_ffn_g.reshape(1, d), wg_bf, wu_bf, wd_bf, out_g.reshape(1, d))


def kernel(x, norm_mix_g, w_in, conv_w, conv_b, dt_bias, a_log, d_skip, ssd_norm_g, i_bias,
           f_bias, mlstm_norm_g, w_out, norm_ffn_g, w_gate, w_up, w_down, final_norm_g):
    batch, seq, d = x.shape
    depth = w_in.shape[0]
    assert batch == 1 and depth == 1, "single-sequence, single-layer problem"
    out = _layer(x.reshape(seq, d), norm_mix_g[0], w_in[0], conv_w[0], conv_b[0], dt_bias[0],
                 a_log[0], d_skip[0], ssd_norm_g[0], i_bias[0], f_bias[0], mlstm_norm_g[0],
                 w_out[0], norm_ffn_g[0], w_gate[0], w_up[0], w_down[0], final_norm_g)
    return out.reshape(batch, seq, d)
```

```python
import math

import jax
import jax.numpy as jnp
from jax import lax
from jax.experimental import pallas as pl
from jax.experimental.pallas import tpu as pltpu

F32 = jnp.float32
BF16 = jnp.bfloat16
EPS = 1e-6
LOG2E = math.log2(math.e)

V7X_LANES = 128
V7X_SUBLANES = 8
V7X_BF16_ROWS = 16
VMEM_LIMIT_BYTES = 56 * 1024 * 1024

SSD_HEADS = 32
SSD_HEAD_DIM = 64
SSD_WIDTH = SSD_HEADS * SSD_HEAD_DIM
SSD_GROUPS = 4
SSD_STATE = 128
SSD_CONV = 4
SSD_GROUP_WIDTH = SSD_WIDTH // SSD_GROUPS
SSD_BC_WIDTH = SSD_GROUPS * SSD_STATE
SSD_CONV_WIDTH = SSD_WIDTH + 2 * SSD_BC_WIDTH
MLSTM_HEADS = 8
MLSTM_QK_DIM = 128
MLSTM_V_DIM = 256
MLSTM_WIDTH = MLSTM_HEADS * MLSTM_V_DIM
MLSTM_QK_WIDTH = MLSTM_HEADS * MLSTM_QK_DIM
GATE_SOFTCAP = 15.0

IN_WIDE_BEFORE_DT = SSD_WIDTH + SSD_CONV_WIDTH
IN_O_COL = IN_WIDE_BEFORE_DT + 2 * MLSTM_QK_WIDTH + MLSTM_WIDTH
IN_WIDE = IN_O_COL + MLSTM_WIDTH
IN_I_COL = IN_WIDE + SSD_HEADS
DT_LANE = 0
I_LANE = SSD_HEADS
F_LANE = SSD_HEADS + MLSTM_HEADS
SMALL_WIDTH = V7X_LANES

SSD_CHUNK = 128
MLSTM_CHUNK = 256
MLSTM_SUB = 128
PROJ_TM = 1024
PROJ_TN = 1024
PROJ_FIRST_TM = 2048
PROJ_FIRST_TN = 512
PROJ_PIECE = 256
OUT_TM = 1024
OUT_TN = 512
OUT_PIECES = 16
FFN_TM = 512
FFN_TF = 512
FFN_SPLIT = 2
NORM_ROWS = 128
CAST_ROWS = 256


def _dot(a, b):
    return jnp.dot(a, b, preferred_element_type=F32)


def _dot_nt(a, b):
    return lax.dot_general(a, b, (((1,), (1,)), ((), ())), preferred_element_type=F32)


def _split3(x):
    hi = x.astype(BF16)
    r1 = x - hi.astype(F32)
    mid = r1.astype(BF16)
    lo = (r1 - mid.astype(F32)).astype(BF16)
    return hi, mid, lo


def _dot01_left(m01, x):
    hi, mid, lo = _split3(x)
    return _dot(m01, hi) + _dot(m01, mid) + _dot(m01, lo)


def _dot01_right(x, m01):
    hi, mid, lo = _split3(x)
    return _dot(hi, m01) + _dot(mid, m01) + _dot(lo, m01)


def _sigmoid(x):
    return 0.5 * jnp.tanh(0.5 * x) + 0.5


def _softplus(x):
    return jnp.maximum(x, 0.0) + jnp.log1p(jnp.exp(-jnp.abs(x)))


def _log_sigmoid(x):
    return jnp.minimum(x, 0.0) - jnp.log1p(jnp.exp(-jnp.abs(x)))


def _softcap(x):
    return GATE_SOFTCAP * jnp.tanh(x / GATE_SOFTCAP)


def _rmsnorm_rows(src_ref, g_ref, dst_ref, rows):
    g = g_ref[...]

    def body(i, carry):
        r0 = pl.multiple_of(i * NORM_ROWS, NORM_ROWS)
        x = src_ref[pl.ds(r0, NORM_ROWS), :].astype(F32)
        y = x * lax.rsqrt(jnp.mean(x * x, axis=-1, keepdims=True) + EPS)
        dst_ref[pl.ds(r0, NORM_ROWS), :] = (y * g).astype(dst_ref.dtype)
        return carry

    lax.fori_loop(0, rows // NORM_ROWS, body, 0)


def _cast_block(src_ref, dst_ref):
    dst_ref[...] = src_ref[...].astype(dst_ref.dtype)


def _causal_constants(t):
    row = lax.broadcasted_iota(jnp.int32, (t, t), 0)
    col = lax.broadcasted_iota(jnp.int32, (t, t), 1)
    tri = (row >= col).astype(BF16)
    tri_t = (row <= col).astype(BF16)
    neg_mask = jnp.where(row >= col, 0.0, -jnp.inf).astype(F32)
    return tri, tri_t, neg_mask


def _project_block(n, u_ref, wt_ref, proj_ref):
    tn = proj_ref.shape[1]

    def pieces(post):
        for j in range(tn // PROJ_PIECE):
            cols = slice(j * PROJ_PIECE, (j + 1) * PROJ_PIECE)
            res = _dot_nt(u_ref[...], wt_ref[cols, :])
            proj_ref[:, cols] = post(res).astype(proj_ref.dtype)

    z_blocks = SSD_WIDTH // tn
    o_block0 = IN_O_COL // tn

    @pl.when(n < z_blocks)
    def _():
        pieces(lambda res: res * _sigmoid(res))

    @pl.when((n >= z_blocks) & (n < o_block0))
    def _():
        pieces(lambda res: res)

    @pl.when(n >= o_block0)
    def _():
        pieces(_sigmoid)


def _in_proj_first_kernel(x_ref, g_ref, w_ref, wnext_ref, wdt_ref, wif_ref,
                          proj_ref, small_ref, wb_ref, ws_ref, u_ref):
    n = pl.program_id(0)
    gap = wnext_ref.shape[0]
    tn = wb_ref.shape[0]
    cast_rows = min(CAST_ROWS, tn)

    @pl.when(n == 0)
    def _():
        _rmsnorm_rows(x_ref, g_ref, u_ref, x_ref.shape[0])
        ws_ref[DT_LANE:I_LANE, :] = wdt_ref[...].astype(BF16)
        ws_ref[I_LANE:F_LANE + MLSTM_HEADS, :] = wif_ref[...].astype(BF16)
        ws_ref[F_LANE + MLSTM_HEADS:, :] = jnp.zeros(
            (SMALL_WIDTH - F_LANE - MLSTM_HEADS, ws_ref.shape[1]), BF16)
        small_ref[...] = _dot_nt(u_ref[...], ws_ref[...])

    @pl.when(n * tn < IN_WIDE_BEFORE_DT)
    def _():
        def body(i, carry):
            r0 = pl.multiple_of(i * cast_rows, cast_rows)
            wb_ref[pl.ds(r0, cast_rows), :] = w_ref[pl.ds(r0, cast_rows), :].astype(BF16)
            return carry
        lax.fori_loop(0, tn // cast_rows, body, 0)

    @pl.when(n * tn >= IN_WIDE_BEFORE_DT)
    def _():
        def body(i, carry):
            r0 = pl.multiple_of(i * gap, gap)
            wb_ref[pl.ds(r0, gap), :] = w_ref[pl.ds(r0 + gap, gap), :].astype(BF16)
            return carry
        lax.fori_loop(0, tn // gap - 1, body, 0)
        wb_ref[tn - gap:, :] = wnext_ref[...].astype(BF16)

    _project_block(n, u_ref, wb_ref, proj_ref)


def _in_proj_rest_kernel(x_ref, g_ref, w_ref, ws_ref, proj_in_ref, small_in_ref,
                         proj_ref, small_ref, u_ref):
    del proj_in_ref, small_in_ref
    n = pl.program_id(1)

    @pl.when(n == 0)
    def _():
        _rmsnorm_rows(x_ref, g_ref, u_ref, x_ref.shape[0])
        small_ref[...] = _dot_nt(u_ref[...], ws_ref[...])

    _project_block(n, u_ref, w_ref, proj_ref)


def _in_proj(x, g, w_in_t):
    seq, d = x.shape
    n_blocks = IN_WIDE // PROJ_TN
    gap = SSD_HEADS
    if_rows = 2 * MLSTM_HEADS
    tm1, tn1 = PROJ_FIRST_TM, PROJ_FIRST_TN
    first_tiles = tm1 // PROJ_TM
    assert first_tiles * PROJ_TM == tm1
    proj, small, w_wide, w_small = pl.pallas_call(
        _in_proj_first_kernel,
        grid=(IN_WIDE // tn1,),
        in_specs=[
            pl.BlockSpec((tm1, d), lambda n: (0, 0), pipeline_mode=pl.Buffered(1)),
            pl.BlockSpec((1, d), lambda n: (0, 0)),
            pl.BlockSpec((tn1, d), lambda n: (n, 0)),
            pl.BlockSpec((gap, d), lambda n: ((n + 1) * (tn1 // gap), 0)),
            pl.BlockSpec((gap, d), lambda n: (IN_WIDE_BEFORE_DT // gap, 0)),
            pl.BlockSpec((if_rows, d), lambda n: (IN_I_COL // if_rows, 0)),
        ],
        out_specs=[
            pl.BlockSpec((tm1, tn1), lambda n: (0, n)),
            pl.BlockSpec((tm1, SMALL_WIDTH), lambda n: (0, 0)),
            pl.BlockSpec((tn1, d), lambda n: (n, 0)),
            pl.BlockSpec((SMALL_WIDTH, d), lambda n: (0, 0)),
        ],
        out_shape=[
            jax.ShapeDtypeStruct((seq, IN_WIDE), BF16),
            jax.ShapeDtypeStruct((seq, SMALL_WIDTH), F32),
            jax.ShapeDtypeStruct((IN_WIDE, d), BF16),
            jax.ShapeDtypeStruct((SMALL_WIDTH, d), BF16),
        ],
        scratch_shapes=[pltpu.VMEM((tm1, d), BF16)],
        compiler_params=pltpu.CompilerParams(
            dimension_semantics=("arbitrary",),
            vmem_limit_bytes=VMEM_LIMIT_BYTES),
        name="in_proj_first",
    )(x, g, w_in_t, w_in_t, w_in_t, w_in_t)

    return pl.pallas_call(
        _in_proj_rest_kernel,
        grid=(seq // PROJ_TM - first_tiles, n_blocks),
        in_specs=[
            pl.BlockSpec((PROJ_TM, d), lambda m, n: (m + first_tiles, 0)),
            pl.BlockSpec((1, d), lambda m, n: (0, 0)),
            pl.BlockSpec((PROJ_TN, d), lambda m, n: (n, 0)),
            pl.BlockSpec((SMALL_WIDTH, d), lambda m, n: (0, 0)),
            pl.BlockSpec(memory_space=pl.ANY),
            pl.BlockSpec(memory_space=pl.ANY),
        ],
        out_specs=[
            pl.BlockSpec((PROJ_TM, PROJ_TN), lambda m, n: (m + first_tiles, n)),
            pl.BlockSpec((PROJ_TM, SMALL_WIDTH), lambda m, n: (m + first_tiles, 0)),
        ],
        out_shape=[
            jax.ShapeDtypeStruct((seq, IN_WIDE), BF16),
            jax.ShapeDtypeStruct((seq, SMALL_WIDTH), F32),
        ],
        scratch_shapes=[pltpu.VMEM((PROJ_TM, d), BF16)],
        input_output_aliases={4: 0, 5: 1},
        compiler_params=pltpu.CompilerParams(
            dimension_semantics=("arbitrary", "arbitrary"),
            vmem_limit_bytes=VMEM_LIMIT_BYTES),
        name="in_proj_rest",
    )(x, g, w_wide, w_small, proj, small)


def _ssd_kernel(zs_ref, xs_ref, bc_ref, small_ref, convw_ref, convb_ref, dtb_ref, alog_ref,
                dskip_ref, ng_ref, tri_ref, trit_ref, negmask_ref, expand_ref, shift_ref,
                lanemask_ref, wg_ref, wu_ref, wo_ref,
                y_ref, wg_bf_ref, wu_bf_ref, wo_bf_ref, xcat_ref, state_ref):
    T = SSD_CHUNK
    P = SSD_HEAD_DIM
    NS = SSD_STATE
    GW = SSD_GROUP_WIDTH
    TAIL = V7X_BF16_ROWS

    _cast_block(wg_ref, wg_bf_ref)
    _cast_block(wu_ref, wu_bf_ref)
    _cast_block(wo_ref, wo_bf_ref)

    @pl.when(pl.program_id(0) == 0)
    def _():
        xcat_ref[0:T, :] = jnp.zeros((T, SSD_CONV_WIDTH), BF16)
        state_ref[...] = jnp.zeros(state_ref.shape, F32)

    xcat_ref[T:2 * T, 0:SSD_WIDTH] = xs_ref[...]
    xcat_ref[T:2 * T, SSD_WIDTH:SSD_CONV_WIDTH] = bc_ref[...]

    def conv_silu(c0, width):
        shifted = _dot(shift_ref[...], xcat_ref[:, c0:c0 + width])
        conv = convb_ref[:, c0:c0 + width] + (
            xcat_ref[T:2 * T, c0:c0 + width].astype(F32) * convw_ref[SSD_CONV - 1:SSD_CONV, c0:c0 + width])
        for back in range(1, SSD_CONV):
            tap = SSD_CONV - 1 - back
            conv = conv + shifted[(back - 1) * T:back * T, :] * convw_ref[tap:tap + 1, c0:c0 + width]
        return conv * _sigmoid(conv)

    lane = lax.broadcasted_iota(jnp.int32, (T, SMALL_WIDTH), 1)
    dt = jnp.where(lane < SSD_HEADS, _softplus(small_ref[...] + dtb_ref[...]), 0.0)
    a_dt = dt * (-jnp.exp(alog_ref[...]))
    col_l = _dot01_left(tri_ref[...], a_dt) * LOG2E
    dt_t = dt.T
    row_l = (_dot01_right(a_dt.T, trit_ref[...]) - jnp.log(dt_t)) * LOG2E
    e_col = jnp.exp2(col_l)
    last_l = col_l[T - 1:T, :]
    w2 = dt * jnp.exp2(last_l - col_l)
    decay_wide = _dot01_right(
        jnp.broadcast_to(jnp.exp2(last_l), (V7X_SUBLANES, SMALL_WIDTH)), expand_ref[...])[0:1, :]

    neg_mask = negmask_ref[...]
    lane_t = lax.broadcasted_iota(jnp.int32, (T, V7X_LANES), 1)
    first_head = lane_t < P
    mask_a = lanemask_ref[0:1, :]
    mask_b = lanemask_ref[1:2, :]

    bc_act = conv_silu(SSD_WIDTH, 2 * SSD_BC_WIDTH)
    for g in range(SSD_GROUPS):
        gcols = slice(g * GW, (g + 1) * GW)
        xs_g = conv_silu(g * GW, GW)
        b_f = bc_act[:, g * NS:(g + 1) * NS]
        c_bf = bc_act[:, SSD_BC_WIDTH + g * NS:SSD_BC_WIDTH + (g + 1) * NS].astype(BF16)
        cb = _dot_nt(c_bf, b_f.astype(BF16))
        state_g = state_ref[g]
        y_off = _dot(c_bf, state_g.astype(BF16))

        y_pairs = []
        xd_pairs = []
        for j in range(GW // V7X_LANES):
            ha = g * (GW // P) + 2 * j
            hb = ha + 1
            m_a = (cb * jnp.exp2(col_l[:, ha:ha + 1] - row_l[ha:ha + 1, :] + neg_mask)).astype(BF16)
            m_b = (cb * jnp.exp2(col_l[:, hb:hb + 1] - row_l[hb:hb + 1, :] + neg_mask)).astype(BF16)
            xp = xs_g[:, j * V7X_LANES:(j + 1) * V7X_LANES]
            xp_bf = xp.astype(BF16)
            y_diag = _dot(jnp.concatenate([m_a, m_b], axis=1),
                          jnp.concatenate([xp_bf * mask_a, xp_bf * mask_b], axis=0))
            off_scale = jnp.where(first_head,
                                  jnp.broadcast_to(e_col[:, ha:ha + 1], (T, V7X_LANES)),
                                  jnp.broadcast_to(e_col[:, hb:hb + 1], (T, V7X_LANES)))
            w2_pair = jnp.where(first_head,
                                jnp.broadcast_to(w2[:, ha:ha + 1], (T, V7X_LANES)),
                                jnp.broadcast_to(w2[:, hb:hb + 1], (T, V7X_LANES)))
            y_pairs.append(y_diag + off_scale * y_off[:, j * V7X_LANES:(j + 1) * V7X_LANES])
            xd_pairs.append((xp * w2_pair).astype(BF16))

        y_g = jnp.concatenate(y_pairs, axis=1) + dskip_ref[:, gcols] * xs_g
        xd_g = jnp.concatenate(xd_pairs, axis=1)
        states_g = _dot(b_f.T.astype(BF16), xd_g)
        state_ref[g] = state_g * decay_wide[:, gcols] + states_g

        y_g = y_g * zs_ref[:, gcols].astype(F32)
        y_g = y_g * lax.rsqrt(jnp.mean(y_g * y_g, axis=-1, keepdims=True) + EPS)
        y_ref[:, gcols] = (y_g * ng_ref[:, gcols]).astype(y_ref.dtype)

    xcat_ref[T - TAIL:T, :] = xcat_ref[2 * T - TAIL:2 * T, :]


def _ssd(proj, small, conv_w, conv_b, dt_bias_row, a_log_row, d_skip_row, norm_g_row,
         w_gate, w_up, w_out):
    seq = proj.shape[0]
    T = SSD_CHUNK
    steps = seq // T
    d, d_ff = w_gate.shape
    slab = d // steps
    slab_o = w_out.shape[0] // steps
    assert slab * steps == d and slab % V7X_BF16_ROWS == 0
    assert slab_o * steps == w_out.shape[0] and slab_o % V7X_BF16_ROWS == 0
    tri, tri_t, neg_mask = _causal_constants(T)
    e_row = lax.broadcasted_iota(jnp.int32, (SMALL_WIDTH, SSD_WIDTH), 0)
    e_col = lax.broadcasted_iota(jnp.int32, (SMALL_WIDTH, SSD_WIDTH), 1)
    expand = (e_row == e_col // SSD_HEAD_DIM).astype(BF16)
    s_row = lax.broadcasted_iota(jnp.int32, ((SSD_CONV - 1) * T, 2 * T), 0)
    s_col = lax.broadcasted_iota(jnp.int32, ((SSD_CONV - 1) * T, 2 * T), 1)
    shift = (s_col == T + s_row % T - (s_row // T + 1)).astype(BF16)
    m_lane = lax.broadcasted_iota(jnp.int32, (2, V7X_LANES), 1)
    m_row = lax.broadcasted_iota(jnp.int32, (2, V7X_LANES), 0)
    lane_mask = ((m_lane < SSD_HEAD_DIM) == (m_row == 0)).astype(BF16)

    full = lambda a: pl.BlockSpec(a.shape, lambda c: (0, 0))
    consts = (conv_w, conv_b, dt_bias_row, a_log_row, d_skip_row, norm_g_row,
              tri, tri_t, neg_mask, expand, shift, lane_mask)
    return pl.pallas_call(
        _ssd_kernel,
        grid=(steps,),
        in_specs=[
            pl.BlockSpec((T, SSD_WIDTH), lambda c: (c, 0)),
            pl.BlockSpec((T, SSD_WIDTH), lambda c: (c, 1)),
            pl.BlockSpec((T, 2 * SSD_BC_WIDTH),
                         lambda c: (c, 2 * SSD_WIDTH // (2 * SSD_BC_WIDTH))),
            pl.BlockSpec((T, SMALL_WIDTH), lambda c: (c, 0)),
            *[full(a) for a in consts],
            pl.BlockSpec((slab, d_ff), lambda c: (c, 0)),
            pl.BlockSpec((slab, d_ff), lambda c: (c, 0)),
            pl.BlockSpec((slab_o, d), lambda c: (c, 0)),
        ],
        out_specs=[
            pl.BlockSpec((T, SSD_WIDTH), lambda c: (c, 0)),
            pl.BlockSpec((slab, d_ff), lambda c: (c, 0)),
            pl.BlockSpec((slab, d_ff), lambda c: (c, 0)),
            pl.BlockSpec((slab_o, d), lambda c: (c, 0)),
        ],
        out_shape=[
            jax.ShapeDtypeStruct((seq, SSD_WIDTH), BF16),
            jax.ShapeDtypeStruct((d, d_ff), BF16),
            jax.ShapeDtypeStruct((d, d_ff), BF16),
            jax.ShapeDtypeStruct(w_out.shape, BF16),
        ],
        scratch_shapes=[
            pltpu.VMEM((2 * T, SSD_CONV_WIDTH), BF16),
            pltpu.VMEM((SSD_GROUPS, SSD_STATE, SSD_GROUP_WIDTH), F32),
        ],
        compiler_params=pltpu.CompilerParams(
            dimension_semantics=("arbitrary",),
            vmem_limit_bytes=VMEM_LIMIT_BYTES),
        name="ssd_mixer",
    )(proj, proj, proj, small, *consts, w_gate, w_up, w_out)


def _cummax_rows(x):
    rows = x.shape[0]
    row = lax.broadcasted_iota(jnp.int32, x.shape, 0)
    d = 1
    while d < rows:
        if d < V7X_SUBLANES:
            shifted = jnp.where(row < d, -jnp.inf, pltpu.roll(x, d, axis=0))
        else:
            shifted = jnp.concatenate(
                [jnp.full((d, x.shape[1]), -jnp.inf, x.dtype), x[:rows - d]], axis=0)
        x = jnp.maximum(x, shifted)
        d *= 2
    return x


def _mlstm_chunk(q_ref, k_ref, v0_ref, v1_ref, o0_ref, o1_ref, small_ref, ib_ref, fb_ref,
                 ng_ref, tri_ref, negmask_ref, ct_ref, nmat_ref, m_ref, h_ref, h_row0):
    T = MLSTM_SUB
    DK = MLSTM_QK_DIM
    DV = MLSTM_V_DIM
    log2_scale = math.log2(DK ** -0.5)
    half = MLSTM_HEADS // 2

    neg_mask = negmask_ref[...]
    lane = lax.broadcasted_iota(jnp.int32, (T, SMALL_WIDTH), 1)
    head_lane = (lane >= F_LANE) & (lane < F_LANE + MLSTM_HEADS)
    lane_dk = lax.broadcasted_iota(jnp.int32, (DK, SMALL_WIDTH), 1)

    for sub in range(MLSTM_CHUNK // T):
        r0 = sub * T
        sm = small_ref[r0:r0 + T, :]
        log_i = pltpu.roll(_softcap(sm + ib_ref[...]) * LOG2E, F_LANE - I_LANE, axis=1)
        log_i = jnp.where(head_lane, log_i, 0.0)
        log_f = jnp.where(head_lane, _log_sigmoid(_softcap(sm + fb_ref[...])) * LOG2E, 0.0)
        bcum = _dot01_left(tri_ref[...], log_f)
        r_c = log_i - bcum
        m_prev = m_ref[0:1, :]
        inter_log = bcum + m_prev
        m_t = jnp.maximum(inter_log, bcum + _cummax_rows(r_c))
        col_term = bcum - m_t
        w_inter = jnp.exp2(inter_log - m_t)
        e_neg_m = jnp.exp2(-m_t)
        b_last = bcum[T - 1:T, :]
        a_c = b_last + r_c
        m_loc = jnp.max(a_c, axis=0, keepdims=True)
        m_new = jnp.maximum(b_last + m_prev, m_loc)
        s_old = jnp.exp2(b_last + m_prev - m_new)
        s_new = jnp.exp2(m_loc - m_new)
        w_rows = (jnp.exp2(a_c - m_loc + log2_scale) * s_new).T
        row_term = (r_c + log2_scale).T
        q_all = q_ref[r0:r0 + T, :]
        nq_inter = _dot(q_all, nmat_ref[...].astype(BF16))
        yield

        for h in range(MLSTM_HEADS):
            v_ref, o_ref = (v0_ref, o0_ref) if h < half else (v1_ref, o1_ref)
            hv = h % half
            hl = F_LANE + h
            q = q_all[:, h * DK:(h + 1) * DK]
            k = k_ref[r0:r0 + T, h * DK:(h + 1) * DK]
            v = v_ref[r0:r0 + T, hv * DV:(hv + 1) * DV]
            ct = ct_ref[h]

            kw = k.astype(F32).T * w_rows[hl:hl + 1, :]
            c_loc = _dot(kw.astype(BF16), v)
            n_loc = jnp.sum(kw, axis=1, keepdims=True)

            p = _dot_nt(q, k) * jnp.exp2(
                col_term[:, hl:hl + 1] + row_term[hl:hl + 1, :] + neg_mask)
            wi = w_inter[:, hl:hl + 1]
            num = _dot(p.astype(BF16), v) + wi * _dot(q, ct.astype(BF16))
            nq = jnp.sum(p, axis=1, keepdims=True) + wi * nq_inter[:, hl:hl + 1]
            inv_den = 1.0 / jnp.maximum(jnp.abs(nq), e_neg_m[:, hl:hl + 1])
            ms = jnp.mean(num * num, axis=-1, keepdims=True)
            out_scale = inv_den * lax.rsqrt(inv_den * inv_den * ms + EPS)
            gate = o_ref[r0:r0 + T, hv * DV:(hv + 1) * DV].astype(F32) * ng_ref[:, h * DV:(h + 1) * DV]
            h_ref[pl.ds(h_row0 + r0, T), h * DV:(h + 1) * DV] = (
                num * out_scale * gate).astype(h_ref.dtype)

            so = s_old[:, hl:hl + 1]
            ct_ref[h] = so * ct + c_loc
            n_rows = nmat_ref[h * DK:(h + 1) * DK, :]
            nmat_ref[h * DK:(h + 1) * DK, :] = so * n_rows + jnp.where(lane_dk == hl, n_loc, 0.0)
            yield

        m_ref[...] = jnp.broadcast_to(m_new, m_ref.shape)


def _mlstm_out_kernel(x_ref, ys_ref, wt_ref, wb_ref,
                      q_ref, k_ref, v0_ref, v1_ref, o0_ref, o1_ref, small_ref, ib_ref, fb_ref,
                      ng_ref, tri_ref, negmask_ref, wd_ref,
                      h1_ref, wd_bf_ref,
                      yml_even_ref, yml_odd_ref, ct_ref, nmat_ref, m_ref):
    m = pl.program_id(0)
    n = pl.program_id(1)
    last = pl.num_programs(0) - 1

    _cast_block(wd_ref, wd_bf_ref)

    @pl.when((m == 0) & (n == 0))
    def _():
        ct_ref[...] = jnp.zeros(ct_ref.shape, F32)
        nmat_ref[...] = jnp.zeros(nmat_ref.shape, F32)
        m_ref[...] = jnp.zeros(m_ref.shape, F32)

    def project(yml_ref):
        col_w = OUT_TN // 2
        k_w = 2 * SSD_WIDTH // (OUT_PIECES // 2)
        for piece in range(OUT_PIECES):
            cols = slice((piece % 2) * col_w, (piece % 2 + 1) * col_w)
            k0 = (piece // 2) * k_w
            src, w_ref = (ys_ref, wt_ref) if k0 < SSD_WIDTH else (yml_ref, wb_ref)
            k0 %= SSD_WIDTH
            part = _dot(src[:, k0:k0 + k_w], w_ref[k0:k0 + k_w, cols])
            if piece < 2:
                h1_ref[:, cols] = x_ref[:, cols] + part
            else:
                h1_ref[:, cols] += part
            yield

    def mix(yml_ref):
        return _mlstm_chunk(q_ref, k_ref, v0_ref, v1_ref, o0_ref, o1_ref, small_ref, ib_ref,
                            fb_ref, ng_ref, tri_ref, negmask_ref, ct_ref, nmat_ref, m_ref,
                            yml_ref, pl.multiple_of(n * MLSTM_CHUNK, MLSTM_CHUNK))

    def run_both(read_ref, write_ref):
        seg_total = (MLSTM_CHUNK // MLSTM_SUB) * (MLSTM_HEADS + 1)
        segs = mix(write_ref)
        done = 0
        for i, _ in enumerate(project(read_ref)):
            want = -(-(i + 1) * seg_total // OUT_PIECES)
            while done < want and next(segs, "end") != "end":
                done += 1
        for _ in segs:
            pass

    @pl.when(m == 0)
    def _():
        for _ in mix(yml_even_ref):
            pass

    @pl.when((m > 0) & (m < last) & (m % 2 == 1))
    def _():
        run_both(yml_even_ref, yml_odd_ref)

    @pl.when((m > 0) & (m < last) & (m % 2 == 0))
    def _():
        run_both(yml_odd_ref, yml_even_ref)

    @pl.when(m == last)
    def _():
        for _ in project(yml_odd_ref):
            pass


def _mlstm_out(x, y_ssd, w_out_bf, proj, small, ib_row, fb_row, norm_g_row, w_down):
    seq, d = x.shape
    T = MLSTM_CHUNK
    tiles = seq // OUT_TM
    n_tiles = d // OUT_TN
    assert tiles % 2 == 0 and OUT_TM == n_tiles * T
    steps = (tiles + 1) * n_tiles
    d_ff = w_down.shape[0]
    slab_d = -(-d_ff // steps)
    slab_d += -slab_d % V7X_BF16_ROWS
    half_w = MLSTM_WIDTH // 2
    q_blk = IN_WIDE_BEFORE_DT // MLSTM_QK_WIDTH
    v_blk = (IN_WIDE_BEFORE_DT + 2 * MLSTM_QK_WIDTH) // half_w
    tri, _, neg_mask = _causal_constants(MLSTM_SUB)
    consts = (ib_row, fb_row, norm_g_row, tri, neg_mask)
    full = lambda a: pl.BlockSpec(a.shape, lambda m, n: (0, 0))
    prev = lambda m: jnp.maximum(m - 1, 0)
    chunk = lambda m, n: jnp.minimum(m * n_tiles + n, seq // T - 1)
    n_slabs = -(-d_ff // slab_d)
    slab = lambda m, n: (jnp.minimum(m * n_tiles + n, n_slabs - 1), 0)
    h1, wd_bf = pl.pallas_call(
        _mlstm_out_kernel,
        grid=(tiles + 1, n_tiles),
        in_specs=[
            pl.BlockSpec((OUT_TM, OUT_TN), lambda m, n: (prev(m), n)),
            pl.BlockSpec((OUT_TM, SSD_WIDTH), lambda m, n: (prev(m), 0)),
            pl.BlockSpec((SSD_WIDTH, OUT_TN), lambda m, n: (0, n)),
            pl.BlockSpec((MLSTM_WIDTH, OUT_TN), lambda m, n: (1, n)),
            pl.BlockSpec((T, MLSTM_QK_WIDTH), lambda m, n: (chunk(m, n), q_blk)),
            pl.BlockSpec((T, MLSTM_QK_WIDTH), lambda m, n: (chunk(m, n), q_blk + 1)),
            pl.BlockSpec((T, half_w), lambda m, n: (chunk(m, n), v_blk)),
            pl.BlockSpec((T, half_w), lambda m, n: (chunk(m, n), v_blk + 1)),
            pl.BlockSpec((T, half_w), lambda m, n: (chunk(m, n), v_blk + 2)),
            pl.BlockSpec((T, half_w), lambda m, n: (chunk(m, n), v_blk + 3)),
            pl.BlockSpec((T, SMALL_WIDTH), lambda m, n: (chunk(m, n), 0)),
            *[full(a) for a in consts],
            pl.BlockSpec((slab_d, d), slab),
        ],
        out_specs=[
            pl.BlockSpec((OUT_TM, OUT_TN), lambda m, n: (prev(m), jnp.where(m == 0, 0, n))),
            pl.BlockSpec((slab_d, d), slab),
        ],
        out_shape=[
            jax.ShapeDtypeStruct((seq, d), F32),
            jax.ShapeDtypeStruct((d_ff, d), BF16),
        ],
        scratch_shapes=[
            pltpu.VMEM((OUT_TM, MLSTM_WIDTH), BF16),
            pltpu.VMEM((OUT_TM, MLSTM_WIDTH), BF16),
            pltpu.VMEM((MLSTM_HEADS, MLSTM_QK_DIM, MLSTM_V_DIM), F32),
            pltpu.VMEM((MLSTM_QK_WIDTH, SMALL_WIDTH), F32),
            pltpu.VMEM((V7X_SUBLANES, SMALL_WIDTH), F32),
        ],
        compiler_params=pltpu.CompilerParams(
            dimension_semantics=("arbitrary", "arbitrary"),
            vmem_limit_bytes=VMEM_LIMIT_BYTES),
        name="mlstm_out",
    )(x, y_ssd, w_out_bf, w_out_bf, proj, proj, proj, proj, proj, proj, small, *consts, w_down)
    return h1, wd_bf


def _ffn_kernel(h_ref, g_ref, wg_ref, wu_ref, wd_ref, gf_ref, out_ref, u_ref):
    f = pl.program_id(1)

    @pl.when(f == 0)
    def _():
        _rmsnorm_rows(h_ref, g_ref, u_ref, h_ref.shape[0])
        out_ref[...] = h_ref[...]

    u = u_ref[...]
    sub = wg_ref.shape[1] // FFN_SPLIT
    acc = None
    for s in range(FFN_SPLIT):
        gate = _dot(u, wg_ref[:, s * sub:(s + 1) * sub])
        up = _dot(u, wu_ref[:, s * sub:(s + 1) * sub])
        a = (gate * _sigmoid(gate) * up).astype(BF16)
        part = _dot(a, wd_ref[s * sub:(s + 1) * sub, :])
        acc = part if acc is None else acc + part
    out_ref[...] += acc

    @pl.when(f == pl.num_programs(1) - 1)
    def _():
        _rmsnorm_rows(out_ref, gf_ref, out_ref, out_ref.shape[0])


def _ffn(h1, g, w_gate, w_up, w_down, g_final):
    seq, d = h1.shape
    d_ff = w_gate.shape[1]
    grid = (seq // FFN_TM, d_ff // FFN_TF)
    return pl.pallas_call(
        _ffn_kernel,
        grid=grid,
        in_specs=[
            pl.BlockSpec((FFN_TM, d), lambda m, f: (m, 0)),
            pl.BlockSpec((1, d), lambda m, f: (0, 0)),
            pl.BlockSpec((d, FFN_TF), lambda m, f: (0, f)),
            pl.BlockSpec((d, FFN_TF), lambda m, f: (0, f)),
            pl.BlockSpec((FFN_TF, d), lambda m, f: (f, 0)),
            pl.BlockSpec((1, d), lambda m, f: (0, 0)),
        ],
        out_specs=pl.BlockSpec((FFN_TM, d), lambda m, f: (m, 0)),
        out_shape=jax.ShapeDtypeStruct((seq, d), F32),
        scratch_shapes=[pltpu.VMEM((FFN_TM, d), BF16)],
        compiler_params=pltpu.CompilerParams(
            dimension_semantics=("arbitrary", "arbitrary"),
            vmem_limit_bytes=VMEM_LIMIT_BYTES),
        name="ffn",
    )(h1, g, w_gate, w_up, w_down, g_final)


def _pad_lanes(vec, lane0):
    return jnp.zeros((1, SMALL_WIDTH), F32).at[0, lane0:lane0 + vec.shape[0]].set(vec.astype(F32))


def _layer(h, norm_mix_g, w_in, conv_w, conv_b, dt_bias, a_log, d_skip, ssd_norm_g,
           i_bias, f_bias, mlstm_norm_g, w_out, norm_ffn_g, w_gate, w_up, w_down, out_g):
    d = h.shape[1]
    proj, small = _in_proj(h, norm_mix_g.reshape(1, d), w_in.T)

    y_ssd, wg_bf, wu_bf, w_out_bf = _ssd(
        proj, small, conv_w.astype(F32), conv_b.reshape(1, -1).astype(F32),
        _pad_lanes(dt_bias, DT_LANE), _pad_lanes(a_log, DT_LANE),
        jnp.repeat(d_skip.astype(F32), SSD_HEAD_DIM).reshape(1, -1),
        ssd_norm_g.reshape(1, -1).astype(F32), w_gate, w_up, w_out)

    h1, wd_bf = _mlstm_out(
        h, y_ssd, w_out_bf, proj, small, _pad_lanes(i_bias, I_LANE), _pad_lanes(f_bias, F_LANE),
        mlstm_norm_g.reshape(1, -1).astype(F32), w_down)
    return _ffn(h1, norm_ffn_g.reshape(1, d), wg_bf, wu_bf, wd_bf, out_g.reshape(1, d))


def kernel(x, norm_mix_g, w_in, conv_w, conv_b, dt_bias, a_log, d_skip, ssd_norm_g, i_bias,
           f_bias, mlstm_norm_g, w_out, norm_ffn_g, w_gate, w_up, w_down, final_norm_g):
    batch, seq, d = x.shape
    depth = w_in.shape[0]
    assert batch == 1 and depth == 1, "single-sequence, single-layer problem"
    out = _layer(x.reshape(seq, d), norm_mix_g[0], w_in[0], conv_w[0], conv_b[0], dt_bias[0],
                 a_log[0], d_skip[0], ssd_norm_g[0], i_bias[0], f_bias[0], mlstm_norm_g[0],
                 w_out[0], norm_ffn_g[0], w_gate[0], w_up[0], w_down[0], final_norm_g)
    return out.reshape(batch, seq, d)
```

```python
import functools
import math

import jax
import jax.numpy as jnp
from jax import lax
from jax.experimental import pallas as pl
from jax.experimental.pallas import tpu as pltpu

F32 = jnp.float32
BF16 = jnp.bfloat16
EPS = 1e-6
LOG2E = math.log2(math.e)

V7X_LANES = 128
V7X_SUBLANES = 8
V7X_BF16_ROWS = 16
VMEM_LIMIT_BYTES = 56 * 1024 * 1024

SSD_HEADS = 32
SSD_HEAD_DIM = 64
SSD_WIDTH = SSD_HEADS * SSD_HEAD_DIM
SSD_GROUPS = 4
SSD_STATE = 128
SSD_CONV = 4
SSD_GROUP_WIDTH = SSD_WIDTH // SSD_GROUPS
SSD_BC_WIDTH = SSD_GROUPS * SSD_STATE
SSD_CONV_WIDTH = SSD_WIDTH + 2 * SSD_BC_WIDTH
MLSTM_HEADS = 8
MLSTM_QK_DIM = 128
MLSTM_V_DIM = 256
MLSTM_WIDTH = MLSTM_HEADS * MLSTM_V_DIM
MLSTM_QK_WIDTH = MLSTM_HEADS * MLSTM_QK_DIM
GATE_SOFTCAP = 15.0

IN_WIDE_BEFORE_DT = SSD_WIDTH + SSD_CONV_WIDTH
IN_WIDE = IN_WIDE_BEFORE_DT + 2 * MLSTM_QK_WIDTH + 2 * MLSTM_WIDTH
IN_I_COL = IN_WIDE + SSD_HEADS
DT_LANE = 0
I_LANE = SSD_HEADS
F_LANE = SSD_HEADS + MLSTM_HEADS
SMALL_WIDTH = V7X_LANES

SSD_CHUNK = 128
SSD_PER_STEP = 2
MLSTM_CHUNK = 256
MLSTM_SUB = 128
PROJ_TM = 1024
PROJ_TN = 1024
PROJ_FIRST_TN = 512
PROJ_PIECE = 256
OUT_TM = 1024
OUT_TN = 512
OUT_PIECES = 16
FFN_TM = 512
FFN_TF = 512
FFN_SPLIT = 2
NORM_ROWS = 128
CAST_ROWS = 256

SSD_IN_BLOCKS = IN_WIDE_BEFORE_DT // PROJ_TN
SSD_STEPS = PROJ_TM // (SSD_PER_STEP * SSD_CHUNK)


def _dot(a, b):
    return jnp.dot(a, b, preferred_element_type=F32)


def _dot_nt(a, b):
    return lax.dot_general(a, b, (((1,), (1,)), ((), ())), preferred_element_type=F32)


def _split3(x):
    hi = x.astype(BF16)
    r1 = x - hi.astype(F32)
    mid = r1.astype(BF16)
    lo = (r1 - mid.astype(F32)).astype(BF16)
    return hi, mid, lo


def _dot01_left(m01, x):
    hi, mid, lo = _split3(x)
    return _dot(m01, hi) + _dot(m01, mid) + _dot(m01, lo)


def _dot01_right(x, m01):
    hi, mid, lo = _split3(x)
    return _dot(hi, m01) + _dot(mid, m01) + _dot(lo, m01)


def _sigmoid(x):
    return 0.5 * jnp.tanh(0.5 * x) + 0.5


def _softplus(x):
    return jnp.maximum(x, 0.0) + jnp.log1p(jnp.exp(-jnp.abs(x)))


def _log_sigmoid(x):
    return jnp.minimum(x, 0.0) - jnp.log1p(jnp.exp(-jnp.abs(x)))


def _softcap(x):
    return GATE_SOFTCAP * jnp.tanh(x / GATE_SOFTCAP)


def _rmsnorm_rows(src_ref, g_ref, dst_ref, rows):
    g = g_ref[...]

    def body(i, carry):
        r0 = pl.multiple_of(i * NORM_ROWS, NORM_ROWS)
        x = src_ref[pl.ds(r0, NORM_ROWS), :].astype(F32)
        y = x * lax.rsqrt(jnp.mean(x * x, axis=-1, keepdims=True) + EPS)
        dst_ref[pl.ds(r0, NORM_ROWS), :] = (y * g).astype(dst_ref.dtype)
        return carry

    lax.fori_loop(0, rows // NORM_ROWS, body, 0)


def _cast_block(src_ref, dst_ref):
    dst_ref[...] = src_ref[...].astype(dst_ref.dtype)


def _causal_constants(t):
    row = lax.broadcasted_iota(jnp.int32, (t, t), 0)
    col = lax.broadcasted_iota(jnp.int32, (t, t), 1)
    tri = (row >= col).astype(BF16)
    tri_t = (row <= col).astype(BF16)
    neg_mask = jnp.where(row >= col, 0.0, -jnp.inf).astype(F32)
    return tri, tri_t, neg_mask


def _run_interleaved(*streams):
    done = [0] * len(streams)
    total = max(count for _, count in streams)
    for tick in range(1, total + 1):
        for i, (gen, count) in enumerate(streams):
            want = -(-tick * count // total)
            while done[i] < want:
                next(gen, None)
                done[i] += 1
    for gen, _ in streams:
        for _ in gen:
            pass


def _slab_spec(n_rows, n_cols, steps, step_of):
    slab = -(-n_rows // steps)
    slab += -slab % V7X_BF16_ROWS
    last = -(-n_rows // slab) - 1
    return pl.BlockSpec((slab, n_cols), lambda *idx: (jnp.minimum(step_of(*idx), last), 0))


def _ssd_chunk(save_ref, small_ref, convw_ref, convb_ref, dtb_ref, alog_ref, dskip_ref, ng_ref,
               tri_ref, trit_ref, negmask_ref, expand_ref, shift_ref, lanemask_ref,
               xcat_ref, state_ref, row0, y_ref, y_row0):
    T = SSD_CHUNK
    P = SSD_HEAD_DIM
    NS = SSD_STATE
    GW = SSD_GROUP_WIDTH
    TAIL = V7X_BF16_ROWS
    rows = pl.ds(row0, T)
    tn = save_ref.shape[2]
    z_block0 = 0
    xbc_block0 = SSD_WIDTH // tn

    for b in range(SSD_CONV_WIDTH // tn):
        xcat_ref[T:2 * T, b * tn:(b + 1) * tn] = save_ref[xbc_block0 + b, rows, :]

    def conv_silu(c0, width):
        shifted = _dot(shift_ref[...], xcat_ref[:, c0:c0 + width])
        conv = convb_ref[:, c0:c0 + width] + (
            xcat_ref[T:2 * T, c0:c0 + width].astype(F32) * convw_ref[SSD_CONV - 1:SSD_CONV, c0:c0 + width])
        for back in range(1, SSD_CONV):
            tap = SSD_CONV - 1 - back
            conv = conv + shifted[(back - 1) * T:back * T, :] * convw_ref[tap:tap + 1, c0:c0 + width]
        return conv * _sigmoid(conv)

    lane = lax.broadcasted_iota(jnp.int32, (T, SMALL_WIDTH), 1)
    dt = jnp.where(lane < SSD_HEADS, _softplus(small_ref[rows, :] + dtb_ref[...]), 0.0)
    a_dt = dt * (-jnp.exp(alog_ref[...]))
    col_l = _dot01_left(tri_ref[...], a_dt) * LOG2E
    dt_t = dt.T
    row_l = (_dot01_right(a_dt.T, trit_ref[...]) - jnp.log(dt_t)) * LOG2E
    e_col = jnp.exp2(col_l)
    last_l = col_l[T - 1:T, :]
    w2 = dt * jnp.exp2(last_l - col_l)
    decay_wide = _dot01_right(
        jnp.broadcast_to(jnp.exp2(last_l), (V7X_SUBLANES, SMALL_WIDTH)), expand_ref[...])[0:1, :]

    neg_mask = negmask_ref[...]
    lane_t = lax.broadcasted_iota(jnp.int32, (T, V7X_LANES), 1)
    first_head = lane_t < P
    mask_a = lanemask_ref[0:1, :]
    mask_b = lanemask_ref[1:2, :]

    bc_act = conv_silu(SSD_WIDTH, 2 * SSD_BC_WIDTH)
    yield
    for g in range(SSD_GROUPS):
        gcols = slice(g * GW, (g + 1) * GW)
        xs_g = conv_silu(g * GW, GW)
        b_f = bc_act[:, g * NS:(g + 1) * NS]
        c_bf = bc_act[:, SSD_BC_WIDTH + g * NS:SSD_BC_WIDTH + (g + 1) * NS].astype(BF16)
        cb = _dot_nt(c_bf, b_f.astype(BF16))
        state_g = state_ref[g]
        y_off = _dot(c_bf, state_g.astype(BF16))

        y_pairs = []
        xd_pairs = []
        for j in range(GW // V7X_LANES):
            ha = g * (GW // P) + 2 * j
            hb = ha + 1
            m_a = (cb * jnp.exp2(col_l[:, ha:ha + 1] - row_l[ha:ha + 1, :] + neg_mask)).astype(BF16)
            m_b = (cb * jnp.exp2(col_l[:, hb:hb + 1] - row_l[hb:hb + 1, :] + neg_mask)).astype(BF16)
            xp = xs_g[:, j * V7X_LANES:(j + 1) * V7X_LANES]
            xp_bf = xp.astype(BF16)
            y_diag = _dot(jnp.concatenate([m_a, m_b], axis=1),
                          jnp.concatenate([xp_bf * mask_a, xp_bf * mask_b], axis=0))
            off_scale = jnp.where(first_head,
                                  jnp.broadcast_to(e_col[:, ha:ha + 1], (T, V7X_LANES)),
                                  jnp.broadcast_to(e_col[:, hb:hb + 1], (T, V7X_LANES)))
            w2_pair = jnp.where(first_head,
                                jnp.broadcast_to(w2[:, ha:ha + 1], (T, V7X_LANES)),
                                jnp.broadcast_to(w2[:, hb:hb + 1], (T, V7X_LANES)))
            y_pairs.append(y_diag + off_scale * y_off[:, j * V7X_LANES:(j + 1) * V7X_LANES])
            xd_pairs.append((xp * w2_pair).astype(BF16))

        y_g = jnp.concatenate(y_pairs, axis=1) + dskip_ref[:, gcols] * xs_g
        xd_g = jnp.concatenate(xd_pairs, axis=1)
        states_g = _dot(b_f.T.astype(BF16), xd_g)
        state_ref[g] = state_g * decay_wide[:, gcols] + states_g

        zb, zc = divmod(g * GW, tn)
        zg = save_ref[z_block0 + zb, rows, zc:zc + GW].astype(F32)
        y_g = y_g * (zg * _sigmoid(zg))
        y_g = y_g * lax.rsqrt(jnp.mean(y_g * y_g, axis=-1, keepdims=True) + EPS)
        y_ref[pl.ds(y_row0, T), gcols] = (y_g * ng_ref[:, gcols]).astype(y_ref.dtype)
        if g == SSD_GROUPS - 1:
            xcat_ref[T - TAIL:T, :] = xcat_ref[2 * T - TAIL:2 * T, :]
        yield


def _ssd_constants():
    T = SSD_CHUNK
    tri, tri_t, neg_mask = _causal_constants(T)
    e_row = lax.broadcasted_iota(jnp.int32, (SMALL_WIDTH, SSD_WIDTH), 0)
    e_col = lax.broadcasted_iota(jnp.int32, (SMALL_WIDTH, SSD_WIDTH), 1)
    expand = (e_row == e_col // SSD_HEAD_DIM).astype(BF16)
    s_row = lax.broadcasted_iota(jnp.int32, ((SSD_CONV - 1) * T, 2 * T), 0)
    s_col = lax.broadcasted_iota(jnp.int32, ((SSD_CONV - 1) * T, 2 * T), 1)
    shift = (s_col == T + s_row % T - (s_row // T + 1)).astype(BF16)
    m_lane = lax.broadcasted_iota(jnp.int32, (2, V7X_LANES), 1)
    m_row = lax.broadcasted_iota(jnp.int32, (2, V7X_LANES), 0)
    lane_mask = ((m_lane < SSD_HEAD_DIM) == (m_row == 0)).astype(BF16)
    return tri, tri_t, neg_mask, expand, shift, lane_mask


def _project_and_mix(n, u_ref, wt_ref, proj_ref, small_ref, ssd_refs, save_ref, xcat_ref,
                     state_ref, y_ref):
    tn = proj_ref.shape[1]
    in_blocks = IN_WIDE_BEFORE_DT // tn

    def project(keep):
        for j in range(tn // PROJ_PIECE):
            cols = slice(j * PROJ_PIECE, (j + 1) * PROJ_PIECE)
            val = _dot_nt(u_ref[...], wt_ref[cols, :]).astype(proj_ref.dtype)
            proj_ref[:, cols] = val
            if keep:
                save_ref[n, :, cols] = val
            yield

    def mix():
        step_row0 = (n - in_blocks) * (SSD_PER_STEP * SSD_CHUNK)
        for c in range(SSD_PER_STEP):
            row0 = pl.multiple_of(step_row0 + c * SSD_CHUNK, SSD_CHUNK)
            yield from _ssd_chunk(save_ref, small_ref, *ssd_refs, xcat_ref, state_ref,
                                  row0, y_ref, c * SSD_CHUNK)

    @pl.when(n < in_blocks)
    def _():
        for _ in project(True):
            pass

    @pl.when((n >= in_blocks) & (n < in_blocks + SSD_STEPS))
    def _():
        _run_interleaved((project(False), tn // PROJ_PIECE),
                         (mix(), SSD_PER_STEP * (SSD_GROUPS + 1)))

    @pl.when(n >= in_blocks + SSD_STEPS)
    def _():
        for _ in project(False):
            pass


def _in_proj_first_kernel(x_ref, g_ref, w_ref, wnext_ref, wdt_ref, wif_ref, *rest):
    ssd_refs = rest[:12]
    (proj_ref, small_ref, y_ref, wb_ref, ws_ref, state_out_ref, tail_out_ref,
     u_ref, save_ref, xcat_ref, state_ref) = rest[12:]
    n = pl.program_id(0)
    gap = wnext_ref.shape[0]
    T = SSD_CHUNK

    @pl.when(n == 0)
    def _():
        _rmsnorm_rows(x_ref, g_ref, u_ref, x_ref.shape[0])
        ws_ref[DT_LANE:I_LANE, :] = wdt_ref[...].astype(BF16)
        ws_ref[I_LANE:F_LANE + MLSTM_HEADS, :] = wif_ref[...].astype(BF16)
        ws_ref[F_LANE + MLSTM_HEADS:, :] = jnp.zeros(
            (SMALL_WIDTH - F_LANE - MLSTM_HEADS, ws_ref.shape[1]), BF16)
        small_ref[...] = _dot_nt(u_ref[...], ws_ref[...])
        xcat_ref[0:T, :] = jnp.zeros((T, SSD_CONV_WIDTH), BF16)
        state_ref[...] = jnp.zeros(state_ref.shape, F32)

    tn = wb_ref.shape[0]
    cast_rows = min(CAST_ROWS, tn)

    @pl.when(n * tn < IN_WIDE_BEFORE_DT)
    def _():
        def body(i, carry):
            r0 = pl.multiple_of(i * cast_rows, cast_rows)
            wb_ref[pl.ds(r0, cast_rows), :] = w_ref[pl.ds(r0, cast_rows), :].astype(BF16)
            return carry
        lax.fori_loop(0, tn // cast_rows, body, 0)

    @pl.when(n * tn >= IN_WIDE_BEFORE_DT)
    def _():
        def body(i, carry):
            r0 = pl.multiple_of(i * gap, gap)
            wb_ref[pl.ds(r0, gap), :] = w_ref[pl.ds(r0 + gap, gap), :].astype(BF16)
            return carry
        lax.fori_loop(0, tn // gap - 1, body, 0)
        wb_ref[tn - gap:, :] = wnext_ref[...].astype(BF16)

    _project_and_mix(n, u_ref, wb_ref, proj_ref, small_ref, ssd_refs, save_ref, xcat_ref,
                     state_ref, y_ref)

    @pl.when(n == pl.num_programs(0) - 1)
    def _():
        state_out_ref[...] = state_ref[...]
        tail_out_ref[...] = xcat_ref[T - V7X_BF16_ROWS:T, :]


def _in_proj_rest_kernel(n_cast, x_ref, g_ref, w_ref, ws_ref, *rest):
    ssd_refs = rest[:12]
    state_in_ref, tail_in_ref = rest[12:14]
    cast_in = rest[17:17 + n_cast]
    proj_ref, small_ref, y_ref = rest[17 + n_cast:20 + n_cast]
    cast_out = rest[20 + n_cast:20 + 2 * n_cast]
    u_ref, save_ref, xcat_ref, state_ref = rest[20 + 2 * n_cast:]
    n = pl.program_id(1)
    T = SSD_CHUNK

    for src_ref, dst_ref in zip(cast_in, cast_out):
        _cast_block(src_ref, dst_ref)

    @pl.when((pl.program_id(0) == 0) & (n == 0))
    def _():
        xcat_ref[0:T - V7X_BF16_ROWS, :] = jnp.zeros((T - V7X_BF16_ROWS, SSD_CONV_WIDTH), BF16)
        xcat_ref[T - V7X_BF16_ROWS:T, :] = tail_in_ref[...]
        state_ref[...] = state_in_ref[...]

    @pl.when(n == 0)
    def _():
        _rmsnorm_rows(x_ref, g_ref, u_ref, x_ref.shape[0])
        small_ref[...] = _dot_nt(u_ref[...], ws_ref[...])

    _project_and_mix(n, u_ref, w_ref, proj_ref, small_ref, ssd_refs, save_ref, xcat_ref,
                     state_ref, y_ref)


def _in_proj(x, g, w_in_t, ssd_params, later_weights):
    seq, d = x.shape
    n_blocks = IN_WIDE // PROJ_TN
    tn1 = PROJ_FIRST_TN
    assert SSD_IN_BLOCKS + SSD_STEPS <= n_blocks
    gap = SSD_HEADS
    if_rows = 2 * MLSTM_HEADS
    y_rows = SSD_PER_STEP * SSD_CHUNK
    y_step = lambda n, tn: jnp.clip(n - IN_WIDE_BEFORE_DT // tn, 0, SSD_STEPS - 1)
    ssd_consts = (*ssd_params, *_ssd_constants())
    state_shape = (SSD_GROUPS, SSD_STATE, SSD_GROUP_WIDTH)
    tail_shape = (V7X_BF16_ROWS, SSD_CONV_WIDTH)
    scratch = lambda tn: [
        pltpu.VMEM((PROJ_TM, d), BF16),
        pltpu.VMEM((IN_WIDE_BEFORE_DT // tn, PROJ_TM, tn), BF16),
        pltpu.VMEM((2 * SSD_CHUNK, SSD_CONV_WIDTH), BF16),
        pltpu.VMEM(state_shape, F32),
    ]
    full1 = lambda a: pl.BlockSpec(a.shape, lambda n: (0,) * a.ndim)
    proj, small, y_ssd, w_wide, w_small, state, tail = pl.pallas_call(
        _in_proj_first_kernel,
        grid=(IN_WIDE // tn1,),
        in_specs=[
            pl.BlockSpec((PROJ_TM, d), lambda n: (0, 0), pipeline_mode=pl.Buffered(1)),
            pl.BlockSpec((1, d), lambda n: (0, 0)),
            pl.BlockSpec((tn1, d), lambda n: (n, 0)),
            pl.BlockSpec((gap, d), lambda n: ((n + 1) * (tn1 // gap), 0)),
            pl.BlockSpec((gap, d), lambda n: (IN_WIDE_BEFORE_DT // gap, 0)),
            pl.BlockSpec((if_rows, d), lambda n: (IN_I_COL // if_rows, 0)),
            *[full1(a) for a in ssd_consts],
        ],
        out_specs=[
            pl.BlockSpec((PROJ_TM, tn1), lambda n: (0, n)),
            pl.BlockSpec((PROJ_TM, SMALL_WIDTH), lambda n: (0, 0)),
            pl.BlockSpec((y_rows, SSD_WIDTH), lambda n: (y_step(n, tn1), 0)),
            pl.BlockSpec((tn1, d), lambda n: (n, 0)),
            pl.BlockSpec((SMALL_WIDTH, d), lambda n: (0, 0)),
            pl.BlockSpec(state_shape, lambda n: (0, 0, 0)),
            pl.BlockSpec(tail_shape, lambda n: (0, 0)),
        ],
        out_shape=[
            jax.ShapeDtypeStruct((seq, IN_WIDE), BF16),
            jax.ShapeDtypeStruct((seq, SMALL_WIDTH), F32),
            jax.ShapeDtypeStruct((seq, SSD_WIDTH), BF16),
            jax.ShapeDtypeStruct((IN_WIDE, d), BF16),
            jax.ShapeDtypeStruct((SMALL_WIDTH, d), BF16),
            jax.ShapeDtypeStruct(state_shape, F32),
            jax.ShapeDtypeStruct(tail_shape, BF16),
        ],
        scratch_shapes=scratch(tn1),
        compiler_params=pltpu.CompilerParams(
            dimension_semantics=("arbitrary",),
            vmem_limit_bytes=VMEM_LIMIT_BYTES),
        name="in_proj_first",
    )(x, g, w_in_t, w_in_t, w_in_t, w_in_t, *ssd_consts)

    m_tiles = seq // PROJ_TM - 1
    steps = m_tiles * n_blocks
    cast_specs = [_slab_spec(w.shape[0], w.shape[1], steps, lambda m, n: m * n_blocks + n)
                  for w in later_weights]
    full2 = lambda a: pl.BlockSpec(a.shape, lambda m, n: (0,) * a.ndim)
    n_in_before_alias = 4 + len(ssd_consts) + 2
    outs = pl.pallas_call(
        functools.partial(_in_proj_rest_kernel, len(later_weights)),
        grid=(m_tiles, n_blocks),
        in_specs=[
            pl.BlockSpec((PROJ_TM, d), lambda m, n: (m + 1, 0), pipeline_mode=pl.Buffered(1)),
            pl.BlockSpec((1, d), lambda m, n: (0, 0)),
            pl.BlockSpec((PROJ_TN, d), lambda m, n: (n, 0)),
            pl.BlockSpec((SMALL_WIDTH, d), lambda m, n: (0, 0)),
            *[full2(a) for a in ssd_consts],
            full2(state),
            full2(tail),
            pl.BlockSpec(memory_space=pl.ANY),
            pl.BlockSpec(memory_space=pl.ANY),
            pl.BlockSpec(memory_space=pl.ANY),
            *cast_specs,
        ],
        out_specs=[
            pl.BlockSpec((PROJ_TM, PROJ_TN), lambda m, n: (m + 1, n)),
            pl.BlockSpec((PROJ_TM, SMALL_WIDTH), lambda m, n: (m + 1, 0)),
            pl.BlockSpec((y_rows, SSD_WIDTH),
                         lambda m, n: ((m + 1) * SSD_STEPS + y_step(n, PROJ_TN), 0)),
            *cast_specs,
        ],
        out_shape=[
            jax.ShapeDtypeStruct((seq, IN_WIDE), BF16),
            jax.ShapeDtypeStruct((seq, SMALL_WIDTH), F32),
            jax.ShapeDtypeStruct((seq, SSD_WIDTH), BF16),
            *[jax.ShapeDtypeStruct(w.shape, BF16) for w in later_weights],
        ],
        scratch_shapes=scratch(PROJ_TN),
        input_output_aliases={n_in_before_alias: 0, n_in_before_alias + 1: 1,
                              n_in_before_alias + 2: 2},
        compiler_params=pltpu.CompilerParams(
            dimension_semantics=("arbitrary", "arbitrary"),
            vmem_limit_bytes=VMEM_LIMIT_BYTES),
        name="in_proj_rest",
    )(x, g, w_wide, w_small, *ssd_consts, state, tail, proj, small, y_ssd, *later_weights)
    return outs[0], outs[1], outs[2], outs[3:]


def _cummax_rows(x):
    rows = x.shape[0]
    row = lax.broadcasted_iota(jnp.int32, x.shape, 0)
    d = 1
    while d < rows:
        if d < V7X_SUBLANES:
            shifted = jnp.where(row < d, -jnp.inf, pltpu.roll(x, d, axis=0))
        else:
            shifted = jnp.concatenate(
                [jnp.full((d, x.shape[1]), -jnp.inf, x.dtype), x[:rows - d]], axis=0)
        x = jnp.maximum(x, shifted)
        d *= 2
    return x


def _mlstm_chunk(q_ref, k_ref, v0_ref, v1_ref, o0_ref, o1_ref, small_ref, ib_ref, fb_ref,
                 ng_ref, tri_ref, negmask_ref, ct_ref, nmat_ref, m_ref, h_ref, h_row0):
    T = MLSTM_SUB
    DK = MLSTM_QK_DIM
    DV = MLSTM_V_DIM
    log2_scale = math.log2(DK ** -0.5)
    half = MLSTM_HEADS // 2

    neg_mask = negmask_ref[...]
    half_ng = 0.5 * ng_ref[...]
    lane = lax.broadcasted_iota(jnp.int32, (T, SMALL_WIDTH), 1)
    head_lane = (lane >= F_LANE) & (lane < F_LANE + MLSTM_HEADS)
    lane_dk = lax.broadcasted_iota(jnp.int32, (DK, SMALL_WIDTH), 1)

    for sub in range(MLSTM_CHUNK // T):
        r0 = sub * T
        sm = small_ref[r0:r0 + T, :]
        log_i = pltpu.roll(_softcap(sm + ib_ref[...]) * LOG2E, F_LANE - I_LANE, axis=1)
        log_i = jnp.where(head_lane, log_i, 0.0)
        log_f = jnp.where(head_lane, _log_sigmoid(_softcap(sm + fb_ref[...])) * LOG2E, 0.0)
        bcum = _dot01_left(tri_ref[...], log_f)
        r_c = log_i - bcum
        m_prev = m_ref[0:1, :]
        inter_log = bcum + m_prev
        m_t = jnp.maximum(inter_log, bcum + _cummax_rows(r_c))
        col_term = bcum - m_t
        w_inter = jnp.exp2(inter_log - m_t)
        e_neg_m = jnp.exp2(-m_t)
        b_last = bcum[T - 1:T, :]
        a_c = b_last + r_c
        m_loc = jnp.max(a_c, axis=0, keepdims=True)
        m_new = jnp.maximum(b_last + m_prev, m_loc)
        s_old = jnp.exp2(b_last + m_prev - m_new)
        s_new = jnp.exp2(m_loc - m_new)
        w_rows = (jnp.exp2(a_c - m_loc + log2_scale) * s_new).T
        row_term = (r_c + log2_scale).T
        q_all = q_ref[r0:r0 + T, :]
        nq_inter = _dot(q_all, nmat_ref[...].astype(BF16))
        yield

        for h in range(MLSTM_HEADS):
            v_ref, o_ref = (v0_ref, o0_ref) if h < half else (v1_ref, o1_ref)
            hv = h % half
            hl = F_LANE + h
            q = q_all[:, h * DK:(h + 1) * DK]
            k = k_ref[r0:r0 + T, h * DK:(h + 1) * DK]
            v = v_ref[r0:r0 + T, hv * DV:(hv + 1) * DV]
            ct = ct_ref[h]

            kw = k.astype(F32).T * w_rows[hl:hl + 1, :]
            c_loc = _dot(kw.astype(BF16), v)
            n_loc = jnp.sum(kw, axis=1, keepdims=True)

            p = _dot_nt(q, k) * jnp.exp2(
                col_term[:, hl:hl + 1] + row_term[hl:hl + 1, :] + neg_mask)
            wi = w_inter[:, hl:hl + 1]
            num = _dot(p.astype(BF16), v) + wi * _dot(q, ct.astype(BF16))
            nq = jnp.sum(p, axis=1, keepdims=True) + wi * nq_inter[:, hl:hl + 1]
            inv_den = 1.0 / jnp.maximum(jnp.abs(nq), e_neg_m[:, hl:hl + 1])
            ms = jnp.mean(num * num, axis=-1, keepdims=True)
            out_scale = inv_den * lax.rsqrt(inv_den * inv_den * ms + EPS)
            hng = half_ng[:, h * DV:(h + 1) * DV]
            og = o_ref[r0:r0 + T, hv * DV:(hv + 1) * DV].astype(F32)
            gate = jnp.tanh(0.5 * og) * hng + hng
            h_ref[pl.ds(h_row0 + r0, T), h * DV:(h + 1) * DV] = (
                num * out_scale * gate).astype(h_ref.dtype)

            so = s_old[:, hl:hl + 1]
            ct_ref[h] = so * ct + c_loc
            n_rows = nmat_ref[h * DK:(h + 1) * DK, :]
            nmat_ref[h * DK:(h + 1) * DK, :] = so * n_rows + jnp.where(lane_dk == hl, n_loc, 0.0)
            yield

        m_ref[...] = jnp.broadcast_to(m_new, m_ref.shape)


def _mlstm_out_kernel(x_ref, ys_ref, wt_ref, wb_ref,
                      q_ref, k_ref, v0_ref, v1_ref, o0_ref, o1_ref, small_ref, ib_ref, fb_ref,
                      ng_ref, tri_ref, negmask_ref, wd_ref, wu_ref,
                      h1_ref, wd_bf_ref, wu_bf_ref,
                      yml_even_ref, yml_odd_ref, ct_ref, nmat_ref, m_ref):
    m = pl.program_id(0)
    n = pl.program_id(1)
    last = pl.num_programs(0) - 1

    _cast_block(wd_ref, wd_bf_ref)
    _cast_block(wu_ref, wu_bf_ref)

    @pl.when((m == 0) & (n == 0))
    def _():
        ct_ref[...] = jnp.zeros(ct_ref.shape, F32)
        nmat_ref[...] = jnp.zeros(nmat_ref.shape, F32)
        m_ref[...] = jnp.zeros(m_ref.shape, F32)

    def project(yml_ref):
        col_w = OUT_TN // 2
        k_w = 2 * SSD_WIDTH // (OUT_PIECES // 2)
        for piece in range(OUT_PIECES):
            cols = slice((piece % 2) * col_w, (piece % 2 + 1) * col_w)
            k0 = (piece // 2) * k_w
            src, w_ref = (ys_ref, wt_ref) if k0 < SSD_WIDTH else (yml_ref, wb_ref)
            k0 %= SSD_WIDTH
            part = _dot(src[:, k0:k0 + k_w], w_ref[k0:k0 + k_w, cols])
            if piece < 2:
                h1_ref[:, cols] = x_ref[:, cols] + part
            else:
                h1_ref[:, cols] += part
            yield

    def mix(yml_ref):
        return _mlstm_chunk(q_ref, k_ref, v0_ref, v1_ref, o0_ref, o1_ref, small_ref, ib_ref,
                            fb_ref, ng_ref, tri_ref, negmask_ref, ct_ref, nmat_ref, m_ref,
                            yml_ref, pl.multiple_of(n * MLSTM_CHUNK, MLSTM_CHUNK))

    seg_total = (MLSTM_CHUNK // MLSTM_SUB) * (MLSTM_HEADS + 1)

    @pl.when(m == 0)
    def _():
        for _ in mix(yml_even_ref):
            pass

    @pl.when((m > 0) & (m < last) & (m % 2 == 1))
    def _():
        _run_interleaved((project(yml_even_ref), OUT_PIECES), (mix(yml_odd_ref), seg_total))

    @pl.when((m > 0) & (m < last) & (m % 2 == 0))
    def _():
        _run_interleaved((project(yml_odd_ref), OUT_PIECES), (mix(yml_even_ref), seg_total))

    @pl.when(m == last)
    def _():
        for _ in project(yml_odd_ref):
            pass


def _mlstm_out(x, y_ssd, w_out_bf, proj, small, ib_row, fb_row, norm_g_row, w_down, w_up):
    seq, d = x.shape
    T = MLSTM_CHUNK
    tiles = seq // OUT_TM
    n_tiles = d // OUT_TN
    assert tiles % 2 == 0 and OUT_TM == n_tiles * T
    steps = (tiles + 1) * n_tiles
    half_w = MLSTM_WIDTH // 2
    q_blk = IN_WIDE_BEFORE_DT // MLSTM_QK_WIDTH
    v_blk = (IN_WIDE_BEFORE_DT + 2 * MLSTM_QK_WIDTH) // half_w
    tri, _, neg_mask = _causal_constants(MLSTM_SUB)
    consts = (ib_row, fb_row, norm_g_row, tri, neg_mask)
    full = lambda a: pl.BlockSpec(a.shape, lambda m, n: (0, 0))
    prev = lambda m: jnp.maximum(m - 1, 0)
    chunk = lambda m, n: jnp.minimum(m * n_tiles + n, seq // T - 1)
    step_of = lambda m, n: m * n_tiles + n
    cast_specs = [_slab_spec(w.shape[0], w.shape[1], steps, step_of) for w in (w_down, w_up)]
    h1, wd_bf, wu_bf = pl.pallas_call(
        _mlstm_out_kernel,
        grid=(tiles + 1, n_tiles),
        in_specs=[
            pl.BlockSpec((OUT_TM, OUT_TN), lambda m, n: (prev(m), n)),
            pl.BlockSpec((OUT_TM, SSD_WIDTH), lambda m, n: (prev(m), 0)),
            pl.BlockSpec((SSD_WIDTH, OUT_TN), lambda m, n: (0, n)),
            pl.BlockSpec((MLSTM_WIDTH, OUT_TN), lambda m, n: (1, n)),
            pl.BlockSpec((T, MLSTM_QK_WIDTH), lambda m, n: (chunk(m, n), q_blk)),
            pl.BlockSpec((T, MLSTM_QK_WIDTH), lambda m, n: (chunk(m, n), q_blk + 1)),
            pl.BlockSpec((T, half_w), lambda m, n: (chunk(m, n), v_blk)),
            pl.BlockSpec((T, half_w), lambda m, n: (chunk(m, n), v_blk + 1)),
            pl.BlockSpec((T, half_w), lambda m, n: (chunk(m, n), v_blk + 2)),
            pl.BlockSpec((T, half_w), lambda m, n: (chunk(m, n), v_blk + 3)),
            pl.BlockSpec((T, SMALL_WIDTH), lambda m, n: (chunk(m, n), 0)),
            *[full(a) for a in consts],
            *cast_specs,
        ],
        out_specs=[
            pl.BlockSpec((OUT_TM, OUT_TN), lambda m, n: (prev(m), jnp.where(m == 0, 0, n))),
            *cast_specs,
        ],
        out_shape=[
            jax.ShapeDtypeStruct((seq, d), F32),
            jax.ShapeDtypeStruct(w_down.shape, BF16),
            jax.ShapeDtypeStruct(w_up.shape, BF16),
        ],
        scratch_shapes=[
            pltpu.VMEM((OUT_TM, MLSTM_WIDTH), BF16),
            pltpu.VMEM((OUT_TM, MLSTM_WIDTH), BF16),
            pltpu.VMEM((MLSTM_HEADS, MLSTM_QK_DIM, MLSTM_V_DIM), F32),
            pltpu.VMEM((MLSTM_QK_WIDTH, SMALL_WIDTH), F32),
            pltpu.VMEM((V7X_SUBLANES, SMALL_WIDTH), F32),
        ],
        compiler_params=pltpu.CompilerParams(
            dimension_semantics=("arbitrary", "arbitrary"),
            vmem_limit_bytes=VMEM_LIMIT_BYTES),
        name="mlstm_out",
    )(x, y_ssd, w_out_bf, w_out_bf, proj, proj, proj, proj, proj, proj, small, *consts,
      w_down, w_up)
    return h1, wd_bf, wu_bf


def _ffn_kernel(h_ref, g_ref, wg_ref, wu_ref, wd_ref, gf_ref, out_ref, u_ref):
    f = pl.program_id(1)

    @pl.when(f == 0)
    def _():
        _rmsnorm_rows(h_ref, g_ref, u_ref, h_ref.shape[0])
        out_ref[...] = h_ref[...]

    u = u_ref[...]
    sub = wg_ref.shape[1] // FFN_SPLIT
    acc = None
    for s in range(FFN_SPLIT):
        gate = _dot(u, wg_ref[:, s * sub:(s + 1) * sub])
        up = _dot(u, wu_ref[:, s * sub:(s + 1) * sub])
        a = (gate * _sigmoid(gate) * up).astype(BF16)
        part = _dot(a, wd_ref[s * sub:(s + 1) * sub, :])
        acc = part if acc is None else acc + part
    out_ref[...] += acc

    @pl.when(f == pl.num_programs(1) - 1)
    def _():
        _rmsnorm_rows(out_ref, gf_ref, out_ref, out_ref.shape[0])


def _ffn(h1, g, w_gate, w_up, w_down, g_final):
    seq, d = h1.shape
    d_ff = w_gate.shape[1]
    grid = (seq // FFN_TM, d_ff // FFN_TF)
    return pl.pallas_call(
        _ffn_kernel,
        grid=grid,
        in_specs=[
            pl.BlockSpec((FFN_TM, d), lambda m, f: (m, 0)),
            pl.BlockSpec((1, d), lambda m, f: (0, 0)),
            pl.BlockSpec((d, FFN_TF), lambda m, f: (0, f)),
            pl.BlockSpec((d, FFN_TF), lambda m, f: (0, f)),
            pl.BlockSpec((FFN_TF, d), lambda m, f: (f, 0)),
            pl.BlockSpec((1, d), lambda m, f: (0, 0)),
        ],
        out_specs=pl.BlockSpec((FFN_TM, d), lambda m, f: (m, 0)),
        out_shape=jax.ShapeDtypeStruct((seq, d), F32),
        scratch_shapes=[pltpu.VMEM((FFN_TM, d), BF16)],
        compiler_params=pltpu.CompilerParams(
            dimension_semantics=("arbitrary", "arbitrary"),
            vmem_limit_bytes=VMEM_LIMIT_BYTES),
        name="ffn",
    )(h1, g, w_gate, w_up, w_down, g_final)


def _pad_lanes(vec, lane0):
    return jnp.zeros((1, SMALL_WIDTH), F32).at[0, lane0:lane0 + vec.shape[0]].set(vec.astype(F32))


def _layer(h, norm_mix_g, w_in, conv_w, conv_b, dt_bias, a_log, d_skip, ssd_norm_g,
           i_bias, f_bias, mlstm_norm_g, w_out, norm_ffn_g, w_gate, w_up, w_down, out_g):
    d = h.shape[1]
    ssd_params = (conv_w.astype(F32), conv_b.reshape(1, -1).astype(F32),
                  _pad_lanes(dt_bias, DT_LANE), _pad_lanes(a_log, DT_LANE),
                  jnp.repeat(d_skip.astype(F32), SSD_HEAD_DIM).reshape(1, -1),
                  ssd_norm_g.reshape(1, -1).astype(F32))
    proj, small, y_ssd, (w_out_bf, wg_bf) = _in_proj(
        h, norm_mix_g.reshape(1, d), w_in.T, ssd_params, (w_out, w_gate))

    h1, wd_bf, wu_bf = _mlstm_out(
        h, y_ssd, w_out_bf, proj, small, _pad_lanes(i_bias, I_LANE), _pad_lanes(f_bias, F_LANE),
        mlstm_norm_g.reshape(1, -1).astype(F32), w_down, w_up)
    return _ffn(h1, norm_ffn_g.reshape(1, d), wg_bf, wu_bf, wd_bf, out_g.reshape(1, d))


def kernel(x, norm_mix_g, w_in, conv_w, conv_b, dt_bias, a_log, d_skip, ssd_norm_g, i_bias,
           f_bias, mlstm_norm_g, w_out, norm_ffn_g, w_gate, w_up, w_down, final_norm_g):
    batch, seq, d = x.shape
    depth = w_in.shape[0]
    assert batch == 1 and depth == 1, "single-sequence, single-layer problem"
    out = _layer(x.reshape(seq, d), norm_mix_g[0], w_in[0], conv_w[0], conv_b[0], dt_bias[0],
                 a_log[0], d_skip[0], ssd_norm_g[0], i_bias[0], f_bias[0], mlstm_norm_g[0],
                 w_out[0], norm_ffn_g[0], w_gate[0], w_up[0], w_down[0], final_norm_g)
    return out.reshape(batch, seq, d)
```

```python
import functools
import math

import jax
import jax.numpy as jnp
from jax import lax
from jax.experimental import pallas as pl
from jax.experimental.pallas import tpu as pltpu

F32 = jnp.float32
BF16 = jnp.bfloat16
EPS = 1e-6
LOG2E = math.log2(math.e)

V7X_LANES = 128
V7X_SUBLANES = 8
V7X_BF16_ROWS = 16
VMEM_LIMIT_BYTES = 56 * 1024 * 1024

SSD_HEADS = 32
SSD_HEAD_DIM = 64
SSD_WIDTH = SSD_HEADS * SSD_HEAD_DIM
SSD_GROUPS = 4
SSD_STATE = 128
SSD_CONV = 4
SSD_GROUP_WIDTH = SSD_WIDTH // SSD_GROUPS
SSD_BC_WIDTH = SSD_GROUPS * SSD_STATE
SSD_CONV_WIDTH = SSD_WIDTH + 2 * SSD_BC_WIDTH
MLSTM_HEADS = 8
MLSTM_QK_DIM = 128
MLSTM_V_DIM = 256
MLSTM_WIDTH = MLSTM_HEADS * MLSTM_V_DIM
MLSTM_QK_WIDTH = MLSTM_HEADS * MLSTM_QK_DIM
GATE_SOFTCAP = 15.0

IN_WIDE_BEFORE_DT = SSD_WIDTH + SSD_CONV_WIDTH
IN_WIDE = IN_WIDE_BEFORE_DT + 2 * MLSTM_QK_WIDTH + 2 * MLSTM_WIDTH
IN_I_COL = IN_WIDE + SSD_HEADS
DT_LANE = 0
I_LANE = SSD_HEADS
F_LANE = SSD_HEADS + MLSTM_HEADS
SMALL_WIDTH = V7X_LANES

SSD_CHUNK = 128
SSD_PER_STEP = 2
MLSTM_CHUNK = 256
MLSTM_SUB = 128
PROJ_TM = 1024
PROJ_TN = 1024
PROJ_FIRST_TN = 512
PROJ_PIECE = 256
OUT_TM = 1024
OUT_TN = 512
OUT_PIECES = 16
FFN_TM = 1024
FFN_TF = 512
FFN_SPLIT = 2
NORM_ROWS = 128
CAST_ROWS = 256

SSD_IN_BLOCKS = IN_WIDE_BEFORE_DT // PROJ_TN
SSD_STEPS = PROJ_TM // (SSD_PER_STEP * SSD_CHUNK)


def _dot(a, b):
    return jnp.dot(a, b, preferred_element_type=F32)


def _dot_nt(a, b):
    return lax.dot_general(a, b, (((1,), (1,)), ((), ())), preferred_element_type=F32)


def _split3(x):
    hi = x.astype(BF16)
    r1 = x - hi.astype(F32)
    mid = r1.astype(BF16)
    lo = (r1 - mid.astype(F32)).astype(BF16)
    return hi, mid, lo


def _dot01_left(m01, x):
    hi, mid, lo = _split3(x)
    return _dot(m01, hi) + _dot(m01, mid) + _dot(m01, lo)


def _dot01_right(x, m01):
    hi, mid, lo = _split3(x)
    return _dot(hi, m01) + _dot(mid, m01) + _dot(lo, m01)


def _sigmoid(x):
    return 0.5 * jnp.tanh(0.5 * x) + 0.5


def _softplus(x):
    return jnp.maximum(x, 0.0) + jnp.log1p(jnp.exp(-jnp.abs(x)))


def _log_sigmoid(x):
    return jnp.minimum(x, 0.0) - jnp.log1p(jnp.exp(-jnp.abs(x)))


def _softcap(x):
    return GATE_SOFTCAP * jnp.tanh(x / GATE_SOFTCAP)


def _rmsnorm_rows(src_ref, g_ref, dst_ref, rows):
    g = g_ref[...]

    def body(i, carry):
        r0 = pl.multiple_of(i * NORM_ROWS, NORM_ROWS)
        x = src_ref[pl.ds(r0, NORM_ROWS), :].astype(F32)
        y = x * lax.rsqrt(jnp.mean(x * x, axis=-1, keepdims=True) + EPS)
        dst_ref[pl.ds(r0, NORM_ROWS), :] = (y * g).astype(dst_ref.dtype)
        return carry

    lax.fori_loop(0, rows // NORM_ROWS, body, 0)


def _cast_block(src_ref, dst_ref):
    dst_ref[...] = src_ref[...].astype(dst_ref.dtype)


def _causal_constants(t):
    row = lax.broadcasted_iota(jnp.int32, (t, t), 0)
    col = lax.broadcasted_iota(jnp.int32, (t, t), 1)
    tri = (row >= col).astype(BF16)
    tri_t = (row <= col).astype(BF16)
    neg_mask = jnp.where(row >= col, 0.0, -jnp.inf).astype(F32)
    return tri, tri_t, neg_mask


def _run_interleaved(*streams):
    done = [0] * len(streams)
    total = max(count for _, count in streams)
    for tick in range(1, total + 1):
        for i, (gen, count) in enumerate(streams):
            want = -(-tick * count // total)
            while done[i] < want:
                next(gen, None)
                done[i] += 1
    for gen, _ in streams:
        for _ in gen:
            pass


def _slab_spec(n_rows, n_cols, steps, step_of):
    slab = -(-n_rows // steps)
    slab += -slab % V7X_BF16_ROWS
    last = -(-n_rows // slab) - 1
    return pl.BlockSpec((slab, n_cols), lambda *idx: (jnp.minimum(step_of(*idx), last), 0))


def _ssd_chunk(save_ref, small_ref, convw_ref, convb_ref, dtb_ref, alog_ref, dskip_ref, ng_ref,
               tri_ref, trit_ref, negmask_ref, expand_ref, shift_ref, lanemask_ref,
               xcat_ref, state_ref, row0, y_ref, y_row0):
    T = SSD_CHUNK
    P = SSD_HEAD_DIM
    NS = SSD_STATE
    GW = SSD_GROUP_WIDTH
    TAIL = V7X_BF16_ROWS
    rows = pl.ds(row0, T)
    tn = save_ref.shape[2]
    z_block0 = 0
    xbc_block0 = SSD_WIDTH // tn

    for b in range(SSD_CONV_WIDTH // tn):
        xcat_ref[T:2 * T, b * tn:(b + 1) * tn] = save_ref[xbc_block0 + b, rows, :]

    def conv_silu(c0, width):
        shifted = _dot(shift_ref[...], xcat_ref[:, c0:c0 + width])
        conv = convb_ref[:, c0:c0 + width] + (
            xcat_ref[T:2 * T, c0:c0 + width].astype(F32) * convw_ref[SSD_CONV - 1:SSD_CONV, c0:c0 + width])
        for back in range(1, SSD_CONV):
            tap = SSD_CONV - 1 - back
            conv = conv + shifted[(back - 1) * T:back * T, :] * convw_ref[tap:tap + 1, c0:c0 + width]
        return conv * _sigmoid(conv)

    lane = lax.broadcasted_iota(jnp.int32, (T, SMALL_WIDTH), 1)
    dt = jnp.where(lane < SSD_HEADS, _softplus(small_ref[rows, :] + dtb_ref[...]), 0.0)
    a_dt = dt * (-jnp.exp(alog_ref[...]))
    col_l = _dot01_left(tri_ref[...], a_dt) * LOG2E
    dt_t = dt.T
    row_l = (_dot01_right(a_dt.T, trit_ref[...]) - jnp.log(dt_t)) * LOG2E
    e_col = jnp.exp2(col_l)
    last_l = col_l[T - 1:T, :]
    w2 = dt * jnp.exp2(last_l - col_l)
    decay_wide = _dot01_right(
        jnp.broadcast_to(jnp.exp2(last_l), (V7X_SUBLANES, SMALL_WIDTH)), expand_ref[...])[0:1, :]

    neg_mask = negmask_ref[...]
    lane_t = lax.broadcasted_iota(jnp.int32, (T, V7X_LANES), 1)
    first_head = lane_t < P
    mask_a = lanemask_ref[0:1, :]
    mask_b = lanemask_ref[1:2, :]

    bc_act = conv_silu(SSD_WIDTH, 2 * SSD_BC_WIDTH)
    yield
    for g in range(SSD_GROUPS):
        gcols = slice(g * GW, (g + 1) * GW)
        xs_g = conv_silu(g * GW, GW)
        b_f = bc_act[:, g * NS:(g + 1) * NS]
        c_bf = bc_act[:, SSD_BC_WIDTH + g * NS:SSD_BC_WIDTH + (g + 1) * NS].astype(BF16)
        cb = _dot_nt(c_bf, b_f.astype(BF16))
        state_g = state_ref[g]
        y_off = _dot(c_bf, state_g.astype(BF16))

        y_pairs = []
        xd_pairs = []
        for j in range(GW // V7X_LANES):
            ha = g * (GW // P) + 2 * j
            hb = ha + 1
            m_a = (cb * jnp.exp2(col_l[:, ha:ha + 1] - row_l[ha:ha + 1, :] + neg_mask)).astype(BF16)
            m_b = (cb * jnp.exp2(col_l[:, hb:hb + 1] - row_l[hb:hb + 1, :] + neg_mask)).astype(BF16)
            xp = xs_g[:, j * V7X_LANES:(j + 1) * V7X_LANES]
            xp_bf = xp.astype(BF16)
            y_diag = _dot(jnp.concatenate([m_a, m_b], axis=1),
                          jnp.concatenate([xp_bf * mask_a, xp_bf * mask_b], axis=0))
            off_scale = jnp.where(first_head,
                                  jnp.broadcast_to(e_col[:, ha:ha + 1], (T, V7X_LANES)),
                                  jnp.broadcast_to(e_col[:, hb:hb + 1], (T, V7X_LANES)))
            w2_pair = jnp.where(first_head,
                                jnp.broadcast_to(w2[:, ha:ha + 1], (T, V7X_LANES)),
                                jnp.broadcast_to(w2[:, hb:hb + 1], (T, V7X_LANES)))
            y_pairs.append(y_diag + off_scale * y_off[:, j * V7X_LANES:(j + 1) * V7X_LANES])
            xd_pairs.append((xp * w2_pair).astype(BF16))

        y_g = jnp.concatenate(y_pairs, axis=1) + dskip_ref[:, gcols] * xs_g
        xd_g = jnp.concatenate(xd_pairs, axis=1)
        states_g = _dot(b_f.T.astype(BF16), xd_g)
        state_ref[g] = state_g * decay_wide[:, gcols] + states_g

        zb, zc = divmod(g * GW, tn)
        zg = save_ref[z_block0 + zb, rows, zc:zc + GW].astype(F32)
        y_g = y_g * (zg * _sigmoid(zg))
        y_g = y_g * lax.rsqrt(jnp.mean(y_g * y_g, axis=-1, keepdims=True) + EPS)
        y_ref[pl.ds(y_row0, T), gcols] = (y_g * ng_ref[:, gcols]).astype(y_ref.dtype)
        if g == SSD_GROUPS - 1:
            xcat_ref[T - TAIL:T, :] = xcat_ref[2 * T - TAIL:2 * T, :]
        yield


def _ssd_constants():
    T = SSD_CHUNK
    tri, tri_t, neg_mask = _causal_constants(T)
    e_row = lax.broadcasted_iota(jnp.int32, (SMALL_WIDTH, SSD_WIDTH), 0)
    e_col = lax.broadcasted_iota(jnp.int32, (SMALL_WIDTH, SSD_WIDTH), 1)
    expand = (e_row == e_col // SSD_HEAD_DIM).astype(BF16)
    s_row = lax.broadcasted_iota(jnp.int32, ((SSD_CONV - 1) * T, 2 * T), 0)
    s_col = lax.broadcasted_iota(jnp.int32, ((SSD_CONV - 1) * T, 2 * T), 1)
    shift = (s_col == T + s_row % T - (s_row // T + 1)).astype(BF16)
    m_lane = lax.broadcasted_iota(jnp.int32, (2, V7X_LANES), 1)
    m_row = lax.broadcasted_iota(jnp.int32, (2, V7X_LANES), 0)
    lane_mask = ((m_lane < SSD_HEAD_DIM) == (m_row == 0)).astype(BF16)
    return tri, tri_t, neg_mask, expand, shift, lane_mask


def _project_and_mix(n, u_ref, wt_ref, proj_ref, small_ref, ssd_refs, save_ref, xcat_ref,
                     state_ref, y_ref):
    tn = proj_ref.shape[1]
    in_blocks = IN_WIDE_BEFORE_DT // tn

    def project(keep):
        for j in range(tn // PROJ_PIECE):
            cols = slice(j * PROJ_PIECE, (j + 1) * PROJ_PIECE)
            val = _dot_nt(u_ref[...], wt_ref[cols, :]).astype(proj_ref.dtype)
            proj_ref[:, cols] = val
            if keep:
                save_ref[n, :, cols] = val
            yield

    def mix():
        step_row0 = (n - in_blocks) * (SSD_PER_STEP * SSD_CHUNK)
        for c in range(SSD_PER_STEP):
            row0 = pl.multiple_of(step_row0 + c * SSD_CHUNK, SSD_CHUNK)
            yield from _ssd_chunk(save_ref, small_ref, *ssd_refs, xcat_ref, state_ref,
                                  row0, y_ref, c * SSD_CHUNK)

    @pl.when(n < in_blocks)
    def _():
        for _ in project(True):
            pass

    @pl.when((n >= in_blocks) & (n < in_blocks + SSD_STEPS))
    def _():
        _run_interleaved((project(False), tn // PROJ_PIECE),
                         (mix(), SSD_PER_STEP * (SSD_GROUPS + 1)))

    @pl.when(n >= in_blocks + SSD_STEPS)
    def _():
        for _ in project(False):
            pass


def _in_proj_first_kernel(x_ref, g_ref, w_ref, wnext_ref, wdt_ref, wif_ref, *rest):
    ssd_refs = rest[:12]
    (proj_ref, small_ref, y_ref, wb_ref, ws_ref, state_out_ref, tail_out_ref,
     u_ref, save_ref, xcat_ref, state_ref) = rest[12:]
    n = pl.program_id(0)
    gap = wnext_ref.shape[0]
    T = SSD_CHUNK

    @pl.when(n == 0)
    def _():
        _rmsnorm_rows(x_ref, g_ref, u_ref, x_ref.shape[0])
        ws_ref[DT_LANE:I_LANE, :] = wdt_ref[...].astype(BF16)
        ws_ref[I_LANE:F_LANE + MLSTM_HEADS, :] = wif_ref[...].astype(BF16)
        ws_ref[F_LANE + MLSTM_HEADS:, :] = jnp.zeros(
            (SMALL_WIDTH - F_LANE - MLSTM_HEADS, ws_ref.shape[1]), BF16)
        small_ref[...] = _dot_nt(u_ref[...], ws_ref[...])
        xcat_ref[0:T, :] = jnp.zeros((T, SSD_CONV_WIDTH), BF16)
        state_ref[...] = jnp.zeros(state_ref.shape, F32)

    tn = wb_ref.shape[0]
    cast_rows = min(CAST_ROWS, tn)

    @pl.when(n * tn < IN_WIDE_BEFORE_DT)
    def _():
        def body(i, carry):
            r0 = pl.multiple_of(i * cast_rows, cast_rows)
            wb_ref[pl.ds(r0, cast_rows), :] = w_ref[pl.ds(r0, cast_rows), :].astype(BF16)
            return carry
        lax.fori_loop(0, tn // cast_rows, body, 0)

    @pl.when(n * tn >= IN_WIDE_BEFORE_DT)
    def _():
        def body(i, carry):
            r0 = pl.multiple_of(i * gap, gap)
            wb_ref[pl.ds(r0, gap), :] = w_ref[pl.ds(r0 + gap, gap), :].astype(BF16)
            return carry
        lax.fori_loop(0, tn // gap - 1, body, 0)
        wb_ref[tn - gap:, :] = wnext_ref[...].astype(BF16)

    _project_and_mix(n, u_ref, wb_ref, proj_ref, small_ref, ssd_refs, save_ref, xcat_ref,
                     state_ref, y_ref)

    @pl.when(n == pl.num_programs(0) - 1)
    def _():
        state_out_ref[...] = state_ref[...]
        tail_out_ref[...] = xcat_ref[T - V7X_BF16_ROWS:T, :]


def _in_proj_rest_kernel(n_cast, x_ref, g_ref, w_ref, ws_ref, *rest):
    ssd_refs = rest[:12]
    state_in_ref, tail_in_ref = rest[12:14]
    cast_in = rest[17:17 + n_cast]
    proj_ref, small_ref, y_ref = rest[17 + n_cast:20 + n_cast]
    cast_out = rest[20 + n_cast:20 + 2 * n_cast]
    u_ref, save_ref, xcat_ref, state_ref = rest[20 + 2 * n_cast:]
    n = pl.program_id(1)
    T = SSD_CHUNK

    for src_ref, dst_ref in zip(cast_in, cast_out):
        _cast_block(src_ref, dst_ref)

    @pl.when((pl.program_id(0) == 0) & (n == 0))
    def _():
        xcat_ref[0:T - V7X_BF16_ROWS, :] = jnp.zeros((T - V7X_BF16_ROWS, SSD_CONV_WIDTH), BF16)
        xcat_ref[T - V7X_BF16_ROWS:T, :] = tail_in_ref[...]
        state_ref[...] = state_in_ref[...]

    @pl.when(n == 0)
    def _():
        _rmsnorm_rows(x_ref, g_ref, u_ref, x_ref.shape[0])
        small_ref[...] = _dot_nt(u_ref[...], ws_ref[...])

    _project_and_mix(n, u_ref, w_ref, proj_ref, small_ref, ssd_refs, save_ref, xcat_ref,
                     state_ref, y_ref)


def _in_proj(x, g, w_in_t, ssd_params, later_weights):
    seq, d = x.shape
    n_blocks = IN_WIDE // PROJ_TN
    tn1 = PROJ_FIRST_TN
    assert SSD_IN_BLOCKS + SSD_STEPS <= n_blocks
    gap = SSD_HEADS
    if_rows = 2 * MLSTM_HEADS
    y_rows = SSD_PER_STEP * SSD_CHUNK
    y_step = lambda n, tn: jnp.clip(n - IN_WIDE_BEFORE_DT // tn, 0, SSD_STEPS - 1)
    ssd_consts = (*ssd_params, *_ssd_constants())
    state_shape = (SSD_GROUPS, SSD_STATE, SSD_GROUP_WIDTH)
    tail_shape = (V7X_BF16_ROWS, SSD_CONV_WIDTH)
    scratch = lambda tn: [
        pltpu.VMEM((PROJ_TM, d), BF16),
        pltpu.VMEM((IN_WIDE_BEFORE_DT // tn, PROJ_TM, tn), BF16),
        pltpu.VMEM((2 * SSD_CHUNK, SSD_CONV_WIDTH), BF16),
        pltpu.VMEM(state_shape, F32),
    ]
    full1 = lambda a: pl.BlockSpec(a.shape, lambda n: (0,) * a.ndim)
    proj, small, y_ssd, w_wide, w_small, state, tail = pl.pallas_call(
        _in_proj_first_kernel,
        grid=(IN_WIDE // tn1,),
        in_specs=[
            pl.BlockSpec((PROJ_TM, d), lambda n: (0, 0), pipeline_mode=pl.Buffered(1)),
            pl.BlockSpec((1, d), lambda n: (0, 0)),
            pl.BlockSpec((tn1, d), lambda n: (n, 0)),
            pl.BlockSpec((gap, d), lambda n: ((n + 1) * (tn1 // gap), 0)),
            pl.BlockSpec((gap, d), lambda n: (IN_WIDE_BEFORE_DT // gap, 0)),
            pl.BlockSpec((if_rows, d), lambda n: (IN_I_COL // if_rows, 0)),
            *[full1(a) for a in ssd_consts],
        ],
        out_specs=[
            pl.BlockSpec((PROJ_TM, tn1), lambda n: (0, n)),
            pl.BlockSpec((PROJ_TM, SMALL_WIDTH), lambda n: (0, 0)),
            pl.BlockSpec((y_rows, SSD_WIDTH), lambda n: (y_step(n, tn1), 0)),
            pl.BlockSpec((tn1, d), lambda n: (n, 0)),
            pl.BlockSpec((SMALL_WIDTH, d), lambda n: (0, 0)),
            pl.BlockSpec(state_shape, lambda n: (0, 0, 0)),
            pl.BlockSpec(tail_shape, lambda n: (0, 0)),
        ],
        out_shape=[
            jax.ShapeDtypeStruct((seq, IN_WIDE), BF16),
            jax.ShapeDtypeStruct((seq, SMALL_WIDTH), F32),
            jax.ShapeDtypeStruct((seq, SSD_WIDTH), BF16),
            jax.ShapeDtypeStruct((IN_WIDE, d), BF16),
            jax.ShapeDtypeStruct((SMALL_WIDTH, d), BF16),
            jax.ShapeDtypeStruct(state_shape, F32),
            jax.ShapeDtypeStruct(tail_shape, BF16),
        ],
        scratch_shapes=scratch(tn1),
        compiler_params=pltpu.CompilerParams(
            dimension_semantics=("arbitrary",),
            vmem_limit_bytes=VMEM_LIMIT_BYTES),
        name="in_proj_first",
    )(x, g, w_in_t, w_in_t, w_in_t, w_in_t, *ssd_consts)

    m_tiles = seq // PROJ_TM - 1
    steps = m_tiles * n_blocks
    cast_specs = [_slab_spec(w.shape[0], w.shape[1], steps, lambda m, n: m * n_blocks + n)
                  for w in later_weights]
    full2 = lambda a: pl.BlockSpec(a.shape, lambda m, n: (0,) * a.ndim)
    n_in_before_alias = 4 + len(ssd_consts) + 2
    outs = pl.pallas_call(
        functools.partial(_in_proj_rest_kernel, len(later_weights)),
        grid=(m_tiles, n_blocks),
        in_specs=[
            pl.BlockSpec((PROJ_TM, d), lambda m, n: (m + 1, 0), pipeline_mode=pl.Buffered(1)),
            pl.BlockSpec((1, d), lambda m, n: (0, 0)),
            pl.BlockSpec((PROJ_TN, d), lambda m, n: (n, 0)),
            pl.BlockSpec((SMALL_WIDTH, d), lambda m, n: (0, 0)),
            *[full2(a) for a in ssd_consts],
            full2(state),
            full2(tail),
            pl.BlockSpec(memory_space=pl.ANY),
            pl.BlockSpec(memory_space=pl.ANY),
            pl.BlockSpec(memory_space=pl.ANY),
            *cast_specs,
        ],
        out_specs=[
            pl.BlockSpec((PROJ_TM, PROJ_TN), lambda m, n: (m + 1, n)),
            pl.BlockSpec((PROJ_TM, SMALL_WIDTH), lambda m, n: (m + 1, 0)),
            pl.BlockSpec((y_rows, SSD_WIDTH),
                         lambda m, n: ((m + 1) * SSD_STEPS + y_step(n, PROJ_TN), 0)),
            *cast_specs,
        ],
        out_shape=[
            jax.ShapeDtypeStruct((seq, IN_WIDE), BF16),
            jax.ShapeDtypeStruct((seq, SMALL_WIDTH), F32),
            jax.ShapeDtypeStruct((seq, SSD_WIDTH), BF16),
            *[jax.ShapeDtypeStruct(w.shape, BF16) for w in later_weights],
        ],
        scratch_shapes=scratch(PROJ_TN),
        input_output_aliases={n_in_before_alias: 0, n_in_before_alias + 1: 1,
                              n_in_before_alias + 2: 2},
        compiler_params=pltpu.CompilerParams(
            dimension_semantics=("arbitrary", "arbitrary"),
            vmem_limit_bytes=VMEM_LIMIT_BYTES),
        name="in_proj_rest",
    )(x, g, w_wide, w_small, *ssd_consts, state, tail, proj, small, y_ssd, *later_weights)
    return outs[0], outs[1], outs[2], outs[3:]


def _cummax_rows(x):
    rows = x.shape[0]
    row = lax.broadcasted_iota(jnp.int32, x.shape, 0)
    d = 1
    while d < rows:
        if d < V7X_SUBLANES:
            shifted = jnp.where(row < d, -jnp.inf, pltpu.roll(x, d, axis=0))
        else:
            shifted = jnp.concatenate(
                [jnp.full((d, x.shape[1]), -jnp.inf, x.dtype), x[:rows - d]], axis=0)
        x = jnp.maximum(x, shifted)
        d *= 2
    return x


def _mlstm_chunk(q_ref, k_ref, v0_ref, v1_ref, o0_ref, o1_ref, small_ref, ib_ref, fb_ref,
                 ng_ref, tri_ref, negmask_ref, ct_ref, nmat_ref, m_ref, h_ref, h_row0):
    T = MLSTM_SUB
    DK = MLSTM_QK_DIM
    DV = MLSTM_V_DIM
    log2_scale = math.log2(DK ** -0.5)
    half = MLSTM_HEADS // 2

    neg_mask = negmask_ref[...]
    half_ng = 0.5 * ng_ref[...]
    lane = lax.broadcasted_iota(jnp.int32, (T, SMALL_WIDTH), 1)
    head_lane = (lane >= F_LANE) & (lane < F_LANE + MLSTM_HEADS)
    lane_dk = lax.broadcasted_iota(jnp.int32, (DK, SMALL_WIDTH), 1)

    for sub in range(MLSTM_CHUNK // T):
        r0 = sub * T
        sm = small_ref[r0:r0 + T, :]
        log_i = pltpu.roll(_softcap(sm + ib_ref[...]) * LOG2E, F_LANE - I_LANE, axis=1)
        log_i = jnp.where(head_lane, log_i, 0.0)
        log_f = jnp.where(head_lane, _log_sigmoid(_softcap(sm + fb_ref[...])) * LOG2E, 0.0)
        bcum = _dot01_left(tri_ref[...], log_f)
        r_c = log_i - bcum
        m_prev = m_ref[0:1, :]
        inter_log = bcum + m_prev
        m_t = jnp.maximum(inter_log, bcum + _cummax_rows(r_c))
        col_term = bcum - m_t
        w_inter = jnp.exp2(inter_log - m_t)
        e_neg_m = jnp.exp2(-m_t)
        b_last = bcum[T - 1:T, :]
        a_c = b_last + r_c
        m_loc = jnp.max(a_c, axis=0, keepdims=True)
        m_new = jnp.maximum(b_last + m_prev, m_loc)
        s_old = jnp.exp2(b_last + m_prev - m_new)
        s_new = jnp.exp2(m_loc - m_new)
        w_rows = (jnp.exp2(a_c - m_loc + log2_scale) * s_new).T
        row_term = (r_c + log2_scale).T
        q_all = q_ref[r0:r0 + T, :]
        nq_inter = _dot(q_all, nmat_ref[...].astype(BF16))
        yield

        for h in range(MLSTM_HEADS):
            v_ref, o_ref = (v0_ref, o0_ref) if h < half else (v1_ref, o1_ref)
            hv = h % half
            hl = F_LANE + h
            q = q_all[:, h * DK:(h + 1) * DK]
            k = k_ref[r0:r0 + T, h * DK:(h + 1) * DK]
            v = v_ref[r0:r0 + T, hv * DV:(hv + 1) * DV]
            ct = ct_ref[h]

            kw = k.astype(F32).T * w_rows[hl:hl + 1, :]
            c_loc = _dot(kw.astype(BF16), v)
            n_loc = jnp.sum(kw, axis=1, keepdims=True)

            p = _dot_nt(q, k) * jnp.exp2(
                col_term[:, hl:hl + 1] + row_term[hl:hl + 1, :] + neg_mask)
            wi = w_inter[:, hl:hl + 1]
            num = _dot(p.astype(BF16), v) + wi * _dot(q, ct.astype(BF16))
            nq = jnp.sum(p, axis=1, keepdims=True) + wi * nq_inter[:, hl:hl + 1]
            inv_den = 1.0 / jnp.maximum(jnp.abs(nq), e_neg_m[:, hl:hl + 1])
            ms = jnp.mean(num * num, axis=-1, keepdims=True)
            out_scale = inv_den * lax.rsqrt(inv_den * inv_den * ms + EPS)
            hng = half_ng[:, h * DV:(h + 1) * DV]
            og = o_ref[r0:r0 + T, hv * DV:(hv + 1) * DV].astype(F32)
            gate = jnp.tanh(0.5 * og) * hng + hng
            h_ref[pl.ds(h_row0 + r0, T), h * DV:(h + 1) * DV] = (
                num * out_scale * gate).astype(h_ref.dtype)

            so = s_old[:, hl:hl + 1]
            ct_ref[h] = so * ct + c_loc
            n_rows = nmat_ref[h * DK:(h + 1) * DK, :]
            nmat_ref[h * DK:(h + 1) * DK, :] = so * n_rows + jnp.where(lane_dk == hl, n_loc, 0.0)
            yield

        m_ref[...] = jnp.broadcast_to(m_new, m_ref.shape)


def _mlstm_out_kernel(x_ref, ys_ref, wt_ref, wb_ref,
                      q_ref, k_ref, v0_ref, v1_ref, o0_ref, o1_ref, small_ref, ib_ref, fb_ref,
                      ng_ref, tri_ref, negmask_ref, wd_ref, wu_ref,
                      h1_ref, wd_bf_ref, wu_bf_ref,
                      yml_even_ref, yml_odd_ref, ct_ref, nmat_ref, m_ref):
    m = pl.program_id(0)
    n = pl.program_id(1)
    last = pl.num_programs(0) - 1

    _cast_block(wd_ref, wd_bf_ref)
    _cast_block(wu_ref, wu_bf_ref)

    @pl.when((m == 0) & (n == 0))
    def _():
        ct_ref[...] = jnp.zeros(ct_ref.shape, F32)
        nmat_ref[...] = jnp.zeros(nmat_ref.shape, F32)
        m_ref[...] = jnp.zeros(m_ref.shape, F32)

    def project(yml_ref):
        col_w = OUT_TN // 2
        k_w = 2 * SSD_WIDTH // (OUT_PIECES // 2)
        for piece in range(OUT_PIECES):
            cols = slice((piece % 2) * col_w, (piece % 2 + 1) * col_w)
            k0 = (piece // 2) * k_w
            src, w_ref = (ys_ref, wt_ref) if k0 < SSD_WIDTH else (yml_ref, wb_ref)
            k0 %= SSD_WIDTH
            part = _dot(src[:, k0:k0 + k_w], w_ref[k0:k0 + k_w, cols])
            if piece < 2:
                h1_ref[:, cols] = x_ref[:, cols] + part
            else:
                h1_ref[:, cols] += part
            yield

    def mix(yml_ref):
        return _mlstm_chunk(q_ref, k_ref, v0_ref, v1_ref, o0_ref, o1_ref, small_ref, ib_ref,
                            fb_ref, ng_ref, tri_ref, negmask_ref, ct_ref, nmat_ref, m_ref,
                            yml_ref, pl.multiple_of(n * MLSTM_CHUNK, MLSTM_CHUNK))

    seg_total = (MLSTM_CHUNK // MLSTM_SUB) * (MLSTM_HEADS + 1)

    @pl.when(m == 0)
    def _():
        for _ in mix(yml_even_ref):
            pass

    @pl.when((m > 0) & (m < last) & (m % 2 == 1))
    def _():
        _run_interleaved((project(yml_even_ref), OUT_PIECES), (mix(yml_odd_ref), seg_total))

    @pl.when((m > 0) & (m < last) & (m % 2 == 0))
    def _():
        _run_interleaved((project(yml_odd_ref), OUT_PIECES), (mix(yml_even_ref), seg_total))

    @pl.when(m == last)
    def _():
        for _ in project(yml_odd_ref):
            pass


def _mlstm_out(x, y_ssd, w_out_bf, proj, small, ib_row, fb_row, norm_g_row, w_down, w_up):
    seq, d = x.shape
    T = MLSTM_CHUNK
    tiles = seq // OUT_TM
    n_tiles = d // OUT_TN
    assert tiles % 2 == 0 and OUT_TM == n_tiles * T
    steps = (tiles + 1) * n_tiles
    half_w = MLSTM_WIDTH // 2
    q_blk = IN_WIDE_BEFORE_DT // MLSTM_QK_WIDTH
    v_blk = (IN_WIDE_BEFORE_DT + 2 * MLSTM_QK_WIDTH) // half_w
    tri, _, neg_mask = _causal_constants(MLSTM_SUB)
    consts = (ib_row, fb_row, norm_g_row, tri, neg_mask)
    full = lambda a: pl.BlockSpec(a.shape, lambda m, n: (0, 0))
    prev = lambda m: jnp.maximum(m - 1, 0)
    chunk = lambda m, n: jnp.minimum(m * n_tiles + n, seq // T - 1)
    step_of = lambda m, n: m * n_tiles + n
    cast_specs = [_slab_spec(w.shape[0], w.shape[1], steps, step_of) for w in (w_down, w_up)]
    h1, wd_bf, wu_bf = pl.pallas_call(
        _mlstm_out_kernel,
        grid=(tiles + 1, n_tiles),
        in_specs=[
            pl.BlockSpec((OUT_TM, OUT_TN), lambda m, n: (prev(m), n)),
            pl.BlockSpec((OUT_TM, SSD_WIDTH), lambda m, n: (prev(m), 0)),
            pl.BlockSpec((SSD_WIDTH, OUT_TN), lambda m, n: (0, n)),
            pl.BlockSpec((MLSTM_WIDTH, OUT_TN), lambda m, n: (1, n)),
            pl.BlockSpec((T, MLSTM_QK_WIDTH), lambda m, n: (chunk(m, n), q_blk)),
            pl.BlockSpec((T, MLSTM_QK_WIDTH), lambda m, n: (chunk(m, n), q_blk + 1)),
            pl.BlockSpec((T, half_w), lambda m, n: (chunk(m, n), v_blk)),
            pl.BlockSpec((T, half_w), lambda m, n: (chunk(m, n), v_blk + 1)),
            pl.BlockSpec((T, half_w), lambda m, n: (chunk(m, n), v_blk + 2)),
            pl.BlockSpec((T, half_w), lambda m, n: (chunk(m, n), v_blk + 3)),
            pl.BlockSpec((T, SMALL_WIDTH), lambda m, n: (chunk(m, n), 0)),
            *[full(a) for a in consts],
            *cast_specs,
        ],
        out_specs=[
            pl.BlockSpec((OUT_TM, OUT_TN), lambda m, n: (prev(m), jnp.where(m == 0, 0, n))),
            *cast_specs,
        ],
        out_shape=[
            jax.ShapeDtypeStruct((seq, d), F32),
            jax.ShapeDtypeStruct(w_down.shape, BF16),
            jax.ShapeDtypeStruct(w_up.shape, BF16),
        ],
        scratch_shapes=[
            pltpu.VMEM((OUT_TM, MLSTM_WIDTH), BF16),
            pltpu.VMEM((OUT_TM, MLSTM_WIDTH), BF16),
            pltpu.VMEM((MLSTM_HEADS, MLSTM_QK_DIM, MLSTM_V_DIM), F32),
            pltpu.VMEM((MLSTM_QK_WIDTH, SMALL_WIDTH), F32),
            pltpu.VMEM((V7X_SUBLANES, SMALL_WIDTH), F32),
        ],
        compiler_params=pltpu.CompilerParams(
            dimension_semantics=("arbitrary", "arbitrary"),
            vmem_limit_bytes=VMEM_LIMIT_BYTES),
        name="mlstm_out",
    )(x, y_ssd, w_out_bf, w_out_bf, proj, proj, proj, proj, proj, proj, small, *consts,
      w_down, w_up)
    return h1, wd_bf, wu_bf


def _ffn_kernel(h_ref, g_ref, wg_ref, wu_ref, wd_ref, gf_ref, out_ref, u_ref):
    f = pl.program_id(1)

    @pl.when(f == 0)
    def _():
        _rmsnorm_rows(h_ref, g_ref, u_ref, h_ref.shape[0])
        out_ref[...] = h_ref[...]

    u = u_ref[...]
    sub = wg_ref.shape[1] // FFN_SPLIT
    acc = None
    for s in range(FFN_SPLIT):
        gate = _dot(u, wg_ref[:, s * sub:(s + 1) * sub])
        up = _dot(u, wu_ref[:, s * sub:(s + 1) * sub])
        a = (gate * _sigmoid(gate) * up).astype(BF16)
        part = _dot(a, wd_ref[s * sub:(s + 1) * sub, :])
        acc = part if acc is None else acc + part
    out_ref[...] += acc

    @pl.when(f == pl.num_programs(1) - 1)
    def _():
        _rmsnorm_rows(out_ref, gf_ref, out_ref, out_ref.shape[0])


def _ffn(h1, g, w_gate, w_up, w_down, g_final):
    seq, d = h1.shape
    d_ff = w_gate.shape[1]
    grid = (seq // FFN_TM, d_ff // FFN_TF)
    return pl.pallas_call(
        _ffn_kernel,
        grid=grid,
        in_specs=[
            pl.BlockSpec((FFN_TM, d), lambda m, f: (m, 0), pipeline_mode=pl.Buffered(1)),
            pl.BlockSpec((1, d), lambda m, f: (0, 0)),
            pl.BlockSpec((d, FFN_TF), lambda m, f: (0, f)),
            pl.BlockSpec((d, FFN_TF), lambda m, f: (0, f)),
            pl.BlockSpec((FFN_TF, d), lambda m, f: (f, 0)),
            pl.BlockSpec((1, d), lambda m, f: (0, 0)),
        ],
        out_specs=pl.BlockSpec((FFN_TM, d), lambda m, f: (m, 0)),
        out_shape=jax.ShapeDtypeStruct((seq, d), F32),
        scratch_shapes=[pltpu.VMEM((FFN_TM, d), BF16)],
        compiler_params=pltpu.CompilerParams(
            dimension_semantics=("arbitrary", "arbitrary"),
            vmem_limit_bytes=VMEM_LIMIT_BYTES),
        name="ffn",
    )(h1, g, w_gate, w_up, w_down, g_final)


def _pad_lanes(vec, lane0):
    return jnp.zeros((1, SMALL_WIDTH), F32).at[0, lane0:lane0 + vec.shape[0]].set(vec.astype(F32))


def _layer(h, norm_mix_g, w_in, conv_w, conv_b, dt_bias, a_log, d_skip, ssd_norm_g,
           i_bias, f_bias, mlstm_norm_g, w_out, norm_ffn_g, w_gate, w_up, w_down, out_g):
    d = h.shape[1]
    ssd_params = (conv_w.astype(F32), conv_b.reshape(1, -1).astype(F32),
                  _pad_lanes(dt_bias, DT_LANE), _pad_lanes(a_log, DT_LANE),
                  jnp.repeat(d_skip.astype(F32), SSD_HEAD_DIM).reshape(1, -1),
                  ssd_norm_g.reshape(1, -1).astype(F32))
    proj, small, y_ssd, (w_out_bf, wg_bf) = _in_proj(
        h, norm_mix_g.reshape(1, d), w_in.T, ssd_params, (w_out, w_gate))

    h1, wd_bf, wu_bf = _mlstm_out(
        h, y_ssd, w_out_bf, proj, small, _pad_lanes(i_bias, I_LANE), _pad_lanes(f_bias, F_LANE),
        mlstm_norm_g.reshape(1, -1).astype(F32), w_down, w_up)
    return _ffn(h1, norm_ffn_g.reshape(1, d), wg_bf, wu_bf, wd_bf, out_g.reshape(1, d))


def kernel(x, norm_mix_g, w_in, conv_w, conv_b, dt_bias, a_log, d_skip, ssd_norm_g, i_bias,
           f_bias, mlstm_norm_g, w_out, norm_ffn_g, w_gate, w_up, w_down, final_norm_g):
    batch, seq, d = x.shape
    depth = w_in.shape[0]
    assert batch == 1 and depth == 1, "single-sequence, single-layer problem"
    out = _layer(x.reshape(seq, d), norm_mix_g[0], w_in[0], conv_w[0], conv_b[0], dt_bias[0],
                 a_log[0], d_skip[0], ssd_norm_g[0], i_bias[0], f_bias[0], mlstm_norm_g[0],
                 w_out[0], norm_ffn_g[0], w_gate[0], w_up[0], w_down[0], final_norm_g)
    return out.reshape(batch, seq, d)
```

```python
import functools
import math

import jax
import jax.numpy as jnp
from jax import lax
from jax.experimental import pallas as pl
from jax.experimental.pallas import tpu as pltpu

F32 = jnp.float32
BF16 = jnp.bfloat16
EPS = 1e-6
LOG2E = math.log2(math.e)

V7X_LANES = 128
V7X_SUBLANES = 8
V7X_BF16_ROWS = 16
VMEM_LIMIT_BYTES = 56 * 1024 * 1024

SSD_HEADS = 32
SSD_HEAD_DIM = 64
SSD_WIDTH = SSD_HEADS * SSD_HEAD_DIM
SSD_GROUPS = 4
SSD_STATE = 128
SSD_CONV = 4
SSD_GROUP_WIDTH = SSD_WIDTH // SSD_GROUPS
SSD_BC_WIDTH = SSD_GROUPS * SSD_STATE
SSD_CONV_WIDTH = SSD_WIDTH + 2 * SSD_BC_WIDTH
MLSTM_HEADS = 8
MLSTM_QK_DIM = 128
MLSTM_V_DIM = 256
MLSTM_WIDTH = MLSTM_HEADS * MLSTM_V_DIM
MLSTM_QK_WIDTH = MLSTM_HEADS * MLSTM_QK_DIM
GATE_SOFTCAP = 15.0

IN_WIDE_BEFORE_DT = SSD_WIDTH + SSD_CONV_WIDTH
IN_WIDE = IN_WIDE_BEFORE_DT + 2 * MLSTM_QK_WIDTH + 2 * MLSTM_WIDTH
IN_I_COL = IN_WIDE + SSD_HEADS
DT_LANE = 0
I_LANE = SSD_HEADS
F_LANE = SSD_HEADS + MLSTM_HEADS
SMALL_WIDTH = V7X_LANES

SSD_CHUNK = 128
SSD_PER_STEP = 2
SSD_TAIL = 8
SSD_N_CONSTS = 11
MLSTM_CHUNK = 256
MLSTM_SUB = 128
PROJ_TM = 1024
PROJ_TN = 1024
PROJ_FIRST_TN = 512
PROJ_PIECE = 256
OUT_TM = 1024
OUT_TN = 512
OUT_PIECES = 16
FFN_TM = 512
FFN_TF = 512
FFN_SPLIT = 2
NORM_ROWS = 128
CAST_ROWS = 256

SSD_IN_BLOCKS = IN_WIDE_BEFORE_DT // PROJ_TN
SSD_STEPS = PROJ_TM // (SSD_PER_STEP * SSD_CHUNK)


def _dot(a, b):
    return jnp.dot(a, b, preferred_element_type=F32)


def _dot_nt(a, b):
    return lax.dot_general(a, b, (((1,), (1,)), ((), ())), preferred_element_type=F32)


def _split3(x):
    hi = x.astype(BF16)
    r1 = x - hi.astype(F32)
    mid = r1.astype(BF16)
    lo = (r1 - mid.astype(F32)).astype(BF16)
    return hi, mid, lo


def _dot01_left(m01, x):
    hi, mid, lo = _split3(x)
    return _dot(m01, hi) + _dot(m01, mid) + _dot(m01, lo)


def _dot01_right(x, m01):
    hi, mid, lo = _split3(x)
    return _dot(hi, m01) + _dot(mid, m01) + _dot(lo, m01)


def _sigmoid(x):
    return 0.5 * jnp.tanh(0.5 * x) + 0.5


def _softplus(x):
    return jnp.maximum(x, 0.0) + jnp.log1p(jnp.exp(-jnp.abs(x)))


def _log_sigmoid(x):
    return jnp.minimum(x, 0.0) - jnp.log1p(jnp.exp(-jnp.abs(x)))


def _softcap(x):
    return GATE_SOFTCAP * jnp.tanh(x / GATE_SOFTCAP)


def _rmsnorm_rows(src_ref, g_ref, dst_ref, rows):
    g = g_ref[...]

    def body(i, carry):
        r0 = pl.multiple_of(i * NORM_ROWS, NORM_ROWS)
        x = src_ref[pl.ds(r0, NORM_ROWS), :].astype(F32)
        y = x * lax.rsqrt(jnp.mean(x * x, axis=-1, keepdims=True) + EPS)
        dst_ref[pl.ds(r0, NORM_ROWS), :] = (y * g).astype(dst_ref.dtype)
        return carry

    lax.fori_loop(0, rows // NORM_ROWS, body, 0)


def _cast_block(src_ref, dst_ref):
    dst_ref[...] = src_ref[...].astype(dst_ref.dtype)


def _causal_constants(t):
    row = lax.broadcasted_iota(jnp.int32, (t, t), 0)
    col = lax.broadcasted_iota(jnp.int32, (t, t), 1)
    tri = (row >= col).astype(BF16)
    tri_t = (row <= col).astype(BF16)
    neg_mask = jnp.where(row >= col, 0.0, -jnp.inf).astype(F32)
    return tri, tri_t, neg_mask


def _run_interleaved(*streams):
    done = [0] * len(streams)
    total = max(count for _, count in streams)
    for tick in range(1, total + 1):
        for i, (gen, count) in enumerate(streams):
            want = -(-tick * count // total)
            while done[i] < want:
                next(gen, None)
                done[i] += 1
    for gen, _ in streams:
        for _ in gen:
            pass


def _slab_spec(n_rows, n_cols, steps, step_of):
    slab = -(-n_rows // steps)
    slab += -slab % V7X_BF16_ROWS
    last = -(-n_rows // slab) - 1
    return pl.BlockSpec((slab, n_cols), lambda *idx: (jnp.minimum(step_of(*idx), last), 0))


def _ssd_chunk(save_ref, small_ref, convw_ref, convb_ref, dtb_ref, alog_ref, dskip_ref, ng_ref,
               tri_ref, trit_ref, negmask_ref, expand_ref, lanemask_ref,
               xpad_ref, state_ref, row0, y_ref, y_row0):
    T = SSD_CHUNK
    P = SSD_HEAD_DIM
    NS = SSD_STATE
    GW = SSD_GROUP_WIDTH
    TAIL = SSD_TAIL
    rows = pl.ds(row0, T)
    tn = save_ref.shape[2]
    z_block0 = 0
    xbc_block0 = SSD_WIDTH // tn

    for b in range(SSD_CONV_WIDTH // tn):
        xpad_ref[TAIL:TAIL + T, b * tn:(b + 1) * tn] = save_ref[xbc_block0 + b, rows, :].astype(F32)

    def conv_silu(c0, width):
        hist = xpad_ref[:, c0:c0 + width]
        conv = convb_ref[:, c0:c0 + width] + (
            hist[TAIL:TAIL + T, :] * convw_ref[SSD_CONV - 1:SSD_CONV, c0:c0 + width])
        for back in range(1, SSD_CONV):
            tap = SSD_CONV - 1 - back
            shifted = pltpu.roll(hist, back, axis=0)[TAIL:TAIL + T, :]
            conv = conv + shifted * convw_ref[tap:tap + 1, c0:c0 + width]
        return conv * _sigmoid(conv)

    lane = lax.broadcasted_iota(jnp.int32, (T, SMALL_WIDTH), 1)
    dt = jnp.where(lane < SSD_HEADS, _softplus(small_ref[rows, :] + dtb_ref[...]), 0.0)
    a_dt = dt * (-jnp.exp(alog_ref[...]))
    col_l = _dot01_left(tri_ref[...], a_dt) * LOG2E
    dt_t = dt.T
    row_l = (_dot01_right(a_dt.T, trit_ref[...]) - jnp.log(dt_t)) * LOG2E
    e_col = jnp.exp2(col_l)
    last_l = col_l[T - 1:T, :]
    w2 = dt * jnp.exp2(last_l - col_l)
    decay_wide = _dot01_right(
        jnp.broadcast_to(jnp.exp2(last_l), (V7X_SUBLANES, SMALL_WIDTH)), expand_ref[...])[0:1, :]

    neg_mask = negmask_ref[...]
    lane_t = lax.broadcasted_iota(jnp.int32, (T, V7X_LANES), 1)
    first_head = lane_t < P
    mask_a = lanemask_ref[0:1, :]
    mask_b = lanemask_ref[1:2, :]

    bc_act = conv_silu(SSD_WIDTH, 2 * SSD_BC_WIDTH)
    yield
    for g in range(SSD_GROUPS):
        gcols = slice(g * GW, (g + 1) * GW)
        xs_g = conv_silu(g * GW, GW)
        b_f = bc_act[:, g * NS:(g + 1) * NS]
        c_bf = bc_act[:, SSD_BC_WIDTH + g * NS:SSD_BC_WIDTH + (g + 1) * NS].astype(BF16)
        cb = _dot_nt(c_bf, b_f.astype(BF16))
        state_g = state_ref[g]
        y_off = _dot(c_bf, state_g.astype(BF16))

        y_pairs = []
        xd_pairs = []
        for j in range(GW // V7X_LANES):
            ha = g * (GW // P) + 2 * j
            hb = ha + 1
            m_a = (cb * jnp.exp2(col_l[:, ha:ha + 1] - row_l[ha:ha + 1, :] + neg_mask)).astype(BF16)
            m_b = (cb * jnp.exp2(col_l[:, hb:hb + 1] - row_l[hb:hb + 1, :] + neg_mask)).astype(BF16)
            xp = xs_g[:, j * V7X_LANES:(j + 1) * V7X_LANES]
            xp_bf = xp.astype(BF16)
            y_diag = _dot(jnp.concatenate([m_a, m_b], axis=1),
                          jnp.concatenate([xp_bf * mask_a, xp_bf * mask_b], axis=0))
            off_scale = jnp.where(first_head,
                                  jnp.broadcast_to(e_col[:, ha:ha + 1], (T, V7X_LANES)),
                                  jnp.broadcast_to(e_col[:, hb:hb + 1], (T, V7X_LANES)))
            w2_pair = jnp.where(first_head,
                                jnp.broadcast_to(w2[:, ha:ha + 1], (T, V7X_LANES)),
                                jnp.broadcast_to(w2[:, hb:hb + 1], (T, V7X_LANES)))
            y_pairs.append(y_diag + off_scale * y_off[:, j * V7X_LANES:(j + 1) * V7X_LANES])
            xd_pairs.append((xp * w2_pair).astype(BF16))

        y_g = jnp.concatenate(y_pairs, axis=1) + dskip_ref[:, gcols] * xs_g
        xd_g = jnp.concatenate(xd_pairs, axis=1)
        states_g = _dot(b_f.T.astype(BF16), xd_g)
        state_ref[g] = state_g * decay_wide[:, gcols] + states_g

        zb, zc = divmod(g * GW, tn)
        zg = save_ref[z_block0 + zb, rows, zc:zc + GW].astype(F32)
        y_g = y_g * (zg * _sigmoid(zg))
        y_g = y_g * lax.rsqrt(jnp.mean(y_g * y_g, axis=-1, keepdims=True) + EPS)
        y_ref[pl.ds(y_row0, T), gcols] = (y_g * ng_ref[:, gcols]).astype(y_ref.dtype)
        if g == SSD_GROUPS - 1:
            xpad_ref[0:TAIL, :] = xpad_ref[T:T + TAIL, :]
        yield


def _ssd_constants():
    tri, tri_t, neg_mask = _causal_constants(SSD_CHUNK)
    e_row = lax.broadcasted_iota(jnp.int32, (SMALL_WIDTH, SSD_WIDTH), 0)
    e_col = lax.broadcasted_iota(jnp.int32, (SMALL_WIDTH, SSD_WIDTH), 1)
    expand = (e_row == e_col // SSD_HEAD_DIM).astype(BF16)
    m_lane = lax.broadcasted_iota(jnp.int32, (2, V7X_LANES), 1)
    m_row = lax.broadcasted_iota(jnp.int32, (2, V7X_LANES), 0)
    lane_mask = ((m_lane < SSD_HEAD_DIM) == (m_row == 0)).astype(BF16)
    return tri, tri_t, neg_mask, expand, lane_mask


def _project_and_mix(n, u_ref, wt_ref, proj_ref, small_ref, ssd_refs, save_ref, xpad_ref,
                     state_ref, y_ref):
    tn = proj_ref.shape[1]
    in_blocks = IN_WIDE_BEFORE_DT // tn

    def project(keep):
        for j in range(tn // PROJ_PIECE):
            cols = slice(j * PROJ_PIECE, (j + 1) * PROJ_PIECE)
            val = _dot_nt(u_ref[...], wt_ref[cols, :]).astype(proj_ref.dtype)
            proj_ref[:, cols] = val
            if keep:
                save_ref[n, :, cols] = val
            yield

    def mix():
        step_row0 = (n - in_blocks) * (SSD_PER_STEP * SSD_CHUNK)
        for c in range(SSD_PER_STEP):
            row0 = pl.multiple_of(step_row0 + c * SSD_CHUNK, SSD_CHUNK)
            yield from _ssd_chunk(save_ref, small_ref, *ssd_refs, xpad_ref, state_ref,
                                  row0, y_ref, c * SSD_CHUNK)

    @pl.when(n < in_blocks)
    def _():
        for _ in project(True):
            pass

    @pl.when((n >= in_blocks) & (n < in_blocks + SSD_STEPS))
    def _():
        _run_interleaved((project(False), tn // PROJ_PIECE),
                         (mix(), SSD_PER_STEP * (SSD_GROUPS + 1)))

    @pl.when(n >= in_blocks + SSD_STEPS)
    def _():
        for _ in project(False):
            pass


def _in_proj_first_kernel(x_ref, g_ref, w_ref, wnext_ref, wdt_ref, wif_ref, *rest):
    ssd_refs = rest[:SSD_N_CONSTS]
    (proj_ref, small_ref, y_ref, wb_ref, ws_ref, state_out_ref, tail_out_ref,
     u_ref, save_ref, xpad_ref, state_ref) = rest[SSD_N_CONSTS:]
    n = pl.program_id(0)
    gap = wnext_ref.shape[0]

    @pl.when(n == 0)
    def _():
        _rmsnorm_rows(x_ref, g_ref, u_ref, x_ref.shape[0])
        ws_ref[DT_LANE:I_LANE, :] = wdt_ref[...].astype(BF16)
        ws_ref[I_LANE:F_LANE + MLSTM_HEADS, :] = wif_ref[...].astype(BF16)
        ws_ref[F_LANE + MLSTM_HEADS:, :] = jnp.zeros(
            (SMALL_WIDTH - F_LANE - MLSTM_HEADS, ws_ref.shape[1]), BF16)
        small_ref[...] = _dot_nt(u_ref[...], ws_ref[...])
        xpad_ref[0:SSD_TAIL, :] = jnp.zeros((SSD_TAIL, SSD_CONV_WIDTH), F32)
        state_ref[...] = jnp.zeros(state_ref.shape, F32)

    tn = wb_ref.shape[0]
    cast_rows = min(CAST_ROWS, tn)

    @pl.when(n * tn < IN_WIDE_BEFORE_DT)
    def _():
        def body(i, carry):
            r0 = pl.multiple_of(i * cast_rows, cast_rows)
            wb_ref[pl.ds(r0, cast_rows), :] = w_ref[pl.ds(r0, cast_rows), :].astype(BF16)
            return carry
        lax.fori_loop(0, tn // cast_rows, body, 0)

    @pl.when(n * tn >= IN_WIDE_BEFORE_DT)
    def _():
        def body(i, carry):
            r0 = pl.multiple_of(i * gap, gap)
            wb_ref[pl.ds(r0, gap), :] = w_ref[pl.ds(r0 + gap, gap), :].astype(BF16)
            return carry
        lax.fori_loop(0, tn // gap - 1, body, 0)
        wb_ref[tn - gap:, :] = wnext_ref[...].astype(BF16)

    _project_and_mix(n, u_ref, wb_ref, proj_ref, small_ref, ssd_refs, save_ref, xpad_ref,
                     state_ref, y_ref)

    @pl.when(n == pl.num_programs(0) - 1)
    def _():
        state_out_ref[...] = state_ref[...]
        tail_out_ref[...] = xpad_ref[0:SSD_TAIL, :]


def _in_proj_rest_kernel(n_cast, x_ref, g_ref, w_ref, ws_ref, *rest):
    k = SSD_N_CONSTS
    ssd_refs = rest[:k]
    state_in_ref, tail_in_ref = rest[k:k + 2]
    cast_in = rest[k + 5:k + 5 + n_cast]
    proj_ref, small_ref, y_ref = rest[k + 5 + n_cast:k + 8 + n_cast]
    cast_out = rest[k + 8 + n_cast:k + 8 + 2 * n_cast]
    u_ref, save_ref, xpad_ref, state_ref = rest[k + 8 + 2 * n_cast:]
    n = pl.program_id(1)

    for src_ref, dst_ref in zip(cast_in, cast_out):
        _cast_block(src_ref, dst_ref)

    @pl.when((pl.program_id(0) == 0) & (n == 0))
    def _():
        xpad_ref[0:SSD_TAIL, :] = tail_in_ref[...]
        state_ref[...] = state_in_ref[...]

    @pl.when(n == 0)
    def _():
        _rmsnorm_rows(x_ref, g_ref, u_ref, x_ref.shape[0])
        small_ref[...] = _dot_nt(u_ref[...], ws_ref[...])

    _project_and_mix(n, u_ref, w_ref, proj_ref, small_ref, ssd_refs, save_ref, xpad_ref,
                     state_ref, y_ref)


def _in_proj(x, g, w_in_t, ssd_params, later_weights):
    seq, d = x.shape
    n_blocks = IN_WIDE // PROJ_TN
    tn1 = PROJ_FIRST_TN
    assert SSD_IN_BLOCKS + SSD_STEPS <= n_blocks
    gap = SSD_HEADS
    if_rows = 2 * MLSTM_HEADS
    y_rows = SSD_PER_STEP * SSD_CHUNK
    y_step = lambda n, tn: jnp.clip(n - IN_WIDE_BEFORE_DT // tn, 0, SSD_STEPS - 1)
    ssd_consts = (*ssd_params, *_ssd_constants())
    assert len(ssd_consts) == SSD_N_CONSTS
    state_shape = (SSD_GROUPS, SSD_STATE, SSD_GROUP_WIDTH)
    tail_shape = (SSD_TAIL, SSD_CONV_WIDTH)
    scratch = lambda tn: [
        pltpu.VMEM((PROJ_TM, d), BF16),
        pltpu.VMEM((IN_WIDE_BEFORE_DT // tn, PROJ_TM, tn), BF16),
        pltpu.VMEM((SSD_TAIL + SSD_CHUNK, SSD_CONV_WIDTH), F32),
        pltpu.VMEM(state_shape, F32),
    ]
    full1 = lambda a: pl.BlockSpec(a.shape, lambda n: (0,) * a.ndim)
    proj, small, y_ssd, w_wide, w_small, state, tail = pl.pallas_call(
        _in_proj_first_kernel,
        grid=(IN_WIDE // tn1,),
        in_specs=[
            pl.BlockSpec((PROJ_TM, d), lambda n: (0, 0), pipeline_mode=pl.Buffered(1)),
            pl.BlockSpec((1, d), lambda n: (0, 0)),
            pl.BlockSpec((tn1, d), lambda n: (n, 0)),
            pl.BlockSpec((gap, d), lambda n: ((n + 1) * (tn1 // gap), 0)),
            pl.BlockSpec((gap, d), lambda n: (IN_WIDE_BEFORE_DT // gap, 0)),
            pl.BlockSpec((if_rows, d), lambda n: (IN_I_COL // if_rows, 0)),
            *[full1(a) for a in ssd_consts],
        ],
        out_specs=[
            pl.BlockSpec((PROJ_TM, tn1), lambda n: (0, n)),
            pl.BlockSpec((PROJ_TM, SMALL_WIDTH), lambda n: (0, 0)),
            pl.BlockSpec((y_rows, SSD_WIDTH), lambda n: (y_step(n, tn1), 0)),
            pl.BlockSpec((tn1, d), lambda n: (n, 0)),
            pl.BlockSpec((SMALL_WIDTH, d), lambda n: (0, 0)),
            pl.BlockSpec(state_shape, lambda n: (0, 0, 0)),
            pl.BlockSpec(tail_shape, lambda n: (0, 0)),
        ],
        out_shape=[
            jax.ShapeDtypeStruct((seq, IN_WIDE), BF16),
            jax.ShapeDtypeStruct((seq, SMALL_WIDTH), F32),
            jax.ShapeDtypeStruct((seq, SSD_WIDTH), BF16),
            jax.ShapeDtypeStruct((IN_WIDE, d), BF16),
            jax.ShapeDtypeStruct((SMALL_WIDTH, d), BF16),
            jax.ShapeDtypeStruct(state_shape, F32),
            jax.ShapeDtypeStruct(tail_shape, F32),
        ],
        scratch_shapes=scratch(tn1),
        compiler_params=pltpu.CompilerParams(
            dimension_semantics=("arbitrary",),
            vmem_limit_bytes=VMEM_LIMIT_BYTES),
        name="in_proj_first",
    )(x, g, w_in_t, w_in_t, w_in_t, w_in_t, *ssd_consts)

    m_tiles = seq // PROJ_TM - 1
    steps = m_tiles * n_blocks
    cast_specs = [_slab_spec(w.shape[0], w.shape[1], steps, lambda m, n: m * n_blocks + n)
                  for w in later_weights]
    full2 = lambda a: pl.BlockSpec(a.shape, lambda m, n: (0,) * a.ndim)
    n_in_before_alias = 4 + len(ssd_consts) + 2
    outs = pl.pallas_call(
        functools.partial(_in_proj_rest_kernel, len(later_weights)),
        grid=(m_tiles, n_blocks),
        in_specs=[
            pl.BlockSpec((PROJ_TM, d), lambda m, n: (m + 1, 0), pipeline_mode=pl.Buffered(1)),
            pl.BlockSpec((1, d), lambda m, n: (0, 0)),
            pl.BlockSpec((PROJ_TN, d), lambda m, n: (n, 0)),
            pl.BlockSpec((SMALL_WIDTH, d), lambda m, n: (0, 0)),
            *[full2(a) for a in ssd_consts],
            full2(state),
            full2(tail),
            pl.BlockSpec(memory_space=pl.ANY),
            pl.BlockSpec(memory_space=pl.ANY),
            pl.BlockSpec(memory_space=pl.ANY),
            *cast_specs,
        ],
        out_specs=[
            pl.BlockSpec((PROJ_TM, PROJ_TN), lambda m, n: (m + 1, n)),
            pl.BlockSpec((PROJ_TM, SMALL_WIDTH), lambda m, n: (m + 1, 0)),
            pl.BlockSpec((y_rows, SSD_WIDTH),
                         lambda m, n: ((m + 1) * SSD_STEPS + y_step(n, PROJ_TN), 0)),
            *cast_specs,
        ],
        out_shape=[
            jax.ShapeDtypeStruct((seq, IN_WIDE), BF16),
            jax.ShapeDtypeStruct((seq, SMALL_WIDTH), F32),
            jax.ShapeDtypeStruct((seq, SSD_WIDTH), BF16),
            *[jax.ShapeDtypeStruct(w.shape, BF16) for w in later_weights],
        ],
        scratch_shapes=scratch(PROJ_TN),
        input_output_aliases={n_in_before_alias: 0, n_in_before_alias + 1: 1,
                              n_in_before_alias + 2: 2},
        compiler_params=pltpu.CompilerParams(
            dimension_semantics=("arbitrary", "arbitrary"),
            vmem_limit_bytes=VMEM_LIMIT_BYTES),
        name="in_proj_rest",
    )(x, g, w_wide, w_small, *ssd_consts, state, tail, proj, small, y_ssd, *later_weights)
    return outs[0], outs[1], outs[2], outs[3:]


def _cummax_rows(x):
    rows = x.shape[0]
    row = lax.broadcasted_iota(jnp.int32, x.shape, 0)
    d = 1
    while d < rows:
        if d < V7X_SUBLANES:
            shifted = jnp.where(row < d, -jnp.inf, pltpu.roll(x, d, axis=0))
        else:
            shifted = jnp.concatenate(
                [jnp.full((d, x.shape[1]), -jnp.inf, x.dtype), x[:rows - d]], axis=0)
        x = jnp.maximum(x, shifted)
        d *= 2
    return x


def _mlstm_chunk(q_ref, k_ref, v0_ref, v1_ref, o0_ref, o1_ref, small_ref, ib_ref, fb_ref,
                 ng_ref, tri_ref, negmask_ref, ct_ref, nmat_ref, m_ref, h_ref, h_row0):
    T = MLSTM_SUB
    DK = MLSTM_QK_DIM
    DV = MLSTM_V_DIM
    log2_scale = math.log2(DK ** -0.5)
    half = MLSTM_HEADS // 2

    neg_mask = negmask_ref[...]
    half_ng = 0.5 * ng_ref[...]
    lane = lax.broadcasted_iota(jnp.int32, (T, SMALL_WIDTH), 1)
    head_lane = (lane >= F_LANE) & (lane < F_LANE + MLSTM_HEADS)
    lane_dk = lax.broadcasted_iota(jnp.int32, (DK, SMALL_WIDTH), 1)

    for sub in range(MLSTM_CHUNK // T):
        r0 = sub * T
        sm = small_ref[r0:r0 + T, :]
        log_i = pltpu.roll(_softcap(sm + ib_ref[...]) * LOG2E, F_LANE - I_LANE, axis=1)
        log_i = jnp.where(head_lane, log_i, 0.0)
        log_f = jnp.where(head_lane, _log_sigmoid(_softcap(sm + fb_ref[...])) * LOG2E, 0.0)
        bcum = _dot01_left(tri_ref[...], log_f)
        r_c = log_i - bcum
        m_prev = m_ref[0:1, :]
        inter_log = bcum + m_prev
        m_t = jnp.maximum(inter_log, bcum + _cummax_rows(r_c))
        col_term = bcum - m_t
        w_inter = jnp.exp2(inter_log - m_t)
        e_neg_m = jnp.exp2(-m_t)
        b_last = bcum[T - 1:T, :]
        a_c = b_last + r_c
        m_loc = jnp.max(a_c, axis=0, keepdims=True)
        m_new = jnp.maximum(b_last + m_prev, m_loc)
        s_old = jnp.exp2(b_last + m_prev - m_new)
        s_new = jnp.exp2(m_loc - m_new)
        w_rows = (jnp.exp2(a_c - m_loc + log2_scale) * s_new).T
        row_term = (r_c + log2_scale).T
        q_all = q_ref[r0:r0 + T, :]
        nq_inter = _dot(q_all, nmat_ref[...].astype(BF16))
        yield

        for h in range(MLSTM_HEADS):
            v_ref, o_ref = (v0_ref, o0_ref) if h < half else (v1_ref, o1_ref)
            hv = h % half
            hl = F_LANE + h
            q = q_all[:, h * DK:(h + 1) * DK]
            k = k_ref[r0:r0 + T, h * DK:(h + 1) * DK]
            v = v_ref[r0:r0 + T, hv * DV:(hv + 1) * DV]
            ct = ct_ref[h]

            kw = k.astype(F32).T * w_rows[hl:hl + 1, :]
            c_loc = _dot(kw.astype(BF16), v)
            n_loc = jnp.sum(kw, axis=1, keepdims=True)

            p = _dot_nt(q, k) * jnp.exp2(
                col_term[:, hl:hl + 1] + row_term[hl:hl + 1, :] + neg_mask)
            wi = w_inter[:, hl:hl + 1]
            num = _dot(p.astype(BF16), v) + wi * _dot(q, ct.astype(BF16))
            nq = jnp.sum(p, axis=1, keepdims=True) + wi * nq_inter[:, hl:hl + 1]
            inv_den = 1.0 / jnp.maximum(jnp.abs(nq), e_neg_m[:, hl:hl + 1])
            ms = jnp.mean(num * num, axis=-1, keepdims=True)
            out_scale = inv_den * lax.rsqrt(inv_den * inv_den * ms + EPS)
            hng = half_ng[:, h * DV:(h + 1) * DV]
            og = o_ref[r0:r0 + T, hv * DV:(hv + 1) * DV].astype(F32)
            gate = jnp.tanh(0.5 * og) * hng + hng
            h_ref[pl.ds(h_row0 + r0, T), h * DV:(h + 1) * DV] = (
                num * out_scale * gate).astype(h_ref.dtype)

            so = s_old[:, hl:hl + 1]
            ct_ref[h] = so * ct + c_loc
            n_rows = nmat_ref[h * DK:(h + 1) * DK, :]
            nmat_ref[h * DK:(h + 1) * DK, :] = so * n_rows + jnp.where(lane_dk == hl, n_loc, 0.0)
            yield

        m_ref[...] = jnp.broadcast_to(m_new, m_ref.shape)


def _mlstm_out_kernel(x_ref, ys_ref, wt_ref, wb_ref,
                      q_ref, k_ref, v0_ref, v1_ref, o0_ref, o1_ref, small_ref, ib_ref, fb_ref,
                      ng_ref, tri_ref, negmask_ref, wd_ref,
                      h1_ref, wd_bf_ref,
                      yml_even_ref, yml_odd_ref, ct_ref, nmat_ref, m_ref):
    m = pl.program_id(0)
    n = pl.program_id(1)
    last = pl.num_programs(0) - 1

    _cast_block(wd_ref, wd_bf_ref)

    @pl.when((m == 0) & (n == 0))
    def _():
        ct_ref[...] = jnp.zeros(ct_ref.shape, F32)
        nmat_ref[...] = jnp.zeros(nmat_ref.shape, F32)
        m_ref[...] = jnp.zeros(m_ref.shape, F32)

    def project(yml_ref):
        col_w = OUT_TN // 2
        k_w = 2 * SSD_WIDTH // (OUT_PIECES // 2)
        for piece in range(OUT_PIECES):
            cols = slice((piece % 2) * col_w, (piece % 2 + 1) * col_w)
            k0 = (piece // 2) * k_w
            src, w_ref = (ys_ref, wt_ref) if k0 < SSD_WIDTH else (yml_ref, wb_ref)
            k0 %= SSD_WIDTH
            part = _dot(src[:, k0:k0 + k_w], w_ref[k0:k0 + k_w, cols])
            if piece < 2:
                h1_ref[:, cols] = x_ref[:, cols] + part
            else:
                h1_ref[:, cols] += part
            yield

    def mix(yml_ref):
        return _mlstm_chunk(q_ref, k_ref, v0_ref, v1_ref, o0_ref, o1_ref, small_ref, ib_ref,
                            fb_ref, ng_ref, tri_ref, negmask_ref, ct_ref, nmat_ref, m_ref,
                            yml_ref, pl.multiple_of(n * MLSTM_CHUNK, MLSTM_CHUNK))

    seg_total = (MLSTM_CHUNK // MLSTM_SUB) * (MLSTM_HEADS + 1)

    @pl.when(m == 0)
    def _():
        for _ in mix(yml_even_ref):
            pass

    @pl.when((m > 0) & (m < last) & (m % 2 == 1))
    def _():
        _run_interleaved((project(yml_even_ref), OUT_PIECES), (mix(yml_odd_ref), seg_total))

    @pl.when((m > 0) & (m < last) & (m % 2 == 0))
    def _():
        _run_interleaved((project(yml_odd_ref), OUT_PIECES), (mix(yml_even_ref), seg_total))

    @pl.when(m == last)
    def _():
        for _ in project(yml_odd_ref):
            pass


def _mlstm_out(x, y_ssd, w_out_bf, proj, small, ib_row, fb_row, norm_g_row, w_down):
    seq, d = x.shape
    T = MLSTM_CHUNK
    tiles = seq // OUT_TM
    n_tiles = d // OUT_TN
    assert tiles % 2 == 0 and OUT_TM == n_tiles * T
    steps = (tiles + 1) * n_tiles
    half_w = MLSTM_WIDTH // 2
    q_blk = IN_WIDE_BEFORE_DT // MLSTM_QK_WIDTH
    v_blk = (IN_WIDE_BEFORE_DT + 2 * MLSTM_QK_WIDTH) // half_w
    tri, _, neg_mask = _causal_constants(MLSTM_SUB)
    consts = (ib_row, fb_row, norm_g_row, tri, neg_mask)
    full = lambda a: pl.BlockSpec(a.shape, lambda m, n: (0, 0))
    prev = lambda m: jnp.maximum(m - 1, 0)
    chunk = lambda m, n: jnp.minimum(m * n_tiles + n, seq // T - 1)
    step_of = lambda m, n: m * n_tiles + n
    cast_specs = [_slab_spec(w_down.shape[0], w_down.shape[1], steps, step_of)]
    return pl.pallas_call(
        _mlstm_out_kernel,
        grid=(tiles + 1, n_tiles),
        in_specs=[
            pl.BlockSpec((OUT_TM, OUT_TN), lambda m, n: (prev(m), n)),
            pl.BlockSpec((OUT_TM, SSD_WIDTH), lambda m, n: (prev(m), 0)),
            pl.BlockSpec((SSD_WIDTH, OUT_TN), lambda m, n: (0, n)),
            pl.BlockSpec((MLSTM_WIDTH, OUT_TN), lambda m, n: (1, n)),
            pl.BlockSpec((T, MLSTM_QK_WIDTH), lambda m, n: (chunk(m, n), q_blk)),
            pl.BlockSpec((T, MLSTM_QK_WIDTH), lambda m, n: (chunk(m, n), q_blk + 1)),
            pl.BlockSpec((T, half_w), lambda m, n: (chunk(m, n), v_blk)),
            pl.BlockSpec((T, half_w), lambda m, n: (chunk(m, n), v_blk + 1)),
            pl.BlockSpec((T, half_w), lambda m, n: (chunk(m, n), v_blk + 2)),
            pl.BlockSpec((T, half_w), lambda m, n: (chunk(m, n), v_blk + 3)),
            pl.BlockSpec((T, SMALL_WIDTH), lambda m, n: (chunk(m, n), 0)),
            *[full(a) for a in consts],
            *cast_specs,
        ],
        out_specs=[
            pl.BlockSpec((OUT_TM, OUT_TN), lambda m, n: (prev(m), jnp.where(m == 0, 0, n))),
            *cast_specs,
        ],
        out_shape=[
            jax.ShapeDtypeStruct((seq, d), F32),
            jax.ShapeDtypeStruct(w_down.shape, BF16),
        ],
        scratch_shapes=[
            pltpu.VMEM((OUT_TM, MLSTM_WIDTH), BF16),
            pltpu.VMEM((OUT_TM, MLSTM_WIDTH), BF16),
            pltpu.VMEM((MLSTM_HEADS, MLSTM_QK_DIM, MLSTM_V_DIM), F32),
            pltpu.VMEM((MLSTM_QK_WIDTH, SMALL_WIDTH), F32),
            pltpu.VMEM((V7X_SUBLANES, SMALL_WIDTH), F32),
        ],
        compiler_params=pltpu.CompilerParams(
            dimension_semantics=("arbitrary", "arbitrary"),
            vmem_limit_bytes=VMEM_LIMIT_BYTES),
        name="mlstm_out",
    )(x, y_ssd, w_out_bf, w_out_bf, proj, proj, proj, proj, proj, proj, small, *consts, w_down)


def _ffn_kernel(h_ref, g_ref, wg_ref, wu_ref, wd_ref, gf_ref, out_ref, u_ref):
    f = pl.program_id(1)

    @pl.when(f == 0)
    def _():
        _rmsnorm_rows(h_ref, g_ref, u_ref, h_ref.shape[0])
        out_ref[...] = h_ref[...]

    u = u_ref[...]
    sub = wg_ref.shape[1] // FFN_SPLIT
    acc = None
    for s in range(FFN_SPLIT):
        gate = _dot(u, wg_ref[:, s * sub:(s + 1) * sub])
        up = _dot(u, wu_ref[:, s * sub:(s + 1) * sub])
        a = (gate * _sigmoid(gate) * up).astype(BF16)
        part = _dot(a, wd_ref[s * sub:(s + 1) * sub, :])
        acc = part if acc is None else acc + part
    out_ref[...] += acc

    @pl.when(f == pl.num_programs(1) - 1)
    def _():
        _rmsnorm_rows(out_ref, gf_ref, out_ref, out_ref.shape[0])


def _ffn(h1, g, w_gate, w_up, w_down, g_final):
    seq, d = h1.shape
    d_ff = w_gate.shape[1]
    grid = (seq // FFN_TM, d_ff // FFN_TF)
    return pl.pallas_call(
        _ffn_kernel,
        grid=grid,
        in_specs=[
            pl.BlockSpec((FFN_TM, d), lambda m, f: (m, 0)),
            pl.BlockSpec((1, d), lambda m, f: (0, 0)),
            pl.BlockSpec((d, FFN_TF), lambda m, f: (0, f)),
            pl.BlockSpec((d, FFN_TF), lambda m, f: (0, f)),
            pl.BlockSpec((FFN_TF, d), lambda m, f: (f, 0)),
            pl.BlockSpec((1, d), lambda m, f: (0, 0)),
        ],
        out_specs=pl.BlockSpec((FFN_TM, d), lambda m, f: (m, 0)),
        out_shape=jax.ShapeDtypeStruct((seq, d), F32),
        scratch_shapes=[pltpu.VMEM((FFN_TM, d), BF16)],
        compiler_params=pltpu.CompilerParams(
            dimension_semantics=("arbitrary", "arbitrary"),
            vmem_limit_bytes=VMEM_LIMIT_BYTES),
        name="ffn",
    )(h1, g, w_gate, w_up, w_down, g_final)


def _pad_lanes(vec, lane0):
    return jnp.zeros((1, SMALL_WIDTH), F32).at[0, lane0:lane0 + vec.shape[0]].set(vec.astype(F32))


def _layer(h, norm_mix_g, w_in, conv_w, conv_b, dt_bias, a_log, d_skip, ssd_norm_g,
           i_bias, f_bias, mlstm_norm_g, w_out, norm_ffn_g, w_gate, w_up, w_down, out_g):
    d = h.shape[1]
    ssd_params = (conv_w.astype(F32), conv_b.reshape(1, -1).astype(F32),
                  _pad_lanes(dt_bias, DT_LANE), _pad_lanes(a_log, DT_LANE),
                  jnp.repeat(d_skip.astype(F32), SSD_HEAD_DIM).reshape(1, -1),
                  ssd_norm_g.reshape(1, -1).astype(F32))
    proj, small, y_ssd, (w_out_bf, wg_bf, wu_bf) = _in_proj(
        h, norm_mix_g.reshape(1, d), w_in.T, ssd_params, (w_out, w_gate, w_up))

    h1, wd_bf = _mlstm_out(
        h, y_ssd, w_out_bf, proj, small, _pad_lanes(i_bias, I_LANE), _pad_lanes(f_bias, F_LANE),
        mlstm_norm_g.reshape(1, -1).astype(F32), w_down)
    return _ffn(h1, norm_ffn_g.reshape(1, d), wg_bf, wu_bf, wd_bf, out_g.reshape(1, d))


def kernel(x, norm_mix_g, w_in, conv_w, conv_b, dt_bias, a_log, d_skip, ssd_norm_g, i_bias,
           f_bias, mlstm_norm_g, w_out, norm_ffn_g, w_gate, w_up, w_down, final_norm_g):
    batch, seq, d = x.shape
    depth = w_in.shape[0]
    assert batch == 1 and depth == 1, "single-sequence, single-layer problem"
    out = _layer(x.reshape(seq, d), norm_mix_g[0], w_in[0], conv_w[0], conv_b[0], dt_bias[0],
                 a_log[0], d_skip[0], ssd_norm_g[0], i_bias[0], f_bias[0], mlstm_norm_g[0],
                 w_out[0], norm_ffn_g[0], w_gate[0], w_up[0], w_down[0], final_norm_g)
    return out.reshape(batch, seq, d)
```

```python
import functools
import math

import jax
import jax.numpy as jnp
from jax import lax
from jax.experimental import pallas as pl
from jax.experimental.pallas import tpu as pltpu

F32 = jnp.float32
BF16 = jnp.bfloat16
EPS = 1e-6
LOG2E = math.log2(math.e)

V7X_LANES = 128
V7X_SUBLANES = 8
V7X_BF16_ROWS = 16
VMEM_LIMIT_BYTES = 56 * 1024 * 1024

SSD_HEADS = 32
SSD_HEAD_DIM = 64
SSD_WIDTH = SSD_HEADS * SSD_HEAD_DIM
SSD_GROUPS = 4
SSD_STATE = 128
SSD_CONV = 4
SSD_GROUP_WIDTH = SSD_WIDTH // SSD_GROUPS
SSD_BC_WIDTH = SSD_GROUPS * SSD_STATE
SSD_CONV_WIDTH = SSD_WIDTH + 2 * SSD_BC_WIDTH
MLSTM_HEADS = 8
MLSTM_QK_DIM = 128
MLSTM_V_DIM = 256
MLSTM_WIDTH = MLSTM_HEADS * MLSTM_V_DIM
MLSTM_QK_WIDTH = MLSTM_HEADS * MLSTM_QK_DIM
GATE_SOFTCAP = 15.0

IN_WIDE_BEFORE_DT = SSD_WIDTH + SSD_CONV_WIDTH
IN_WIDE = IN_WIDE_BEFORE_DT + 2 * MLSTM_QK_WIDTH + 2 * MLSTM_WIDTH
IN_I_COL = IN_WIDE + SSD_HEADS
DT_LANE = 0
I_LANE = SSD_HEADS
F_LANE = SSD_HEADS + MLSTM_HEADS
SMALL_WIDTH = V7X_LANES

SSD_CHUNK = 128
SSD_PER_STEP = 2
SSD_TAIL = 8
SSD_N_CONSTS = 11
MLSTM_CHUNK = 256
MLSTM_SUB = 128
PROJ_TM = 1024
PROJ_TN = 1024
PROJ_FIRST_TN = 512
PROJ_PIECE = 256
OUT_TM = 1024
OUT_TN = 512
OUT_PIECES = 16
FFN_TM = 512
FFN_TF = 512
FFN_SPLIT = 2
NORM_ROWS = 128
CAST_ROWS = 256

SSD_IN_BLOCKS = IN_WIDE_BEFORE_DT // PROJ_TN
SSD_STEPS = PROJ_TM // (SSD_PER_STEP * SSD_CHUNK)


def _dot(a, b):
    return jnp.dot(a, b, preferred_element_type=F32)


def _dot_nt(a, b):
    return lax.dot_general(a, b, (((1,), (1,)), ((), ())), preferred_element_type=F32)


def _split3(x):
    hi = x.astype(BF16)
    r1 = x - hi.astype(F32)
    mid = r1.astype(BF16)
    lo = (r1 - mid.astype(F32)).astype(BF16)
    return hi, mid, lo


def _dot01_left(m01, x):
    hi, mid, lo = _split3(x)
    return _dot(m01, hi) + _dot(m01, mid) + _dot(m01, lo)


def _dot01_right(x, m01):
    hi, mid, lo = _split3(x)
    return _dot(hi, m01) + _dot(mid, m01) + _dot(lo, m01)


def _sigmoid(x):
    return 0.5 * jnp.tanh(0.5 * x) + 0.5


def _softplus(x):
    return jnp.maximum(x, 0.0) + jnp.log1p(jnp.exp(-jnp.abs(x)))


def _log_sigmoid(x):
    return jnp.minimum(x, 0.0) - jnp.log1p(jnp.exp(-jnp.abs(x)))


def _softcap(x):
    return GATE_SOFTCAP * jnp.tanh(x / GATE_SOFTCAP)


def _rmsnorm_rows(src_ref, g_ref, dst_ref, rows):
    g = g_ref[...]

    def body(i, carry):
        r0 = pl.multiple_of(i * NORM_ROWS, NORM_ROWS)
        x = src_ref[pl.ds(r0, NORM_ROWS), :].astype(F32)
        y = x * lax.rsqrt(jnp.mean(x * x, axis=-1, keepdims=True) + EPS)
        dst_ref[pl.ds(r0, NORM_ROWS), :] = (y * g).astype(dst_ref.dtype)
        return carry

    lax.fori_loop(0, rows // NORM_ROWS, body, 0)


def _cast_block(src_ref, dst_ref):
    dst_ref[...] = src_ref[...].astype(dst_ref.dtype)


def _causal_constants(t):
    row = lax.broadcasted_iota(jnp.int32, (t, t), 0)
    col = lax.broadcasted_iota(jnp.int32, (t, t), 1)
    tri = (row >= col).astype(BF16)
    tri_t = (row <= col).astype(BF16)
    neg_mask = jnp.where(row >= col, 0.0, -jnp.inf).astype(F32)
    return tri, tri_t, neg_mask


def _run_interleaved(*streams, ticks=None):
    done = [0] * len(streams)
    total = ticks or max(count for _, count in streams)
    for tick in range(1, total + 1):
        for i, (gen, count) in enumerate(streams):
            want = -(-tick * count // total)
            while done[i] < want:
                next(gen, None)
                done[i] += 1
    for gen, _ in streams:
        for _ in gen:
            pass


def _slab_spec(n_rows, n_cols, steps, step_of):
    slab = -(-n_rows // steps)
    slab += -slab % V7X_BF16_ROWS
    last = -(-n_rows // slab) - 1
    return pl.BlockSpec((slab, n_cols), lambda *idx: (jnp.minimum(step_of(*idx), last), 0))


def _ssd_chunk(save_ref, small_ref, convw_ref, convb_ref, dtb_ref, alog_ref, dskip_ref, ng_ref,
               tri_ref, trit_ref, negmask_ref, expand_ref, lanemask_ref,
               xpad_ref, state_ref, row0, y_ref, y_row0):
    T = SSD_CHUNK
    P = SSD_HEAD_DIM
    NS = SSD_STATE
    GW = SSD_GROUP_WIDTH
    TAIL = SSD_TAIL
    rows = pl.ds(row0, T)
    tn = save_ref.shape[2]
    z_block0 = 0
    xbc_block0 = SSD_WIDTH // tn

    for b in range(SSD_CONV_WIDTH // tn):
        xpad_ref[TAIL:TAIL + T, b * tn:(b + 1) * tn] = save_ref[xbc_block0 + b, rows, :].astype(F32)

    def conv_silu(c0, width):
        hist = xpad_ref[:, c0:c0 + width]
        conv = convb_ref[:, c0:c0 + width] + (
            hist[TAIL:TAIL + T, :] * convw_ref[SSD_CONV - 1:SSD_CONV, c0:c0 + width])
        for back in range(1, SSD_CONV):
            tap = SSD_CONV - 1 - back
            shifted = pltpu.roll(hist, back, axis=0)[TAIL:TAIL + T, :]
            conv = conv + shifted * convw_ref[tap:tap + 1, c0:c0 + width]
        return conv * _sigmoid(conv)

    lane = lax.broadcasted_iota(jnp.int32, (T, SMALL_WIDTH), 1)
    dt = jnp.where(lane < SSD_HEADS, _softplus(small_ref[rows, :] + dtb_ref[...]), 0.0)
    a_dt = dt * (-jnp.exp(alog_ref[...]))
    col_l = _dot01_left(tri_ref[...], a_dt) * LOG2E
    dt_t = dt.T
    row_l = (_dot01_right(a_dt.T, trit_ref[...]) - jnp.log(dt_t)) * LOG2E
    e_col = jnp.exp2(col_l)
    last_l = col_l[T - 1:T, :]
    w2 = dt * jnp.exp2(last_l - col_l)
    decay_wide = _dot01_right(
        jnp.broadcast_to(jnp.exp2(last_l), (V7X_SUBLANES, SMALL_WIDTH)), expand_ref[...])[0:1, :]

    neg_mask = negmask_ref[...]
    lane_t = lax.broadcasted_iota(jnp.int32, (T, V7X_LANES), 1)
    first_head = lane_t < P
    mask_a = lanemask_ref[0:1, :]
    mask_b = lanemask_ref[1:2, :]

    bc_act = conv_silu(SSD_WIDTH, 2 * SSD_BC_WIDTH)
    yield
    for g in range(SSD_GROUPS):
        gcols = slice(g * GW, (g + 1) * GW)
        xs_g = conv_silu(g * GW, GW)
        b_f = bc_act[:, g * NS:(g + 1) * NS]
        c_bf = bc_act[:, SSD_BC_WIDTH + g * NS:SSD_BC_WIDTH + (g + 1) * NS].astype(BF16)
        cb = _dot_nt(c_bf, b_f.astype(BF16))
        state_g = state_ref[g]
        y_off = _dot(c_bf, state_g.astype(BF16))

        y_pairs = []
        xd_pairs = []
        for j in range(GW // V7X_LANES):
            ha = g * (GW // P) + 2 * j
            hb = ha + 1
            m_a = (cb * jnp.exp2(col_l[:, ha:ha + 1] - row_l[ha:ha + 1, :] + neg_mask)).astype(BF16)
            m_b = (cb * jnp.exp2(col_l[:, hb:hb + 1] - row_l[hb:hb + 1, :] + neg_mask)).astype(BF16)
            xp = xs_g[:, j * V7X_LANES:(j + 1) * V7X_LANES]
            xp_bf = xp.astype(BF16)
            y_diag = _dot(jnp.concatenate([m_a, m_b], axis=1),
                          jnp.concatenate([xp_bf * mask_a, xp_bf * mask_b], axis=0))
            off_scale = jnp.where(first_head,
                                  jnp.broadcast_to(e_col[:, ha:ha + 1], (T, V7X_LANES)),
                                  jnp.broadcast_to(e_col[:, hb:hb + 1], (T, V7X_LANES)))
            w2_pair = jnp.where(first_head,
                                jnp.broadcast_to(w2[:, ha:ha + 1], (T, V7X_LANES)),
                                jnp.broadcast_to(w2[:, hb:hb + 1], (T, V7X_LANES)))
            y_pairs.append(y_diag + off_scale * y_off[:, j * V7X_LANES:(j + 1) * V7X_LANES])
            xd_pairs.append((xp * w2_pair).astype(BF16))

        y_g = jnp.concatenate(y_pairs, axis=1) + dskip_ref[:, gcols] * xs_g
        xd_g = jnp.concatenate(xd_pairs, axis=1)
        states_g = _dot(b_f.T.astype(BF16), xd_g)
        state_ref[g] = state_g * decay_wide[:, gcols] + states_g

        zb, zc = divmod(g * GW, tn)
        zg = save_ref[z_block0 + zb, rows, zc:zc + GW].astype(F32)
        y_g = y_g * (zg * _sigmoid(zg))
        y_g = y_g * lax.rsqrt(jnp.mean(y_g * y_g, axis=-1, keepdims=True) + EPS)
        y_ref[pl.ds(y_row0, T), gcols] = (y_g * ng_ref[:, gcols]).astype(y_ref.dtype)
        if g == SSD_GROUPS - 1:
            xpad_ref[0:TAIL, :] = xpad_ref[T:T + TAIL, :]
        yield


def _ssd_constants():
    tri, tri_t, neg_mask = _causal_constants(SSD_CHUNK)
    e_row = lax.broadcasted_iota(jnp.int32, (SMALL_WIDTH, SSD_WIDTH), 0)
    e_col = lax.broadcasted_iota(jnp.int32, (SMALL_WIDTH, SSD_WIDTH), 1)
    expand = (e_row == e_col // SSD_HEAD_DIM).astype(BF16)
    m_lane = lax.broadcasted_iota(jnp.int32, (2, V7X_LANES), 1)
    m_row = lax.broadcasted_iota(jnp.int32, (2, V7X_LANES), 0)
    lane_mask = ((m_lane < SSD_HEAD_DIM) == (m_row == 0)).astype(BF16)
    return tri, tri_t, neg_mask, expand, lane_mask


def _project_and_mix(n, u_ref, wt_ref, proj_ref, small_ref, ssd_refs, save_ref, xpad_ref,
                     state_ref, y_ref):
    tn = proj_ref.shape[1]
    in_blocks = IN_WIDE_BEFORE_DT // tn

    def project(keep):
        for j in range(tn // PROJ_PIECE):
            cols = slice(j * PROJ_PIECE, (j + 1) * PROJ_PIECE)
            val = _dot_nt(u_ref[...], wt_ref[cols, :]).astype(proj_ref.dtype)
            proj_ref[:, cols] = val
            if keep:
                save_ref[n, :, cols] = val
            yield

    def mix():
        step_row0 = (n - in_blocks) * (SSD_PER_STEP * SSD_CHUNK)
        for c in range(SSD_PER_STEP):
            row0 = pl.multiple_of(step_row0 + c * SSD_CHUNK, SSD_CHUNK)
            yield from _ssd_chunk(save_ref, small_ref, *ssd_refs, xpad_ref, state_ref,
                                  row0, y_ref, c * SSD_CHUNK)

    @pl.when(n < in_blocks)
    def _():
        for _ in project(True):
            pass

    @pl.when((n >= in_blocks) & (n < in_blocks + SSD_STEPS))
    def _():
        pieces = tn // PROJ_PIECE
        _run_interleaved((project(False), pieces), (mix(), SSD_PER_STEP * (SSD_GROUPS + 1)),
                         ticks=pieces if pieces < PROJ_TN // PROJ_PIECE else None)

    @pl.when(n >= in_blocks + SSD_STEPS)
    def _():
        for _ in project(False):
            pass


def _in_proj_first_kernel(x_ref, g_ref, w_ref, wnext_ref, wdt_ref, wif_ref, *rest):
    ssd_refs = rest[:SSD_N_CONSTS]
    (proj_ref, small_ref, y_ref, wb_ref, ws_ref, state_out_ref, tail_out_ref,
     u_ref, save_ref, xpad_ref, state_ref) = rest[SSD_N_CONSTS:]
    n = pl.program_id(0)
    gap = wnext_ref.shape[0]

    @pl.when(n == 0)
    def _():
        _rmsnorm_rows(x_ref, g_ref, u_ref, x_ref.shape[0])
        ws_ref[DT_LANE:I_LANE, :] = wdt_ref[...].astype(BF16)
        ws_ref[I_LANE:F_LANE + MLSTM_HEADS, :] = wif_ref[...].astype(BF16)
        ws_ref[F_LANE + MLSTM_HEADS:, :] = jnp.zeros(
            (SMALL_WIDTH - F_LANE - MLSTM_HEADS, ws_ref.shape[1]), BF16)
        small_ref[...] = _dot_nt(u_ref[...], ws_ref[...])
        xpad_ref[0:SSD_TAIL, :] = jnp.zeros((SSD_TAIL, SSD_CONV_WIDTH), F32)
        state_ref[...] = jnp.zeros(state_ref.shape, F32)

    tn = wb_ref.shape[0]
    cast_rows = min(CAST_ROWS, tn)

    @pl.when(n * tn < IN_WIDE_BEFORE_DT)
    def _():
        def body(i, carry):
            r0 = pl.multiple_of(i * cast_rows, cast_rows)
            wb_ref[pl.ds(r0, cast_rows), :] = w_ref[pl.ds(r0, cast_rows), :].astype(BF16)
            return carry
        lax.fori_loop(0, tn // cast_rows, body, 0)

    @pl.when(n * tn >= IN_WIDE_BEFORE_DT)
    def _():
        def body(i, carry):
            r0 = pl.multiple_of(i * gap, gap)
            wb_ref[pl.ds(r0, gap), :] = w_ref[pl.ds(r0 + gap, gap), :].astype(BF16)
            return carry
        lax.fori_loop(0, tn // gap - 1, body, 0)
        wb_ref[tn - gap:, :] = wnext_ref[...].astype(BF16)

    _project_and_mix(n, u_ref, wb_ref, proj_ref, small_ref, ssd_refs, save_ref, xpad_ref,
                     state_ref, y_ref)

    @pl.when(n == pl.num_programs(0) - 1)
    def _():
        state_out_ref[...] = state_ref[...]
        tail_out_ref[...] = xpad_ref[0:SSD_TAIL, :]


def _in_proj_rest_kernel(n_cast, x_ref, g_ref, w_ref, ws_ref, *rest):
    k = SSD_N_CONSTS
    ssd_refs = rest[:k]
    state_in_ref, tail_in_ref = rest[k:k + 2]
    cast_in = rest[k + 5:k + 5 + n_cast]
    proj_ref, small_ref, y_ref = rest[k + 5 + n_cast:k + 8 + n_cast]
    cast_out = rest[k + 8 + n_cast:k + 8 + 2 * n_cast]
    u_ref, save_ref, xpad_ref, state_ref = rest[k + 8 + 2 * n_cast:]
    n = pl.program_id(1)

    for src_ref, dst_ref in zip(cast_in, cast_out):
        _cast_block(src_ref, dst_ref)

    @pl.when((pl.program_id(0) == 0) & (n == 0))
    def _():
        xpad_ref[0:SSD_TAIL, :] = tail_in_ref[...]
        state_ref[...] = state_in_ref[...]

    @pl.when(n == 0)
    def _():
        _rmsnorm_rows(x_ref, g_ref, u_ref, x_ref.shape[0])
        small_ref[...] = _dot_nt(u_ref[...], ws_ref[...])

    _project_and_mix(n, u_ref, w_ref, proj_ref, small_ref, ssd_refs, save_ref, xpad_ref,
                     state_ref, y_ref)


def _in_proj(x, g, w_in_t, ssd_params, later_weights):
    seq, d = x.shape
    n_blocks = IN_WIDE // PROJ_TN
    tn1 = PROJ_FIRST_TN
    assert SSD_IN_BLOCKS + SSD_STEPS <= n_blocks
    gap = SSD_HEADS
    if_rows = 2 * MLSTM_HEADS
    y_rows = SSD_PER_STEP * SSD_CHUNK
    y_step = lambda n, tn: jnp.clip(n - IN_WIDE_BEFORE_DT // tn, 0, SSD_STEPS - 1)
    ssd_consts = (*ssd_params, *_ssd_constants())
    assert len(ssd_consts) == SSD_N_CONSTS
    state_shape = (SSD_GROUPS, SSD_STATE, SSD_GROUP_WIDTH)
    tail_shape = (SSD_TAIL, SSD_CONV_WIDTH)
    scratch = lambda tn: [
        pltpu.VMEM((PROJ_TM, d), BF16),
        pltpu.VMEM((IN_WIDE_BEFORE_DT // tn, PROJ_TM, tn), BF16),
        pltpu.VMEM((SSD_TAIL + SSD_CHUNK, SSD_CONV_WIDTH), F32),
        pltpu.VMEM(state_shape, F32),
    ]
    full1 = lambda a: pl.BlockSpec(a.shape, lambda n: (0,) * a.ndim)
    proj, small, y_ssd, w_wide, w_small, state, tail = pl.pallas_call(
        _in_proj_first_kernel,
        grid=(IN_WIDE // tn1,),
        in_specs=[
            pl.BlockSpec((PROJ_TM, d), lambda n: (0, 0), pipeline_mode=pl.Buffered(1)),
            pl.BlockSpec((1, d), lambda n: (0, 0)),
            pl.BlockSpec((tn1, d), lambda n: (n, 0)),
            pl.BlockSpec((gap, d), lambda n: ((n + 1) * (tn1 // gap), 0)),
            pl.BlockSpec((gap, d), lambda n: (IN_WIDE_BEFORE_DT // gap, 0)),
            pl.BlockSpec((if_rows, d), lambda n: (IN_I_COL // if_rows, 0)),
            *[full1(a) for a in ssd_consts],
        ],
        out_specs=[
            pl.BlockSpec((PROJ_TM, tn1), lambda n: (0, n)),
            pl.BlockSpec((PROJ_TM, SMALL_WIDTH), lambda n: (0, 0)),
            pl.BlockSpec((y_rows, SSD_WIDTH), lambda n: (y_step(n, tn1), 0)),
            pl.BlockSpec((tn1, d), lambda n: (n, 0)),
            pl.BlockSpec((SMALL_WIDTH, d), lambda n: (0, 0)),
            pl.BlockSpec(state_shape, lambda n: (0, 0, 0)),
            pl.BlockSpec(tail_shape, lambda n: (0, 0)),
        ],
        out_shape=[
            jax.ShapeDtypeStruct((seq, IN_WIDE), BF16),
            jax.ShapeDtypeStruct((seq, SMALL_WIDTH), F32),
            jax.ShapeDtypeStruct((seq, SSD_WIDTH), BF16),
            jax.ShapeDtypeStruct((IN_WIDE, d), BF16),
            jax.ShapeDtypeStruct((SMALL_WIDTH, d), BF16),
            jax.ShapeDtypeStruct(state_shape, F32),
            jax.ShapeDtypeStruct(tail_shape, F32),
        ],
        scratch_shapes=scratch(tn1),
        compiler_params=pltpu.CompilerParams(
            dimension_semantics=("arbitrary",),
            vmem_limit_bytes=VMEM_LIMIT_BYTES),
        name="in_proj_first",
    )(x, g, w_in_t, w_in_t, w_in_t, w_in_t, *ssd_consts)

    m_tiles = seq // PROJ_TM - 1
    steps = m_tiles * n_blocks
    cast_specs = [_slab_spec(w.shape[0], w.shape[1], steps, lambda m, n: m * n_blocks + n)
                  for w in later_weights]
    full2 = lambda a: pl.BlockSpec(a.shape, lambda m, n: (0,) * a.ndim)
    n_in_before_alias = 4 + len(ssd_consts) + 2
    outs = pl.pallas_call(
        functools.partial(_in_proj_rest_kernel, len(later_weights)),
        grid=(m_tiles, n_blocks),
        in_specs=[
            pl.BlockSpec((PROJ_TM, d), lambda m, n: (m + 1, 0), pipeline_mode=pl.Buffered(1)),
            pl.BlockSpec((1, d), lambda m, n: (0, 0)),
            pl.BlockSpec((PROJ_TN, d), lambda m, n: (n, 0)),
            pl.BlockSpec((SMALL_WIDTH, d), lambda m, n: (0, 0)),
            *[full2(a) for a in ssd_consts],
            full2(state),
            full2(tail),
            pl.BlockSpec(memory_space=pl.ANY),
            pl.BlockSpec(memory_space=pl.ANY),
            pl.BlockSpec(memory_space=pl.ANY),
            *cast_specs,
        ],
        out_specs=[
            pl.BlockSpec((PROJ_TM, PROJ_TN), lambda m, n: (m + 1, n)),
            pl.BlockSpec((PROJ_TM, SMALL_WIDTH), lambda m, n: (m + 1, 0)),
            pl.BlockSpec((y_rows, SSD_WIDTH),
                         lambda m, n: ((m + 1) * SSD_STEPS + y_step(n, PROJ_TN), 0)),
            *cast_specs,
        ],
        out_shape=[
            jax.ShapeDtypeStruct((seq, IN_WIDE), BF16),
            jax.ShapeDtypeStruct((seq, SMALL_WIDTH), F32),
            jax.ShapeDtypeStruct((seq, SSD_WIDTH), BF16),
            *[jax.ShapeDtypeStruct(w.shape, BF16) for w in later_weights],
        ],
        scratch_shapes=scratch(PROJ_TN),
        input_output_aliases={n_in_before_alias: 0, n_in_before_alias + 1: 1,
                              n_in_before_alias + 2: 2},
        compiler_params=pltpu.CompilerParams(
            dimension_semantics=("arbitrary", "arbitrary"),
            vmem_limit_bytes=VMEM_LIMIT_BYTES),
        name="in_proj_rest",
    )(x, g, w_wide, w_small, *ssd_consts, state, tail, proj, small, y_ssd, *later_weights)
    return outs[0], outs[1], outs[2], outs[3:]


def _cummax_rows(x):
    rows = x.shape[0]
    row = lax.broadcasted_iota(jnp.int32, x.shape, 0)
    d = 1
    while d < rows:
        if d < V7X_SUBLANES:
            shifted = jnp.where(row < d, -jnp.inf, pltpu.roll(x, d, axis=0))
        else:
            shifted = jnp.concatenate(
                [jnp.full((d, x.shape[1]), -jnp.inf, x.dtype), x[:rows - d]], axis=0)
        x = jnp.maximum(x, shifted)
        d *= 2
    return x


def _mlstm_chunk(q_ref, k_ref, v0_ref, v1_ref, o0_ref, o1_ref, small_ref, ib_ref, fb_ref,
                 ng_ref, tri_ref, negmask_ref, ct_ref, nmat_ref, m_ref, h_ref, h_row0):
    T = MLSTM_SUB
    DK = MLSTM_QK_DIM
    DV = MLSTM_V_DIM
    log2_scale = math.log2(DK ** -0.5)
    half = MLSTM_HEADS // 2

    neg_mask = negmask_ref[...]
    half_ng = 0.5 * ng_ref[...]
    lane = lax.broadcasted_iota(jnp.int32, (T, SMALL_WIDTH), 1)
    head_lane = (lane >= F_LANE) & (lane < F_LANE + MLSTM_HEADS)
    lane_dk = lax.broadcasted_iota(jnp.int32, (DK, SMALL_WIDTH), 1)

    for sub in range(MLSTM_CHUNK // T):
        r0 = sub * T
        sm = small_ref[r0:r0 + T, :]
        log_i = pltpu.roll(_softcap(sm + ib_ref[...]) * LOG2E, F_LANE - I_LANE, axis=1)
        log_i = jnp.where(head_lane, log_i, 0.0)
        log_f = jnp.where(head_lane, _log_sigmoid(_softcap(sm + fb_ref[...])) * LOG2E, 0.0)
        bcum = _dot01_left(tri_ref[...], log_f)
        r_c = log_i - bcum
        m_prev = m_ref[0:1, :]
        inter_log = bcum + m_prev
        m_t = jnp.maximum(inter_log, bcum + _cummax_rows(r_c))
        col_term = bcum - m_t
        w_inter = jnp.exp2(inter_log - m_t)
        e_neg_m = jnp.exp2(-m_t)
        b_last = bcum[T - 1:T, :]
        a_c = b_last + r_c
        m_loc = jnp.max(a_c, axis=0, keepdims=True)
        m_new = jnp.maximum(b_last + m_prev, m_loc)
        s_old = jnp.exp2(b_last + m_prev - m_new)
        s_new = jnp.exp2(m_loc - m_new)
        w_rows = (jnp.exp2(a_c - m_loc + log2_scale) * s_new).T
        row_term = (r_c + log2_scale).T
        q_all = q_ref[r0:r0 + T, :]
        nq_inter = _dot(q_all, nmat_ref[...].astype(BF16))
        yield

        for h in range(MLSTM_HEADS):
            v_ref, o_ref = (v0_ref, o0_ref) if h < half else (v1_ref, o1_ref)
            hv = h % half
            hl = F_LANE + h
            q = q_all[:, h * DK:(h + 1) * DK]
            k = k_ref[r0:r0 + T, h * DK:(h + 1) * DK]
            v = v_ref[r0:r0 + T, hv * DV:(hv + 1) * DV]
            ct = ct_ref[h]

            kw = k.astype(F32).T * w_rows[hl:hl + 1, :]
            c_loc = _dot(kw.astype(BF16), v)
            n_loc = jnp.sum(kw, axis=1, keepdims=True)

            p = _dot_nt(q, k) * jnp.exp2(
                col_term[:, hl:hl + 1] + row_term[hl:hl + 1, :] + neg_mask)
            wi = w_inter[:, hl:hl + 1]
            num = _dot(p.astype(BF16), v) + wi * _dot(q, ct.astype(BF16))
            nq = jnp.sum(p, axis=1, keepdims=True) + wi * nq_inter[:, hl:hl + 1]
            inv_den = 1.0 / jnp.maximum(jnp.abs(nq), e_neg_m[:, hl:hl + 1])
            ms = jnp.mean(num * num, axis=-1, keepdims=True)
            out_scale = inv_den * lax.rsqrt(inv_den * inv_den * ms + EPS)
            hng = half_ng[:, h * DV:(h + 1) * DV]
            og = o_ref[r0:r0 + T, hv * DV:(hv + 1) * DV].astype(F32)
            gate = jnp.tanh(0.5 * og) * hng + hng
            h_ref[pl.ds(h_row0 + r0, T), h * DV:(h + 1) * DV] = (
                num * out_scale * gate).astype(h_ref.dtype)

            so = s_old[:, hl:hl + 1]
            ct_ref[h] = so * ct + c_loc
            n_rows = nmat_ref[h * DK:(h + 1) * DK, :]
            nmat_ref[h * DK:(h + 1) * DK, :] = so * n_rows + jnp.where(lane_dk == hl, n_loc, 0.0)
            yield

        m_ref[...] = jnp.broadcast_to(m_new, m_ref.shape)


def _mlstm_out_kernel(x_ref, ys_ref, wt_ref, wb_ref,
                      q_ref, k_ref, v0_ref, v1_ref, o0_ref, o1_ref, small_ref, ib_ref, fb_ref,
                      ng_ref, tri_ref, negmask_ref, wd_ref,
                      h1_ref, wd_bf_ref,
                      yml_even_ref, yml_odd_ref, ct_ref, nmat_ref, m_ref):
    m = pl.program_id(0)
    n = pl.program_id(1)
    last = pl.num_programs(0) - 1

    _cast_block(wd_ref, wd_bf_ref)

    @pl.when((m == 0) & (n == 0))
    def _():
        ct_ref[...] = jnp.zeros(ct_ref.shape, F32)
        nmat_ref[...] = jnp.zeros(nmat_ref.shape, F32)
        m_ref[...] = jnp.zeros(m_ref.shape, F32)

    def project(yml_ref):
        col_w = OUT_TN // 2
        k_w = 2 * SSD_WIDTH // (OUT_PIECES // 2)
        for piece in range(OUT_PIECES):
            cols = slice((piece % 2) * col_w, (piece % 2 + 1) * col_w)
            k0 = (piece // 2) * k_w
            src, w_ref = (ys_ref, wt_ref) if k0 < SSD_WIDTH else (yml_ref, wb_ref)
            k0 %= SSD_WIDTH
            part = _dot(src[:, k0:k0 + k_w], w_ref[k0:k0 + k_w, cols])
            if piece < 2:
                h1_ref[:, cols] = x_ref[:, cols] + part
            else:
                h1_ref[:, cols] += part
            yield

    def mix(yml_ref):
        return _mlstm_chunk(q_ref, k_ref, v0_ref, v1_ref, o0_ref, o1_ref, small_ref, ib_ref,
                            fb_ref, ng_ref, tri_ref, negmask_ref, ct_ref, nmat_ref, m_ref,
                            yml_ref, pl.multiple_of(n * MLSTM_CHUNK, MLSTM_CHUNK))

    seg_total = (MLSTM_CHUNK // MLSTM_SUB) * (MLSTM_HEADS + 1)

    @pl.when(m == 0)
    def _():
        for _ in mix(yml_even_ref):
            pass

    @pl.when((m > 0) & (m < last) & (m % 2 == 1))
    def _():
        _run_interleaved((project(yml_even_ref), OUT_PIECES), (mix(yml_odd_ref), seg_total),
                         ticks=OUT_PIECES)

    @pl.when((m > 0) & (m < last) & (m % 2 == 0))
    def _():
        _run_interleaved((project(yml_odd_ref), OUT_PIECES), (mix(yml_even_ref), seg_total),
                         ticks=OUT_PIECES)

    @pl.when(m == last)
    def _():
        for _ in project(yml_odd_ref):
            pass


def _mlstm_out(x, y_ssd, w_out_bf, proj, small, ib_row, fb_row, norm_g_row, w_down):
    seq, d = x.shape
    T = MLSTM_CHUNK
    tiles = seq // OUT_TM
    n_tiles = d // OUT_TN
    assert tiles % 2 == 0 and OUT_TM == n_tiles * T
    steps = (tiles + 1) * n_tiles
    half_w = MLSTM_WIDTH // 2
    q_blk = IN_WIDE_BEFORE_DT // MLSTM_QK_WIDTH
    v_blk = (IN_WIDE_BEFORE_DT + 2 * MLSTM_QK_WIDTH) // half_w
    tri, _, neg_mask = _causal_constants(MLSTM_SUB)
    consts = (ib_row, fb_row, norm_g_row, tri, neg_mask)
    full = lambda a: pl.BlockSpec(a.shape, lambda m, n: (0, 0))
    prev = lambda m: jnp.maximum(m - 1, 0)
    chunk = lambda m, n: jnp.minimum(m * n_tiles + n, seq // T - 1)
    step_of = lambda m, n: m * n_tiles + n
    cast_specs = [_slab_spec(w_down.shape[0], w_down.shape[1], steps, step_of)]
    return pl.pallas_call(
        _mlstm_out_kernel,
        grid=(tiles + 1, n_tiles),
        in_specs=[
            pl.BlockSpec((OUT_TM, OUT_TN), lambda m, n: (prev(m), n)),
            pl.BlockSpec((OUT_TM, SSD_WIDTH), lambda m, n: (prev(m), 0)),
            pl.BlockSpec((SSD_WIDTH, OUT_TN), lambda m, n: (0, n)),
            pl.BlockSpec((MLSTM_WIDTH, OUT_TN), lambda m, n: (1, n)),
            pl.BlockSpec((T, MLSTM_QK_WIDTH), lambda m, n: (chunk(m, n), q_blk)),
            pl.BlockSpec((T, MLSTM_QK_WIDTH), lambda m, n: (chunk(m, n), q_blk + 1)),
            pl.BlockSpec((T, half_w), lambda m, n: (chunk(m, n), v_blk)),
            pl.BlockSpec((T, half_w), lambda m, n: (chunk(m, n), v_blk + 1)),
            pl.BlockSpec((T, half_w), lambda m, n: (chunk(m, n), v_blk + 2)),
            pl.BlockSpec((T, half_w), lambda m, n: (chunk(m, n), v_blk + 3)),
            pl.BlockSpec((T, SMALL_WIDTH), lambda m, n: (chunk(m, n), 0)),
            *[full(a) for a in consts],
            *cast_specs,
        ],
        out_specs=[
            pl.BlockSpec((OUT_TM, OUT_TN), lambda m, n: (prev(m), jnp.where(m == 0, 0, n))),
            *cast_specs,
        ],
        out_shape=[
            jax.ShapeDtypeStruct((seq, d), F32),
            jax.ShapeDtypeStruct(w_down.shape, BF16),
        ],
        scratch_shapes=[
            pltpu.VMEM((OUT_TM, MLSTM_WIDTH), BF16),
            pltpu.VMEM((OUT_TM, MLSTM_WIDTH), BF16),
            pltpu.VMEM((MLSTM_HEADS, MLSTM_QK_DIM, MLSTM_V_DIM), F32),
            pltpu.VMEM((MLSTM_QK_WIDTH, SMALL_WIDTH), F32),
            pltpu.VMEM((V7X_SUBLANES, SMALL_WIDTH), F32),
        ],
        compiler_params=pltpu.CompilerParams(
            dimension_semantics=("arbitrary", "arbitrary"),
            vmem_limit_bytes=VMEM_LIMIT_BYTES),
        name="mlstm_out",
    )(x, y_ssd, w_out_bf, w_out_bf, proj, proj, proj, proj, proj, proj, small, *consts, w_down)


def _ffn_kernel(h_ref, hn_ref, g_ref, wg_ref, wu_ref, wd_ref, gf_ref, out_ref, u_even_ref, u_odd_ref):
    m = pl.program_id(0)
    f = pl.program_id(1)
    n_chunks = h_ref.shape[0] // NORM_ROWS

    @pl.when((m == 0) & (f == 0))
    def _():
        _rmsnorm_rows(h_ref, g_ref, u_even_ref, h_ref.shape[0])

    def step(u_ref, u_next_ref, first):
        u = u_ref[...]
        sub = wg_ref.shape[1] // FFN_SPLIT
        acc = None
        for s in range(FFN_SPLIT):
            gate = _dot(u, wg_ref[:, s * sub:(s + 1) * sub])
            up = _dot(u, wu_ref[:, s * sub:(s + 1) * sub])
            a = (gate * _sigmoid(gate) * up).astype(BF16)
            part = _dot(a, wd_ref[s * sub:(s + 1) * sub, :])
            acc = part if acc is None else acc + part
            if s == 0:
                r0 = pl.multiple_of((f % n_chunks) * NORM_ROWS, NORM_ROWS)
                x = hn_ref[pl.ds(r0, NORM_ROWS), :]
                y = x * lax.rsqrt(jnp.mean(x * x, axis=-1, keepdims=True) + EPS)
                u_next_ref[pl.ds(r0, NORM_ROWS), :] = (y * g_ref[...]).astype(u_next_ref.dtype)
        base_ref = h_ref if first else out_ref
        out_ref[...] = base_ref[...] + acc

    for parity, (u_ref, u_next_ref) in enumerate(((u_even_ref, u_odd_ref), (u_odd_ref, u_even_ref))):
        for first in (True, False):
            @pl.when((m % 2 == parity) & ((f == 0) == first))
            def _():
                step(u_ref, u_next_ref, first)

    @pl.when(f == pl.num_programs(1) - 1)
    def _():
        _rmsnorm_rows(out_ref, gf_ref, out_ref, out_ref.shape[0])


def _ffn(h1, g, w_gate, w_up, w_down, g_final):
    seq, d = h1.shape
    d_ff = w_gate.shape[1]
    grid = (seq // FFN_TM, d_ff // FFN_TF)
    assert grid[1] >= FFN_TM // NORM_ROWS
    return pl.pallas_call(
        _ffn_kernel,
        grid=grid,
        in_specs=[
            pl.BlockSpec((FFN_TM, d), lambda m, f: (m, 0)),
            pl.BlockSpec((FFN_TM, d), lambda m, f: (jnp.minimum(m + 1, grid[0] - 1), 0)),
            pl.BlockSpec((1, d), lambda m, f: (0, 0)),
            pl.BlockSpec((d, FFN_TF), lambda m, f: (0, f)),
            pl.BlockSpec((d, FFN_TF), lambda m, f: (0, f)),
            pl.BlockSpec((FFN_TF, d), lambda m, f: (f, 0)),
            pl.BlockSpec((1, d), lambda m, f: (0, 0)),
        ],
        out_specs=pl.BlockSpec((FFN_TM, d), lambda m, f: (m, 0)),
        out_shape=jax.ShapeDtypeStruct((seq, d), F32),
        scratch_shapes=[pltpu.VMEM((FFN_TM, d), BF16), pltpu.VMEM((FFN_TM, d), BF16)],
        compiler_params=pltpu.CompilerParams(
            dimension_semantics=("arbitrary", "arbitrary"),
            vmem_limit_bytes=VMEM_LIMIT_BYTES),
        name="ffn",
    )(h1, h1, g, w_gate, w_up, w_down, g_final)


def _pad_lanes(vec, lane0):
    return jnp.zeros((1, SMALL_WIDTH), F32).at[0, lane0:lane0 + vec.shape[0]].set(vec.astype(F32))


def _layer(h, norm_mix_g, w_in, conv_w, conv_b, dt_bias, a_log, d_skip, ssd_norm_g,
           i_bias, f_bias, mlstm_norm_g, w_out, norm_ffn_g, w_gate, w_up, w_down, out_g):
    d = h.shape[1]
    ssd_params = (conv_w.astype(F32), conv_b.reshape(1, -1).astype(F32),
                  _pad_lanes(dt_bias, DT_LANE), _pad_lanes(a_log, DT_LANE),
                  jnp.repeat(d_skip.astype(F32), SSD_HEAD_DIM).reshape(1, -1),
                  ssd_norm_g.reshape(1, -1).astype(F32))
    proj, small, y_ssd, (w_out_bf, wg_bf, wu_bf) = _in_proj(
        h, norm_mix_g.reshape(1, d), w_in.T, ssd_params, (w_out, w_gate, w_up))

    h1, wd_bf = _mlstm_out(
        h, y_ssd, w_out_bf, proj, small, _pad_lanes(i_bias, I_LANE), _pad_lanes(f_bias, F_LANE),
        mlstm_norm_g.reshape(1, -1).astype(F32), w_down)
    return _ffn(h1, norm_ffn_g.reshape(1, d), wg_bf, wu_bf, wd_bf, out_g.reshape(1, d))


def kernel(x, norm_mix_g, w_in, conv_w, conv_b, dt_bias, a_log, d_skip, ssd_norm_g, i_bias,
           f_bias, mlstm_norm_g, w_out, norm_ffn_g, w_gate, w_up, w_down, final_norm_g):
    batch, seq, d = x.shape
    depth = w_in.shape[0]
    assert batch == 1 and depth == 1, "single-sequence, single-layer problem"
    out = _layer(x.reshape(seq, d), norm_mix_g[0], w_in[0], conv_w[0], conv_b[0], dt_bias[0],
                 a_log[0], d_skip[0], ssd_norm_g[0], i_bias[0], f_bias[0], mlstm_norm_g[0],
                 w_out[0], norm_ffn_g[0], w_gate[0], w_up[0], w_down[0], final_norm_g)
    return out.reshape(batch, seq, d)
```
